```python
import math
import jax, jax.numpy as jnp
from jax import lax
import numpy as np

D_MODEL = 2048
BATCH = 2
SEQ = 8192
DEPTH = 1

HEAD_DIM = 128
MIX_WIDTH = D_MODEL
RET_WIDTH = MIX_WIDTH // 2
ATTN_WIDTH = MIX_WIDTH - RET_WIDTH
RET_HEADS = RET_WIDTH // HEAD_DIM
ATTN_HEADS = ATTN_WIDTH // HEAD_DIM
ATTN_KV_HEADS = ATTN_HEADS // 4
KV_WIDTH = ATTN_KV_HEADS * HEAD_DIM
IN_WIDTH = 4 * RET_WIDTH + ATTN_WIDTH + 2 * KV_WIDTH
RET_CHUNK = 128
Q_BLOCK = 128
GRID_W = 64
AXIS_DIM = HEAD_DIM // 2
ROPE_THETA = 10000.0
MEM_TOKENS = 256
CROSS_HEADS = 4
CROSS_HEAD_DIM = 128
CROSS_WIDTH = CROSS_HEADS * CROSS_HEAD_DIM
D_FF = 4 * D_MODEL
NORM_EPS = 1e-6

kernel_name = "hybrid_retention_gqa_encoder_block"


def rms_norm(x, w):
    xf = x.astype(jnp.float32)
    y = xf * lax.rsqrt(jnp.mean(xf * xf, axis=-1, keepdims=True) + NORM_EPS)
    return (y * w.astype(jnp.float32)).astype(x.dtype)


def axial_rope_tables(seq_len):
    rows = seq_len // GRID_W
    row = jnp.repeat(jnp.arange(rows, dtype=jnp.float32), GRID_W)
    col = jnp.tile(jnp.arange(GRID_W, dtype=jnp.float32), rows)
    inv_freq = 1.0 / (ROPE_THETA ** (jnp.arange(0, AXIS_DIM, 2, dtype=jnp.float32) / AXIS_DIM))
    ang_r = row[:, None] * inv_freq[None, :]
    ang_c = col[:, None] * inv_freq[None, :]
    return (jnp.cos(ang_r), jnp.sin(ang_r), jnp.cos(ang_c), jnp.sin(ang_c))


def _rope_half(x, cos, sin):
    x1, x2 = jnp.split(x, 2, axis=-1)
    cos = cos.astype(x.dtype)
    sin = sin.astype(x.dtype)
    return jnp.concatenate([x1 * cos - x2 * sin, x2 * cos + x1 * sin], axis=-1)


def apply_axial_rope(x, rope):
    cos_r, sin_r, cos_c, sin_c = rope
    xr, xc = jnp.split(x, 2, axis=-1)
    return jnp.concatenate([_rope_half(xr, cos_r, sin_r), _rope_half(xc, cos_c, sin_c)], axis=-1)


def retention_one_direction(q, k, v, log_gamma, strict):
    B, H, S, dk = q.shape
    dv = v.shape[-1]
    n_chunks = S // RET_CHUNK

    def to_chunks(t):
        return t.reshape(B, H, n_chunks, RET_CHUNK, t.shape[-1]).transpose(2, 0, 1, 3, 4)

    qc, kc, vc = to_chunks(q), to_chunks(k), to_chunks(v)
    idx = jnp.arange(RET_CHUNK, dtype=jnp.float32)
    rel = idx[:, None] - idx[None, :]
    mask = rel > 0 if strict else rel >= 0
    lg = log_gamma[:, None, None]
    decay_inner = jnp.where(mask[None], jnp.exp(lg * jnp.where(mask, rel, 0.0)[None]), 0.0)
    decay_query = jnp.exp(log_gamma[:, None] * (idx + 1.0)[None, :])[..., None]
    decay_key = jnp.exp(log_gamma[:, None] * (RET_CHUNK - 1.0 - idx)[None, :])[..., None]
    decay_chunk = jnp.exp(log_gamma * RET_CHUNK)[:, None, None]

    def step(state, inp):
        q_i, k_i, v_i = inp
        scores = jnp.einsum('bhnd,bhmd->bhnm', q_i, k_i) * decay_inner
        out = jnp.einsum('bhnm,bhmv->bhnv', scores, v_i)
        out = out + jnp.einsum('bhnd,bhdv->bhnv', q_i, state) * decay_query
        state = state * decay_chunk + jnp.einsum('bhmd,bhmv->bhdv', k_i * decay_key, v_i)
        return state, out

    state0 = jnp.zeros((B, H, dk, dv), jnp.float32)
    _, out = lax.scan(step, state0, (qc, kc, vc))
    return out.transpose(1, 2, 0, 3, 4).reshape(B, H, S, dv)


def bidirectional_retention(q, k, v, log_gamma_fwd, log_gamma_bwd):
    fwd = retention_one_direction(q, k, v, log_gamma_fwd, strict=False)
    flip = lambda t: jnp.flip(t, axis=2)
    bwd = flip(retention_one_direction(flip(q), flip(k), flip(v), log_gamma_bwd, strict=True))
    return fwd + bwd


def block_gqa_attention(q, k, v):
    B, H, S, d = q.shape
    kvh = k.shape[1]
    groups = H // kvh
    n_blocks = S // Q_BLOCK
    scale = d ** -0.5
    qb = q.reshape(B, kvh, groups, n_blocks, Q_BLOCK, d).transpose(3, 0, 1, 2, 4, 5)

    def one_block(q_blk):
        s = jnp.einsum('bkgqd,bksd->bkgqs', q_blk, k).astype(jnp.float32) * scale
        p = jax.nn.softmax(s, axis=-1)
        return jnp.einsum('bkgqs,bksd->bkgqd', p.astype(v.dtype), v)

    o = lax.map(one_block, qb)
    return o.transpose(1, 0, 4, 2, 3, 5).reshape(B, S, H * d)


def hybrid_mixer(h, w_in, ret_decay_fwd, ret_decay_bwd, ret_gn_w, ret_gn_b,
                 attn_q_norm_w, attn_k_norm_w, w_out, rope):
    B, S, _ = h.shape
    proj = h @ w_in
    c1 = RET_WIDTH
    c2 = 2 * RET_WIDTH
    c3 = 3 * RET_WIDTH
    c4 = 4 * RET_WIDTH
    c5 = c4 + ATTN_WIDTH
    c6 = c5 + KV_WIDTH
    rq, rk, rv, rg, aq, ak, av = jnp.split(proj, [c1, c2, c3, c4, c5, c6], axis=-1)

    def heads(t, n):
        return t.reshape(B, S, n, HEAD_DIM).transpose(0, 2, 1, 3)

    rq = apply_axial_rope(heads(rq, RET_HEADS), rope).astype(jnp.float32)
    rk = (apply_axial_rope(heads(rk, RET_HEADS), rope).astype(jnp.float32)) * (HEAD_DIM ** -0.5)
    rv = heads(rv, RET_HEADS).astype(jnp.float32)
    log_g_f = -jnp.exp(ret_decay_fwd.astype(jnp.float32))
    log_g_b = -jnp.exp(ret_decay_bwd.astype(jnp.float32))
    y = bidirectional_retention(rq, rk, rv, log_g_f, log_g_b)
    mu = jnp.mean(y, axis=-1, keepdims=True)
    var = jnp.mean(jnp.square(y - mu), axis=-1, keepdims=True)
    y = ((y - mu) * lax.rsqrt(var + NORM_EPS)).transpose(0, 2, 1, 3).reshape(B, S, RET_WIDTH)
    y = y * ret_gn_w.astype(jnp.float32) + ret_gn_b.astype(jnp.float32)
    y_ret = (jax.nn.silu(rg.astype(jnp.float32)) * y).astype(h.dtype)

    aq = apply_axial_rope(rms_norm(heads(aq, ATTN_HEADS), attn_q_norm_w), rope)
    ak = apply_axial_rope(rms_norm(heads(ak, ATTN_KV_HEADS), attn_k_norm_w), rope)
    av = heads(av, ATTN_KV_HEADS)
    y_attn = block_gqa_attention(aq, ak, av)

    return jnp.concatenate([y_ret, y_attn], axis=-1) @ w_out


def memory_cross_attention(h, m, wq, wk, wv, wo):
    B, S, _ = h.shape
    M = m.shape[1]
    q = (h @ wq).reshape(B, S, CROSS_HEADS, CROSS_HEAD_DIM)
    k = (m @ wk).reshape(B, M, CROSS_HEADS, CROSS_HEAD_DIM)
    v = (m @ wv).reshape(B, M, CROSS_HEADS, CROSS_HEAD_DIM)
    s = jnp.einsum('bshd,bmhd->bhsm', q, k).astype(jnp.float32) * (CROSS_HEAD_DIM ** -0.5)
    p = jax.nn.softmax(s, axis=-1)
    o = jnp.einsum('bhsm,bmhd->bshd', p.astype(v.dtype), v).reshape(B, S, CROSS_WIDTH)
    return o @ wo


def setup_inputs(seed: int = 0) -> dict:
    key = jax.random.key(seed)
    ks = jax.random.split(key, 24)

    def w(k, shape, fan_in):
        return jax.random.normal(k, shape, jnp.float32) * (fan_in ** -0.5)

    def gain(k, shape):
        return 1.0 + 0.02 * jax.random.normal(k, shape, jnp.float32)

    base = jnp.log(-jnp.log1p(-(2.0 ** (-(5.0 + jnp.arange(RET_HEADS, dtype=jnp.float32))))))
    return {
        "x": jax.random.normal(ks[0], (BATCH, SEQ, D_MODEL), jnp.float32),
        "mem": jax.random.normal(ks[1], (BATCH, MEM_TOKENS, D_MODEL), jnp.float32),
        "norm_mix_w": gain(ks[2], (DEPTH, D_MODEL)),
        "w_in": w(ks[3], (DEPTH, D_MODEL, IN_WIDTH), D_MODEL),
        "ret_decay_fwd": base[None, :] + 0.05 * jax.random.normal(ks[4], (DEPTH, RET_HEADS), jnp.float32),
        "ret_decay_bwd": base[None, :] + 0.05 * jax.random.normal(ks[5], (DEPTH, RET_HEADS), jnp.float32),
        "ret_gn_w": gain(ks[6], (DEPTH, RET_WIDTH)),
        "ret_gn_b": 0.02 * jax.random.normal(ks[7], (DEPTH, RET_WIDTH), jnp.float32),
        "attn_q_norm_w": gain(ks[8], (DEPTH, HEAD_DIM)),
        "attn_k_norm_w": gain(ks[9], (DEPTH, HEAD_DIM)),
        "w_out": w(ks[10], (DEPTH, MIX_WIDTH, D_MODEL), MIX_WIDTH),
        "norm_cross_w": gain(ks[11], (DEPTH, D_MODEL)),
        "norm_mem_w": gain(ks[12], (DEPTH, D_MODEL)),
        "w_cross_q": w(ks[13], (DEPTH, D_MODEL, CROSS_WIDTH), D_MODEL),
        "w_cross_k": w(ks[14], (DEPTH, D_MODEL, CROSS_WIDTH), D_MODEL),
        "w_cross_v": w(ks[15], (DEPTH, D_MODEL, CROSS_WIDTH), D_MODEL),
        "w_cross_o": w(ks[16], (DEPTH, CROSS_WIDTH, D_MODEL), CROSS_WIDTH),
        "norm_mlp_w": gain(ks[17], (DEPTH, D_MODEL)),
        "w_mlp_up": w(ks[18], (DEPTH, D_MODEL, D_FF), D_MODEL),
        "w_mlp_down": w(ks[19], (DEPTH, D_FF, D_MODEL), D_FF),
        "norm_final_w": gain(ks[20], (D_MODEL,)),
    }


def reference(x, mem, norm_mix_w, w_in, ret_decay_fwd, ret_decay_bwd, ret_gn_w, ret_gn_b,
              attn_q_norm_w, attn_k_norm_w, w_out, norm_cross_w, norm_mem_w,
              w_cross_q, w_cross_k, w_cross_v, w_cross_o, norm_mlp_w,
              w_mlp_up, w_mlp_down, norm_final_w):
    rope = axial_rope_tables(x.shape[1])
    for l in range(DEPTH):
        h = rms_norm(x, norm_mix_w[l])
        x = x + hybrid_mixer(h, w_in[l], ret_decay_fwd[l], ret_decay_bwd[l], ret_gn_w[l], ret_gn_b[l],
                             attn_q_norm_w[l], attn_k_norm_w[l], w_out[l], rope)
        h = rms_norm(x, norm_cross_w[l])
        m = rms_norm(mem, norm_mem_w[l])
        x = x + memory_cross_attention(h, m, w_cross_q[l], w_cross_k[l], w_cross_v[l], w_cross_o[l])
        h = rms_norm(x, norm_mlp_w[l])
        x = x + jnp.square(jax.nn.relu(h @ w_mlp_up[l])) @ w_mlp_down[l]
    return rms_norm(x, norm_final_w)
```

```python
import functools
import math

import jax
import jax.numpy as jnp
from jax import lax
from jax.experimental import pallas as pl
from jax.experimental.pallas import tpu as pltpu

D_MODEL = 2048
HEAD_DIM = 128
RET_WIDTH = 1024
ATTN_WIDTH = 1024
RET_HEADS = 8
ATTN_HEADS = 8
ATTN_KV_HEADS = 2
GQA_GROUP = ATTN_HEADS // ATTN_KV_HEADS
KV_WIDTH = ATTN_KV_HEADS * HEAD_DIM
IN_WIDTH = 4 * RET_WIDTH + ATTN_WIDTH + 2 * KV_WIDTH
GRID_W = 64
AXIS_DIM = HEAD_DIM // 2
ROPE_THETA = 10000.0
MEM_TOKENS = 256
CROSS_HEADS = 4
CROSS_HEAD_DIM = 128
CROSS_WIDTH = CROSS_HEADS * CROSS_HEAD_DIM
D_FF = 4 * D_MODEL
NORM_EPS = 1e-6

V7X_LANES = 128
V7X_BF16_SUBLANES = 16
V7X_VMEM_BYTES = 64 * 1024 * 1024

F32 = jnp.float32
BF16 = jnp.bfloat16
LOG2E = math.log2(math.e)
NEG_BIG = -1e30

IN_TM = 1024
IN_TN = 512
IN_RC = 256
RET_C = 256
ATT_TQ = 512
ATT_TK = 512
ATT_ONES = V7X_BF16_SUBLANES
OC_TM = 512
OC_RC = 256
MLP_TM = 512
MLP_TF = 1024
MLP_RC = 256


def _vmem_limit(nbytes):
    return int(min(nbytes + 16 * 1024 * 1024, V7X_VMEM_BYTES - 8 * 1024 * 1024))


def _rms_scale(y):
    return lax.rsqrt(jnp.mean(y * y, axis=-1, keepdims=True) + NORM_EPS)


def _rope(y, cos, sin_signed, first_half):
    swapped = jnp.where(first_half, pltpu.roll(y, 96, 1), pltpu.roll(y, 32, 1))
    return y * cos + swapped * sin_signed


def _in_proj_kernel(x_ref, nw_ref, w_ref, cos_ref, sin_ref, qg_ref, kg_ref, o_ref, h_ref):
    j = pl.program_id(1)

    @pl.when(j == 0)
    def _():
        def body(r, c):
            rows = pl.ds(pl.multiple_of(r * IN_RC, IN_RC), IN_RC)
            xr = x_ref[rows, :]
            h_ref[rows, :] = (xr * _rms_scale(xr) * nw_ref[...]).astype(BF16)
            return c
        lax.fori_loop(0, IN_TM // IN_RC, body, 0)

    lane = lax.broadcasted_iota(jnp.int32, (IN_RC, HEAD_DIM), 1)
    first_half = (lane % AXIS_DIM) < (AXIS_DIM // 2)
    ret_k_scale = HEAD_DIM ** -0.5
    attn_q_scale = (HEAD_DIM ** -0.5) * LOG2E

    def run(epilogue):
        def body(r, c):
            rows = pl.ds(pl.multiple_of(r * IN_RC, IN_RC), IN_RC)
            acc = jnp.dot(h_ref[rows, :], w_ref[...], preferred_element_type=F32)
            cos = cos_ref[rows, :]
            sin = sin_ref[rows, :]
            for s in range(IN_TN // HEAD_DIM):
                cols = slice(s * HEAD_DIM, (s + 1) * HEAD_DIM)
                o_ref[rows, cols] = epilogue(s, acc[:, cols], cos, sin).astype(BF16)
            return c
        lax.fori_loop(0, IN_TM // IN_RC, body, 0)

    def ret_q(s, y, cos, sin):
        return _rope(y, cos, sin, first_half)

    def ret_k(s, y, cos, sin):
        return _rope(y, cos, sin, first_half) * ret_k_scale

    def plain(s, y, cos, sin):
        return y

    def attn_q(s, y, cos, sin):
        y = y * _rms_scale(y) * qg_ref[...]
        return _rope(y, cos, sin, first_half) * attn_q_scale

    def attn_kv(s, y, cos, sin):
        if s < ATTN_KV_HEADS:
            y = y * _rms_scale(y) * kg_ref[...]
            return _rope(y, cos, sin, first_half)
        return y

    nb = RET_WIDTH // IN_TN
    pl.when(j < nb)(lambda: run(ret_q))
    pl.when((j >= nb) & (j < 2 * nb))(lambda: run(ret_k))
    pl.when((j >= 2 * nb) & (j < 4 * nb))(lambda: run(plain))
    pl.when((j >= 4 * nb) & (j < 5 * nb))(lambda: run(attn_q))
    pl.when(j == 5 * nb)(lambda: run(attn_kv))


def _in_proj(x2d, norm_w, w_in, cos_t, sin_t, q_gain, k_gain, seq):
    n = x2d.shape[0]
    assert IN_WIDTH % IN_TN == 0 and 2 * KV_WIDTH == IN_TN and n % IN_TM == 0 and seq % IN_TM == 0
    tiles_per_seq = seq // IN_TM
    vmem = (2 * IN_TM * D_MODEL * 4 + IN_TM * D_MODEL * 2 + 2 * D_MODEL * IN_TN * 2
            + 2 * IN_TM * IN_TN * 2 + 4 * IN_TM * HEAD_DIM * 4)
    return pl.pallas_call(
        _in_proj_kernel,
        out_shape=jax.ShapeDtypeStruct((n, IN_WIDTH), BF16),
        grid=(n // IN_TM, IN_WIDTH // IN_TN),
        in_specs=[
            pl.BlockSpec((IN_TM, D_MODEL), lambda i, j: (i, 0)),
            pl.BlockSpec((1, D_MODEL), lambda i, j: (0, 0)),
            pl.BlockSpec((D_MODEL, IN_TN), lambda i, j: (0, j)),
            pl.BlockSpec((IN_TM, HEAD_DIM), lambda i, j: (i % tiles_per_seq, 0)),
            pl.BlockSpec((IN_TM, HEAD_DIM), lambda i, j: (i % tiles_per_seq, 0)),
            pl.BlockSpec((1, HEAD_DIM), lambda i, j: (0, 0)),
            pl.BlockSpec((1, HEAD_DIM), lambda i, j: (0, 0)),
        ],
        out_specs=pl.BlockSpec((IN_TM, IN_TN), lambda i, j: (i, j)),
        scratch_shapes=[pltpu.VMEM((IN_TM, D_MODEL), BF16)],
        compiler_params=pltpu.CompilerParams(
            dimension_semantics=("arbitrary", "arbitrary"),
            vmem_limit_bytes=_vmem_limit(vmem)),
        name="in_proj",
    )(x2d, norm_w, w_in, cos_t, sin_t, q_gain, k_gain)


def _retention_kernel(dec_ref, q_ref, k_ref, v_ref, g_ref, gw_ref, gb_ref, o_ref,
                      d_ref, tab_ref, sb_ref, *, n_chunks):
    c_len = RET_C
    lg_f = -jnp.exp(dec_ref[0, 0:1, :])
    lg_b = -jnp.exp(dec_ref[0, 1:2, :])
    row = lax.broadcasted_iota(jnp.int32, (c_len, c_len), 0)
    col = lax.broadcasted_iota(jnp.int32, (c_len, c_len), 1)
    rel = (row - col).astype(F32)
    d_ref[...] = jnp.exp(jnp.where(rel >= 0, lg_f * rel, -lg_b * rel))
    idx = lax.broadcasted_iota(jnp.int32, (c_len, HEAD_DIM), 0).astype(F32)
    lf = lg_f[:, :HEAD_DIM]
    lb = lg_b[:, :HEAD_DIM]
    tab_ref[0] = jnp.exp(lf * (idx + 1.0))
    tab_ref[1] = jnp.exp(lb * (c_len - idx))
    tab_ref[2] = jnp.exp(lf * (c_len - 1.0 - idx))
    tab_ref[3] = jnp.exp(lb * idx)
    chunk_f = jnp.exp(lf * c_len)
    chunk_b = jnp.exp(lb * c_len)
    contract_rows = (((0,), (0,)), ((), ()))
    contract_cols = (((1,), (1,)), ((), ()))

    def chunk_rows(c):
        return pl.ds(pl.multiple_of(c * c_len, c_len), c_len)

    def backward(t, state):
        c = n_chunks - 1 - t
        rows = chunk_rows(c)
        sb_ref[c] = state.astype(BF16)
        kd = (k_ref[rows, :].astype(F32) * tab_ref[3]).astype(BF16)
        upd = lax.dot_general(kd, v_ref[rows, :], contract_rows, preferred_element_type=F32)
        return state * chunk_b + upd

    lax.fori_loop(0, n_chunks, backward, jnp.zeros((HEAD_DIM, HEAD_DIM), F32))

    def forward(c, state):
        rows = chunk_rows(c)
        q = q_ref[rows, :]
        k = k_ref[rows, :]
        v = v_ref[rows, :]
        scores = lax.dot_general(q, k, contract_cols, preferred_element_type=F32)
        masked = (scores * d_ref[...]).astype(BF16)
        qf = q.astype(F32)
        lhs = jnp.concatenate(
            [masked, (qf * tab_ref[0]).astype(BF16), (qf * tab_ref[1]).astype(BF16)], axis=1)
        rhs = jnp.concatenate([v, state.astype(BF16), sb_ref[c]], axis=0)
        out = jnp.dot(lhs, rhs, preferred_element_type=F32)
        mu = jnp.mean(out, axis=-1, keepdims=True)
        cen = out - mu
        var = jnp.mean(cen * cen, axis=-1, keepdims=True)
        y = cen * lax.rsqrt(var + NORM_EPS) * gw_ref[...] + gb_ref[...]
        g = g_ref[rows, :].astype(F32)
        y = y * (g / (1.0 + jnp.exp(-g)))
        o_ref[rows, :] = y.astype(BF16)
        kd = (k.astype(F32) * tab_ref[2]).astype(BF16)
        upd = lax.dot_general(kd, v, contract_rows, preferred_element_type=F32)
        return state * chunk_f + upd

    lax.fori_loop(0, n_chunks, forward, jnp.zeros((HEAD_DIM, HEAD_DIM), F32))


def _retention(proj, dec, gn_w, gn_b, batch, seq):
    n = proj.shape[0]
    assert seq % RET_C == 0
    n_chunks = seq // RET_C
    heads_per_group = RET_WIDTH // HEAD_DIM

    def col_block(group):
        return lambda b, h: (b, group * heads_per_group + h)

    vmem = (2 * 5 * seq * HEAD_DIM * 2 + RET_C * RET_C * 4 + 4 * RET_C * HEAD_DIM * 4
            + n_chunks * HEAD_DIM * HEAD_DIM * 2)
    return pl.pallas_call(
        functools.partial(_retention_kernel, n_chunks=n_chunks),
        out_shape=jax.ShapeDtypeStruct((n, RET_WIDTH), BF16),
        grid=(batch, RET_HEADS),
        in_specs=[
            pl.BlockSpec((1, 2, RET_C), lambda b, h: (h, 0, 0)),
            pl.BlockSpec((seq, HEAD_DIM), col_block(0)),
            pl.BlockSpec((seq, HEAD_DIM), col_block(1)),
            pl.BlockSpec((seq, HEAD_DIM), col_block(2)),
            pl.BlockSpec((seq, HEAD_DIM), col_block(3)),
            pl.BlockSpec((1, HEAD_DIM), lambda b, h: (0, h)),
            pl.BlockSpec((1, HEAD_DIM), lambda b, h: (0, h)),
        ],
        out_specs=pl.BlockSpec((seq, HEAD_DIM), lambda b, h: (b, h)),
        scratch_shapes=[
            pltpu.VMEM((RET_C, RET_C), F32),
            pltpu.VMEM((4, RET_C, HEAD_DIM), F32),
            pltpu.VMEM((n_chunks, HEAD_DIM, HEAD_DIM), BF16),
        ],
        compiler_params=pltpu.CompilerParams(
            dimension_semantics=("arbitrary", "arbitrary"),
            vmem_limit_bytes=_vmem_limit(vmem)),
        name="retention",
    )(dec, proj, proj, proj, proj, gn_w, gn_b)


def _attention_kernel(q_ref, k_ref, v_ref, o_ref, vt_ref, acc_ref, m_ref, *, n_kv):
    i = pl.program_id(2)

    @pl.when(i == 0)
    def _():
        ones = jnp.ones((ATT_ONES, ATT_TK), BF16)

        def body(c, carry):
            rows = pl.ds(pl.multiple_of(c * ATT_TK, ATT_TK), ATT_TK)
            vt_ref[c, 0:HEAD_DIM, :] = v_ref[rows, :].astype(F32).T.astype(BF16)
            vt_ref[c, HEAD_DIM:HEAD_DIM + ATT_ONES, :] = ones
            return carry
        lax.fori_loop(0, n_kv, body, 0)

    contract_cols = (((1,), (1,)), ((), ()))
    for h in range(GQA_GROUP):
        cols = slice(h * HEAD_DIM, (h + 1) * HEAD_DIM)
        m_ref[...] = jnp.full(m_ref.shape, NEG_BIG, F32)
        acc_ref[...] = jnp.zeros(acc_ref.shape, F32)

        def kv_step(c, carry):
            rows = pl.ds(pl.multiple_of(c * ATT_TK, ATT_TK), ATT_TK)
            s_t = lax.dot_general(k_ref[rows, :], q_ref[:, cols], contract_cols,
                                  preferred_element_type=F32)
            m_old = m_ref[...]
            m_new = jnp.maximum(m_old, jnp.max(s_t, axis=0, keepdims=True))
            alpha = jnp.exp2(m_old - m_new)
            p_t = jnp.exp2(s_t - m_new).astype(BF16)
            acc_ref[...] = acc_ref[...] * alpha + jnp.dot(vt_ref[c], p_t,
                                                          preferred_element_type=F32)
            m_ref[...] = m_new
            return carry
        lax.fori_loop(0, n_kv, kv_step, 0)

        acc = acc_ref[...]
        o_t = acc[0:HEAD_DIM, :] * (1.0 / acc[HEAD_DIM:HEAD_DIM + 1, :])
        o_ref[:, cols] = o_t.T.astype(BF16)


def _attention(proj, batch, seq):
    n = proj.shape[0]
    assert seq % ATT_TQ == 0 and seq % ATT_TK == 0
    n_q = seq // ATT_TQ
    n_kv = seq // ATT_TK
    group_w = GQA_GROUP * HEAD_DIM
    q_col0 = 4 * RET_WIDTH // group_w
    k_col0 = (4 * RET_WIDTH + ATTN_WIDTH) // HEAD_DIM
    v_col0 = k_col0 + ATTN_KV_HEADS
    vmem = (2 * ATT_TQ * group_w * 2 * 2 + 2 * 2 * seq * HEAD_DIM * 2
            + (HEAD_DIM + ATT_ONES) * seq * 2 + (HEAD_DIM + ATT_ONES + 8) * ATT_TQ * 4
            + 3 * ATT_TK * ATT_TQ * 4)
    return pl.pallas_call(
        functools.partial(_attention_kernel, n_kv=n_kv),
        out_shape=jax.ShapeDtypeStruct((n, ATTN_WIDTH), BF16),
        grid=(batch, ATTN_KV_HEADS, n_q),
        in_specs=[
            pl.BlockSpec((ATT_TQ, group_w), lambda b, g, i: (b * n_q + i, q_col0 + g)),
            pl.BlockSpec((seq, HEAD_DIM), lambda b, g, i: (b, k_col0 + g)),
            pl.BlockSpec((seq, HEAD_DIM), lambda b, g, i: (b, v_col0 + g)),
        ],
        out_specs=pl.BlockSpec((ATT_TQ, group_w), lambda b, g, i: (b * n_q + i, g)),
        scratch_shapes=[
            pltpu.VMEM((n_kv, HEAD_DIM + ATT_ONES, ATT_TK), BF16),
            pltpu.VMEM((HEAD_DIM + ATT_ONES, ATT_TQ), F32),
            pltpu.VMEM((1, ATT_TQ), F32),
        ],
        compiler_params=pltpu.CompilerParams(
            dimension_semantics=("arbitrary", "arbitrary", "arbitrary"),
            vmem_limit_bytes=_vmem_limit(vmem)),
        name="attention",
    )(proj, proj, proj)


def _mem_kv_kernel(m_ref, nw_ref, w_ref, o_ref):
    m = m_ref[...]
    h = (m * _rms_scale(m) * nw_ref[...]).astype(BF16)
    o_ref[...] = jnp.dot(h, w_ref[...], preferred_element_type=F32).astype(BF16)


def _mem_kv(mem2d, norm_w, w_kv, batch):
    rows = mem2d.shape[0] // batch
    vmem = 2 * rows * D_MODEL * 4 + 2 * D_MODEL * 2 * CROSS_WIDTH * 2 + 2 * rows * 2 * CROSS_WIDTH * 2
    return pl.pallas_call(
        _mem_kv_kernel,
        out_shape=jax.ShapeDtypeStruct((mem2d.shape[0], 2 * CROSS_WIDTH), BF16),
        grid=(batch,),
        in_specs=[
            pl.BlockSpec((rows, D_MODEL), lambda b: (b, 0)),
            pl.BlockSpec((1, D_MODEL), lambda b: (0, 0)),
            pl.BlockSpec((D_MODEL, 2 * CROSS_WIDTH), lambda b: (0, 0)),
        ],
        out_specs=pl.BlockSpec((rows, 2 * CROSS_WIDTH), lambda b: (b, 0)),
        compiler_params=pltpu.CompilerParams(
            dimension_semantics=("arbitrary",),
            vmem_limit_bytes=_vmem_limit(vmem)),
        name="mem_kv",
    )(mem2d, norm_w, w_kv)


def _out_cross_kernel(x_ref, yr_ref, ya_ref, wo_ref, nw_ref, wq_ref, kv_ref, wco_ref, o_ref):
    contract_cols = (((1,), (1,)), ((), ()))
    scale = CROSS_HEAD_DIM ** -0.5

    def body(r, carry):
        rows = pl.ds(pl.multiple_of(r * OC_RC, OC_RC), OC_RC)
        x1 = (x_ref[rows, :]
              + jnp.dot(yr_ref[rows, :], wo_ref[0:RET_WIDTH, :], preferred_element_type=F32)
              + jnp.dot(ya_ref[rows, :], wo_ref[RET_WIDTH:, :], preferred_element_type=F32))
        h = (x1 * _rms_scale(x1) * nw_ref[...]).astype(BF16)
        q = jnp.dot(h, wq_ref[...], preferred_element_type=F32).astype(BF16)
        heads = []
        for hd in range(CROSS_HEADS):
            cols = slice(hd * CROSS_HEAD_DIM, (hd + 1) * CROSS_HEAD_DIM)
            k = kv_ref[:, cols]
            v = kv_ref[:, CROSS_WIDTH + hd * CROSS_HEAD_DIM:CROSS_WIDTH + (hd + 1) * CROSS_HEAD_DIM]
            s = lax.dot_general(q[:, cols], k, contract_cols, preferred_element_type=F32) * scale
            e = jnp.exp(s - jnp.max(s, axis=-1, keepdims=True))
            p = (e / jnp.sum(e, axis=-1, keepdims=True)).astype(BF16)
            heads.append(jnp.dot(p, v, preferred_element_type=F32).astype(BF16))
        o = jnp.concatenate(heads, axis=1)
        o_ref[rows, :] = x1 + jnp.dot(o, wco_ref[...], preferred_element_type=F32)
        return carry
    lax.fori_loop(0, OC_TM // OC_RC, body, 0)


def _out_cross(x2d, y_ret, y_attn, w_out, norm_w, w_q, mem_kv, w_co, seq):
    n = x2d.shape[0]
    assert n % OC_TM == 0 and seq % OC_TM == 0
    tiles_per_seq = seq // OC_TM
    vmem = (2 * 2 * OC_TM * D_MODEL * 4 + 2 * 2 * OC_TM * RET_WIDTH * 2 + 2 * D_MODEL * D_MODEL * 2
            + 2 * 2 * D_MODEL * CROSS_WIDTH * 2 + 2 * MEM_TOKENS * 2 * CROSS_WIDTH * 2)
    return pl.pallas_call(
        _out_cross_kernel,
        out_shape=jax.ShapeDtypeStruct((n, D_MODEL), F32),
        grid=(n // OC_TM,),
        in_specs=[
            pl.BlockSpec((OC_TM, D_MODEL), lambda i: (i, 0)),
            pl.BlockSpec((OC_TM, RET_WIDTH), lambda i: (i, 0)),
            pl.BlockSpec((OC_TM, ATTN_WIDTH), lambda i: (i, 0)),
            pl.BlockSpec((D_MODEL, D_MODEL), lambda i: (0, 0)),
            pl.BlockSpec((1, D_MODEL), lambda i: (0, 0)),
            pl.BlockSpec((D_MODEL, CROSS_WIDTH), lambda i: (0, 0)),
            pl.BlockSpec((MEM_TOKENS, 2 * CROSS_WIDTH), lambda i: (i // tiles_per_seq, 0)),
            pl.BlockSpec((CROSS_WIDTH, D_MODEL), lambda i: (0, 0)),
        ],
        out_specs=pl.BlockSpec((OC_TM, D_MODEL), lambda i: (i, 0)),
        compiler_params=pltpu.CompilerParams(
            dimension_semantics=("arbitrary",),
            vmem_limit_bytes=_vmem_limit(vmem)),
        name="out_cross",
    )(x2d, y_ret, y_attn, w_out, norm_w, w_q, mem_kv, w_co)


def _mlp_kernel(x_ref, nw_ref, wu_ref, wd_ref, fw_ref, o_ref, h_ref):
    j = pl.program_id(1)

    def row_chunk(r):
        return pl.ds(pl.multiple_of(r * MLP_RC, MLP_RC), MLP_RC)

    @pl.when(j == 0)
    def _():
        def body(r, c):
            rows = row_chunk(r)
            xr = x_ref[rows, :]
            h_ref[rows, :] = (xr * _rms_scale(xr) * nw_ref[...]).astype(BF16)
            o_ref[rows, :] = xr
            return c
        lax.fori_loop(0, MLP_TM // MLP_RC, body, 0)

    def body(r, c):
        rows = row_chunk(r)
        u = jnp.dot(h_ref[rows, :], wu_ref[...], preferred_element_type=F32)
        u = jnp.maximum(u, 0.0)
        a = (u * u).astype(BF16)
        o_ref[rows, :] += jnp.dot(a, wd_ref[...], preferred_element_type=F32)
        return c
    lax.fori_loop(0, MLP_TM // MLP_RC, body, 0)

    @pl.when(j == pl.num_programs(1) - 1)
    def _():
        def body(r, c):
            rows = row_chunk(r)
            y = o_ref[rows, :]
            o_ref[rows, :] = y * _rms_scale(y) * fw_ref[...]
            return c
        lax.fori_loop(0, MLP_TM // MLP_RC, body, 0)


def _mlp(x2d, norm_w, w_up, w_down, final_w):
    n = x2d.shape[0]
    assert n % MLP_TM == 0 and D_FF % MLP_TF == 0
    vmem = (2 * 2 * MLP_TM * D_MODEL * 4 + MLP_TM * D_MODEL * 2 + 2 * 2 * D_MODEL * MLP_TF * 2)
    return pl.pallas_call(
        _mlp_kernel,
        out_shape=jax.ShapeDtypeStruct((n, D_MODEL), F32),
        grid=(n // MLP_TM, D_FF // MLP_TF),
        in_specs=[
            pl.BlockSpec((MLP_TM, D_MODEL), lambda i, j: (i, 0)),
            pl.BlockSpec((1, D_MODEL), lambda i, j: (0, 0)),
            pl.BlockSpec((D_MODEL, MLP_TF), lambda i, j: (0, j)),
            pl.BlockSpec((MLP_TF, D_MODEL), lambda i, j: (j, 0)),
            pl.BlockSpec((1, D_MODEL), lambda i, j: (0, 0)),
        ],
        out_specs=pl.BlockSpec((MLP_TM, D_MODEL), lambda i, j: (i, 0)),
        scratch_shapes=[pltpu.VMEM((MLP_TM, D_MODEL), BF16)],
        compiler_params=pltpu.CompilerParams(
            dimension_semantics=("arbitrary", "arbitrary"),
            vmem_limit_bytes=_vmem_limit(vmem)),
        name="mlp",
    )(x2d, norm_w, w_up, w_down, final_w)


def _rope_tables(seq):
    t = jnp.arange(seq, dtype=jnp.int32)
    row = (t // GRID_W).astype(F32)
    col = (t % GRID_W).astype(F32)
    inv_freq = 1.0 / (ROPE_THETA ** (jnp.arange(0, AXIS_DIM, 2, dtype=F32) / AXIS_DIM))
    ang_r = row[:, None] * inv_freq[None, :]
    ang_c = col[:, None] * inv_freq[None, :]
    cos_t = jnp.concatenate([jnp.cos(ang_r), jnp.cos(ang_r), jnp.cos(ang_c), jnp.cos(ang_c)], axis=-1)
    sin_t = jnp.concatenate([-jnp.sin(ang_r), jnp.sin(ang_r), -jnp.sin(ang_c), jnp.sin(ang_c)], axis=-1)
    return cos_t, sin_t


def kernel(x, mem, norm_mix_w, w_in, ret_decay_fwd, ret_decay_bwd, ret_gn_w, ret_gn_b, attn_q_norm_w, attn_k_norm_w, w_out, norm_cross_w, norm_mem_w, w_cross_q, w_cross_k, w_cross_v, w_cross_o, norm_mlp_w, w_mlp_up, w_mlp_down, norm_final_w):
    batch, seq, _ = x.shape
    assert w_in.shape[0] == 1, "single-layer block: per-layer parameters have a leading axis of 1"
    cos_t, sin_t = _rope_tables(seq)
    xs = x.reshape(batch * seq, D_MODEL)
    mem2d = mem.reshape(batch * MEM_TOKENS, D_MODEL)
    dec = jnp.stack([ret_decay_fwd[0], ret_decay_bwd[0]], axis=1)
    dec = jnp.broadcast_to(dec[:, :, None], (RET_HEADS, 2, RET_C)).astype(F32)
    proj = _in_proj(xs, norm_mix_w, w_in[0].astype(BF16), cos_t, sin_t,
                    attn_q_norm_w, attn_k_norm_w, seq)
    y_ret = _retention(proj, dec, ret_gn_w, ret_gn_b, batch, seq)
    y_attn = _attention(proj, batch, seq)
    w_kv = jnp.concatenate([w_cross_k[0], w_cross_v[0]], axis=1).astype(BF16)
    mem_kv = _mem_kv(mem2d, norm_mem_w, w_kv, batch)
    xs = _out_cross(xs, y_ret, y_attn, w_out[0].astype(BF16), norm_cross_w,
                    w_cross_q[0].astype(BF16), mem_kv, w_cross_o[0].astype(BF16), seq)
    xs = _mlp(xs, norm_mlp_w, w_mlp_up[0].astype(BF16), w_mlp_down[0].astype(BF16),
              norm_final_w[None, :])
    return xs.reshape(batch, seq, D_MODEL)
```

```python
import functools
import math

import jax
import jax.numpy as jnp
from jax import lax
from jax.experimental import pallas as pl
from jax.experimental.pallas import tpu as pltpu

D_MODEL = 2048
HEAD_DIM = 128
RET_WIDTH = 1024
ATTN_WIDTH = 1024
RET_HEADS = 8
ATTN_HEADS = 8
ATTN_KV_HEADS = 2
GQA_GROUP = ATTN_HEADS // ATTN_KV_HEADS
KV_WIDTH = ATTN_KV_HEADS * HEAD_DIM
IN_WIDTH = 4 * RET_WIDTH + ATTN_WIDTH + 2 * KV_WIDTH
GRID_W = 64
AXIS_DIM = HEAD_DIM // 2
ROPE_THETA = 10000.0
MEM_TOKENS = 256
CROSS_HEADS = 4
CROSS_HEAD_DIM = 128
CROSS_WIDTH = CROSS_HEADS * CROSS_HEAD_DIM
D_FF = 4 * D_MODEL
NORM_EPS = 1e-6

V7X_LANES = 128
V7X_BF16_SUBLANES = 16
V7X_VMEM_BYTES = 64 * 1024 * 1024

F32 = jnp.float32
BF16 = jnp.bfloat16
LOG2E = math.log2(math.e)
NEG_BIG = -1e30

IN_TM = 1024
IN_TN = 512
IN_RC = 256
RET_C = 256
ATT_TQ = 512
ATT_TK = 4096
ATT_ONES = V7X_BF16_SUBLANES
ATT_DENOM_FLOOR = 2.0 ** -40
OC_TM = 512
OC_RC = 256
MLP_TM = 512
MLP_TF = 1024
MLP_RC = 256


def _vmem_limit(nbytes):
    return int(min(nbytes + 16 * 1024 * 1024, V7X_VMEM_BYTES - 8 * 1024 * 1024))


def _rms_scale(y):
    return lax.rsqrt(jnp.mean(y * y, axis=-1, keepdims=True) + NORM_EPS)


def _rope(y, cos, sin_signed, first_half):
    swapped = jnp.where(first_half, pltpu.roll(y, 96, 1), pltpu.roll(y, 32, 1))
    return y * cos + swapped * sin_signed


def _in_proj_kernel(x_ref, nw_ref, w_ref, cos_ref, sin_ref, qg_ref, kg_ref, o_ref, h_ref):
    j = pl.program_id(1)

    @pl.when(j == 0)
    def _():
        def body(r, c):
            rows = pl.ds(pl.multiple_of(r * IN_RC, IN_RC), IN_RC)
            xr = x_ref[rows, :]
            h_ref[rows, :] = (xr * _rms_scale(xr) * nw_ref[...]).astype(BF16)
            return c
        lax.fori_loop(0, IN_TM // IN_RC, body, 0)

    lane = lax.broadcasted_iota(jnp.int32, (IN_RC, HEAD_DIM), 1)
    first_half = (lane % AXIS_DIM) < (AXIS_DIM // 2)
    ret_k_scale = HEAD_DIM ** -0.5
    attn_q_scale = (HEAD_DIM ** -0.5) * LOG2E

    def run(epilogue):
        def body(r, c):
            rows = pl.ds(pl.multiple_of(r * IN_RC, IN_RC), IN_RC)
            acc = jnp.dot(h_ref[rows, :], w_ref[...], preferred_element_type=F32)
            cos = cos_ref[rows, :]
            sin = sin_ref[rows, :]
            for s in range(IN_TN // HEAD_DIM):
                cols = slice(s * HEAD_DIM, (s + 1) * HEAD_DIM)
                o_ref[rows, cols] = epilogue(s, acc[:, cols], cos, sin).astype(BF16)
            return c
        lax.fori_loop(0, IN_TM // IN_RC, body, 0)

    def ret_q(s, y, cos, sin):
        return _rope(y, cos, sin, first_half)

    def ret_k(s, y, cos, sin):
        return _rope(y, cos, sin, first_half) * ret_k_scale

    def plain(s, y, cos, sin):
        return y

    def attn_q(s, y, cos, sin):
        y = y * _rms_scale(y) * qg_ref[...]
        return _rope(y, cos, sin, first_half) * attn_q_scale

    def attn_kv(s, y, cos, sin):
        if s < ATTN_KV_HEADS:
            y = y * _rms_scale(y) * kg_ref[...]
            return _rope(y, cos, sin, first_half)
        return y

    nb = RET_WIDTH // IN_TN
    pl.when(j < nb)(lambda: run(ret_q))
    pl.when((j >= nb) & (j < 2 * nb))(lambda: run(ret_k))
    pl.when((j >= 2 * nb) & (j < 4 * nb))(lambda: run(plain))
    pl.when((j >= 4 * nb) & (j < 5 * nb))(lambda: run(attn_q))
    pl.when(j == 5 * nb)(lambda: run(attn_kv))


def _in_proj(x2d, norm_w, w_in, cos_t, sin_t, q_gain, k_gain, seq):
    n = x2d.shape[0]
    assert IN_WIDTH % IN_TN == 0 and 2 * KV_WIDTH == IN_TN and n % IN_TM == 0 and seq % IN_TM == 0
    tiles_per_seq = seq // IN_TM
    vmem = (2 * IN_TM * D_MODEL * 4 + IN_TM * D_MODEL * 2 + 2 * D_MODEL * IN_TN * 2
            + 2 * IN_TM * IN_TN * 2 + 4 * IN_TM * HEAD_DIM * 4)
    return pl.pallas_call(
        _in_proj_kernel,
        out_shape=jax.ShapeDtypeStruct((n, IN_WIDTH), BF16),
        grid=(n // IN_TM, IN_WIDTH // IN_TN),
        in_specs=[
            pl.BlockSpec((IN_TM, D_MODEL), lambda i, j: (i, 0)),
            pl.BlockSpec((1, D_MODEL), lambda i, j: (0, 0)),
            pl.BlockSpec((D_MODEL, IN_TN), lambda i, j: (0, j)),
            pl.BlockSpec((IN_TM, HEAD_DIM), lambda i, j: (i % tiles_per_seq, 0)),
            pl.BlockSpec((IN_TM, HEAD_DIM), lambda i, j: (i % tiles_per_seq, 0)),
            pl.BlockSpec((1, HEAD_DIM), lambda i, j: (0, 0)),
            pl.BlockSpec((1, HEAD_DIM), lambda i, j: (0, 0)),
        ],
        out_specs=pl.BlockSpec((IN_TM, IN_TN), lambda i, j: (i, j)),
        scratch_shapes=[pltpu.VMEM((IN_TM, D_MODEL), BF16)],
        compiler_params=pltpu.CompilerParams(
            dimension_semantics=("arbitrary", "arbitrary"),
            vmem_limit_bytes=_vmem_limit(vmem)),
        name="in_proj",
    )(x2d, norm_w, w_in, cos_t, sin_t, q_gain, k_gain)


def _retention_kernel(dec_ref, q_ref, k_ref, v_ref, g_ref, gw_ref, gb_ref, o_ref,
                      d_ref, tab_ref, sb_ref, *, n_chunks):
    c_len = RET_C
    lg_f = -jnp.exp(dec_ref[0, 0:1, :])
    lg_b = -jnp.exp(dec_ref[0, 1:2, :])
    row = lax.broadcasted_iota(jnp.int32, (c_len, c_len), 0)
    col = lax.broadcasted_iota(jnp.int32, (c_len, c_len), 1)
    rel = (row - col).astype(F32)
    d_ref[...] = jnp.exp(jnp.where(rel >= 0, lg_f * rel, -lg_b * rel))
    idx = lax.broadcasted_iota(jnp.int32, (c_len, HEAD_DIM), 0).astype(F32)
    lf = lg_f[:, :HEAD_DIM]
    lb = lg_b[:, :HEAD_DIM]
    tab_ref[0] = jnp.exp(lf * (idx + 1.0))
    tab_ref[1] = jnp.exp(lb * (c_len - idx))
    tab_ref[2] = jnp.exp(lf * (c_len - 1.0 - idx))
    tab_ref[3] = jnp.exp(lb * idx)
    chunk_f = jnp.exp(lf * c_len)
    chunk_b = jnp.exp(lb * c_len)
    contract_rows = (((0,), (0,)), ((), ()))
    contract_cols = (((1,), (1,)), ((), ()))

    def chunk_rows(c):
        return pl.ds(pl.multiple_of(c * c_len, c_len), c_len)

    def backward(t, state):
        c = n_chunks - 1 - t
        rows = chunk_rows(c)
        sb_ref[c] = state.astype(BF16)
        kd = (k_ref[rows, :].astype(F32) * tab_ref[3]).astype(BF16)
        upd = lax.dot_general(kd, v_ref[rows, :], contract_rows, preferred_element_type=F32)
        return state * chunk_b + upd

    lax.fori_loop(0, n_chunks, backward, jnp.zeros((HEAD_DIM, HEAD_DIM), F32))

    def forward(c, state):
        rows = chunk_rows(c)
        q = q_ref[rows, :]
        k = k_ref[rows, :]
        v = v_ref[rows, :]
        scores = lax.dot_general(q, k, contract_cols, preferred_element_type=F32)
        masked = (scores * d_ref[...]).astype(BF16)
        qf = q.astype(F32)
        lhs = jnp.concatenate(
            [masked, (qf * tab_ref[0]).astype(BF16), (qf * tab_ref[1]).astype(BF16)], axis=1)
        rhs = jnp.concatenate([v, state.astype(BF16), sb_ref[c]], axis=0)
        out = jnp.dot(lhs, rhs, preferred_element_type=F32)
        mu = jnp.mean(out, axis=-1, keepdims=True)
        cen = out - mu
        var = jnp.mean(cen * cen, axis=-1, keepdims=True)
        y = cen * lax.rsqrt(var + NORM_EPS) * gw_ref[...] + gb_ref[...]
        g = g_ref[rows, :].astype(F32)
        y = y * (g / (1.0 + jnp.exp(-g)))
        o_ref[rows, :] = y.astype(BF16)
        kd = (k.astype(F32) * tab_ref[2]).astype(BF16)
        upd = lax.dot_general(kd, v, contract_rows, preferred_element_type=F32)
        return state * chunk_f + upd

    lax.fori_loop(0, n_chunks, forward, jnp.zeros((HEAD_DIM, HEAD_DIM), F32))


def _retention(proj, dec, gn_w, gn_b, batch, seq):
    n = proj.shape[0]
    assert seq % RET_C == 0
    n_chunks = seq // RET_C
    heads_per_group = RET_WIDTH // HEAD_DIM

    def col_block(group):
        return lambda b, h: (b, group * heads_per_group + h)

    vmem = (2 * 5 * seq * HEAD_DIM * 2 + RET_C * RET_C * 4 + 4 * RET_C * HEAD_DIM * 4
            + n_chunks * HEAD_DIM * HEAD_DIM * 2)
    return pl.pallas_call(
        functools.partial(_retention_kernel, n_chunks=n_chunks),
        out_shape=jax.ShapeDtypeStruct((n, RET_WIDTH), BF16),
        grid=(batch, RET_HEADS),
        in_specs=[
            pl.BlockSpec((1, 2, RET_C), lambda b, h: (h, 0, 0)),
            pl.BlockSpec((seq, HEAD_DIM), col_block(0)),
            pl.BlockSpec((seq, HEAD_DIM), col_block(1)),
            pl.BlockSpec((seq, HEAD_DIM), col_block(2)),
            pl.BlockSpec((seq, HEAD_DIM), col_block(3)),
            pl.BlockSpec((1, HEAD_DIM), lambda b, h: (0, h)),
            pl.BlockSpec((1, HEAD_DIM), lambda b, h: (0, h)),
        ],
        out_specs=pl.BlockSpec((seq, HEAD_DIM), lambda b, h: (b, h)),
        scratch_shapes=[
            pltpu.VMEM((RET_C, RET_C), F32),
            pltpu.VMEM((4, RET_C, HEAD_DIM), F32),
            pltpu.VMEM((n_chunks, HEAD_DIM, HEAD_DIM), BF16),
        ],
        compiler_params=pltpu.CompilerParams(
            dimension_semantics=("arbitrary", "arbitrary"),
            vmem_limit_bytes=_vmem_limit(vmem)),
        name="retention",
    )(dec, proj, proj, proj, proj, gn_w, gn_b)


def _attention_kernel(q_ref, k_ref, v_ref, o_ref, vt_ref, kmax_ref, qa_ref, acc_ref, *, n_kv):
    i = pl.program_id(2)
    contract_cols = (((1,), (1,)), ((), ()))

    def kv_rows(c):
        return pl.ds(pl.multiple_of(c * ATT_TK, ATT_TK), ATT_TK)

    @pl.when(i == 0)
    def _():
        ones = jnp.ones((ATT_ONES, ATT_TK), BF16)

        def body(c, k2max):
            rows = kv_rows(c)
            vt_ref[c, 0:HEAD_DIM, :] = v_ref[rows, :].astype(F32).T.astype(BF16)
            vt_ref[c, HEAD_DIM:HEAD_DIM + ATT_ONES, :] = ones
            kc = k_ref[rows, :].astype(F32)
            k2 = jnp.max(jnp.sum(kc * kc, axis=1, keepdims=True), axis=0, keepdims=True)
            return jnp.maximum(k2max, k2)
        k2max = lax.fori_loop(0, n_kv, body, jnp.zeros((1, 1), F32))
        kmax_ref[...] = jnp.broadcast_to(jnp.sqrt(k2max), kmax_ref.shape)

    q_lane = lax.broadcasted_iota(jnp.int32, (ATT_TQ, HEAD_DIM), 1)
    k_lane = lax.broadcasted_iota(jnp.int32, (ATT_TK, HEAD_DIM), 1)
    k_one = jnp.where(k_lane == 0, 1.0, 0.0).astype(BF16)
    head_cols = [slice(h * HEAD_DIM, (h + 1) * HEAD_DIM) for h in range(GQA_GROUP)]

    def set_shift(h, shift_col):
        qa_ref[h, :, HEAD_DIM:] = jnp.where(q_lane == 0, -shift_col, 0.0).astype(BF16)

    def sweep():
        acc_ref[...] = jnp.zeros(acc_ref.shape, F32)

        def kv_step(c, carry):
            k_aug = jnp.concatenate([k_ref[kv_rows(c), :], k_one], axis=1)
            vt = vt_ref[c]
            for h in range(GQA_GROUP):
                s_t = lax.dot_general(k_aug, qa_ref[h], contract_cols,
                                      preferred_element_type=F32)
                p_t = jnp.exp2(s_t).astype(BF16)
                acc_ref[h] += jnp.dot(vt, p_t, preferred_element_type=F32)
            return carry
        lax.fori_loop(0, n_kv, kv_step, 0)

    for h in range(GQA_GROUP):
        qh = q_ref[:, head_cols[h]]
        qa_ref[h, :, 0:HEAD_DIM] = qh
        qf = qh.astype(F32)
        q_norm = jnp.sqrt(jnp.sum(qf * qf, axis=1, keepdims=True))
        set_shift(h, q_norm * kmax_ref[0:1, 0:1])
    sweep()

    denom_min = jnp.min(acc_ref[0, HEAD_DIM:HEAD_DIM + 1, :])
    for h in range(1, GQA_GROUP):
        denom_min = jnp.minimum(denom_min, jnp.min(acc_ref[h, HEAD_DIM:HEAD_DIM + 1, :]))

    @pl.when(denom_min < ATT_DENOM_FLOOR)
    def _():
        for h in range(GQA_GROUP):
            def body(c, m):
                s = lax.dot_general(q_ref[:, head_cols[h]], k_ref[kv_rows(c), :], contract_cols,
                                    preferred_element_type=F32)
                return jnp.maximum(m, jnp.max(s, axis=1, keepdims=True))
            set_shift(h, lax.fori_loop(0, n_kv, body, jnp.full((ATT_TQ, 1), NEG_BIG, F32)))
        sweep()

    for h in range(GQA_GROUP):
        acc = acc_ref[h]
        o_t = acc[0:HEAD_DIM, :] * (1.0 / acc[HEAD_DIM:HEAD_DIM + 1, :])
        o_ref[:, head_cols[h]] = o_t.T.astype(BF16)


def _attention(proj, batch, seq):
    n = proj.shape[0]
    assert seq % ATT_TQ == 0 and seq % ATT_TK == 0
    n_q = seq // ATT_TQ
    n_kv = seq // ATT_TK
    group_w = GQA_GROUP * HEAD_DIM
    q_col0 = 4 * RET_WIDTH // group_w
    k_col0 = (4 * RET_WIDTH + ATTN_WIDTH) // HEAD_DIM
    v_col0 = k_col0 + ATTN_KV_HEADS
    vmem = (2 * ATT_TQ * group_w * 2 * 2 + 2 * 2 * seq * HEAD_DIM * 2
            + (HEAD_DIM + ATT_ONES) * seq * 2 + GQA_GROUP * (HEAD_DIM + ATT_ONES) * ATT_TQ * 4
            + GQA_GROUP * ATT_TQ * 2 * HEAD_DIM * 2 + 2 * GQA_GROUP * ATT_TK * ATT_TQ * 4)
    return pl.pallas_call(
        functools.partial(_attention_kernel, n_kv=n_kv),
        out_shape=jax.ShapeDtypeStruct((n, ATTN_WIDTH), BF16),
        grid=(batch, ATTN_KV_HEADS, n_q),
        in_specs=[
            pl.BlockSpec((ATT_TQ, group_w), lambda b, g, i: (b * n_q + i, q_col0 + g)),
            pl.BlockSpec((seq, HEAD_DIM), lambda b, g, i: (b, k_col0 + g)),
            pl.BlockSpec((seq, HEAD_DIM), lambda b, g, i: (b, v_col0 + g)),
        ],
        out_specs=pl.BlockSpec((ATT_TQ, group_w), lambda b, g, i: (b * n_q + i, g)),
        scratch_shapes=[
            pltpu.VMEM((n_kv, HEAD_DIM + ATT_ONES, ATT_TK), BF16),
            pltpu.VMEM((8, HEAD_DIM), F32),
            pltpu.VMEM((GQA_GROUP, ATT_TQ, 2 * HEAD_DIM), BF16),
            pltpu.VMEM((GQA_GROUP, HEAD_DIM + ATT_ONES, ATT_TQ), F32),
        ],
        compiler_params=pltpu.CompilerParams(
            dimension_semantics=("arbitrary", "arbitrary", "arbitrary"),
            vmem_limit_bytes=_vmem_limit(vmem)),
        name="attention",
    )(proj, proj, proj)


def _mem_kv_kernel(m_ref, nw_ref, w_ref, o_ref):
    m = m_ref[...]
    h = (m * _rms_scale(m) * nw_ref[...]).astype(BF16)
    o_ref[...] = jnp.dot(h, w_ref[...], preferred_element_type=F32).astype(BF16)


def _mem_kv(mem2d, norm_w, w_kv, batch):
    rows = mem2d.shape[0] // batch
    vmem = 2 * rows * D_MODEL * 4 + 2 * D_MODEL * 2 * CROSS_WIDTH * 2 + 2 * rows * 2 * CROSS_WIDTH * 2
    return pl.pallas_call(
        _mem_kv_kernel,
        out_shape=jax.ShapeDtypeStruct((mem2d.shape[0], 2 * CROSS_WIDTH), BF16),
        grid=(batch,),
        in_specs=[
            pl.BlockSpec((rows, D_MODEL), lambda b: (b, 0)),
            pl.BlockSpec((1, D_MODEL), lambda b: (0, 0)),
            pl.BlockSpec((D_MODEL, 2 * CROSS_WIDTH), lambda b: (0, 0)),
        ],
        out_specs=pl.BlockSpec((rows, 2 * CROSS_WIDTH), lambda b: (b, 0)),
        compiler_params=pltpu.CompilerParams(
            dimension_semantics=("arbitrary",),
            vmem_limit_bytes=_vmem_limit(vmem)),
        name="mem_kv",
    )(mem2d, norm_w, w_kv)


def _out_cross_kernel(x_ref, yr_ref, ya_ref, wo_ref, nw_ref, wq_ref, kv_ref, wco_ref, o_ref):
    contract_cols = (((1,), (1,)), ((), ()))
    scale = CROSS_HEAD_DIM ** -0.5

    def body(r, carry):
        rows = pl.ds(pl.multiple_of(r * OC_RC, OC_RC), OC_RC)
        x1 = (x_ref[rows, :]
              + jnp.dot(yr_ref[rows, :], wo_ref[0:RET_WIDTH, :], preferred_element_type=F32)
              + jnp.dot(ya_ref[rows, :], wo_ref[RET_WIDTH:, :], preferred_element_type=F32))
        h = (x1 * _rms_scale(x1) * nw_ref[...]).astype(BF16)
        q = jnp.dot(h, wq_ref[...], preferred_element_type=F32).astype(BF16)
        heads = []
        for hd in range(CROSS_HEADS):
            cols = slice(hd * CROSS_HEAD_DIM, (hd + 1) * CROSS_HEAD_DIM)
            k = kv_ref[:, cols]
            v = kv_ref[:, CROSS_WIDTH + hd * CROSS_HEAD_DIM:CROSS_WIDTH + (hd + 1) * CROSS_HEAD_DIM]
            s = lax.dot_general(q[:, cols], k, contract_cols, preferred_element_type=F32) * scale
            e = jnp.exp(s - jnp.max(s, axis=-1, keepdims=True))
            p = (e / jnp.sum(e, axis=-1, keepdims=True)).astype(BF16)
            heads.append(jnp.dot(p, v, preferred_element_type=F32).astype(BF16))
        o = jnp.concatenate(heads, axis=1)
        o_ref[rows, :] = x1 + jnp.dot(o, wco_ref[...], preferred_element_type=F32)
        return carry
    lax.fori_loop(0, OC_TM // OC_RC, body, 0)


def _out_cross(x2d, y_ret, y_attn, w_out, norm_w, w_q, mem_kv, w_co, seq):
    n = x2d.shape[0]
    assert n % OC_TM == 0 and seq % OC_TM == 0
    tiles_per_seq = seq // OC_TM
    vmem = (2 * 2 * OC_TM * D_MODEL * 4 + 2 * 2 * OC_TM * RET_WIDTH * 2 + 2 * D_MODEL * D_MODEL * 2
            + 2 * 2 * D_MODEL * CROSS_WIDTH * 2 + 2 * MEM_TOKENS * 2 * CROSS_WIDTH * 2)
    return pl.pallas_call(
        _out_cross_kernel,
        out_shape=jax.ShapeDtypeStruct((n, D_MODEL), F32),
        grid=(n // OC_TM,),
        in_specs=[
            pl.BlockSpec((OC_TM, D_MODEL), lambda i: (i, 0)),
            pl.BlockSpec((OC_TM, RET_WIDTH), lambda i: (i, 0)),
            pl.BlockSpec((OC_TM, ATTN_WIDTH), lambda i: (i, 0)),
            pl.BlockSpec((D_MODEL, D_MODEL), lambda i: (0, 0)),
            pl.BlockSpec((1, D_MODEL), lambda i: (0, 0)),
            pl.BlockSpec((D_MODEL, CROSS_WIDTH), lambda i: (0, 0)),
            pl.BlockSpec((MEM_TOKENS, 2 * CROSS_WIDTH), lambda i: (i // tiles_per_seq, 0)),
            pl.BlockSpec((CROSS_WIDTH, D_MODEL), lambda i: (0, 0)),
        ],
        out_specs=pl.BlockSpec((OC_TM, D_MODEL), lambda i: (i, 0)),
        compiler_params=pltpu.CompilerParams(
            dimension_semantics=("arbitrary",),
            vmem_limit_bytes=_vmem_limit(vmem)),
        name="out_cross",
    )(x2d, y_ret, y_attn, w_out, norm_w, w_q, mem_kv, w_co)


def _mlp_kernel(x_ref, nw_ref, wu_ref, wd_ref, fw_ref, o_ref, h_ref):
    j = pl.program_id(1)

    def row_chunk(r):
        return pl.ds(pl.multiple_of(r * MLP_RC, MLP_RC), MLP_RC)

    @pl.when(j == 0)
    def _():
        def body(r, c):
            rows = row_chunk(r)
            xr = x_ref[rows, :]
            h_ref[rows, :] = (xr * _rms_scale(xr) * nw_ref[...]).astype(BF16)
            o_ref[rows, :] = xr
            return c
        lax.fori_loop(0, MLP_TM // MLP_RC, body, 0)

    def body(r, c):
        rows = row_chunk(r)
        u = jnp.dot(h_ref[rows, :], wu_ref[...], preferred_element_type=F32)
        u = jnp.maximum(u, 0.0)
        a = (u * u).astype(BF16)
        o_ref[rows, :] += jnp.dot(a, wd_ref[...], preferred_element_type=F32)
        return c
    lax.fori_loop(0, MLP_TM // MLP_RC, body, 0)

    @pl.when(j == pl.num_programs(1) - 1)
    def _():
        def body(r, c):
            rows = row_chunk(r)
            y = o_ref[rows, :]
            o_ref[rows, :] = y * _rms_scale(y) * fw_ref[...]
            return c
        lax.fori_loop(0, MLP_TM // MLP_RC, body, 0)


def _mlp(x2d, norm_w, w_up, w_down, final_w):
    n = x2d.shape[0]
    assert n % MLP_TM == 0 and D_FF % MLP_TF == 0
    vmem = (2 * 2 * MLP_TM * D_MODEL * 4 + MLP_TM * D_MODEL * 2 + 2 * 2 * D_MODEL * MLP_TF * 2)
    return pl.pallas_call(
        _mlp_kernel,
        out_shape=jax.ShapeDtypeStruct((n, D_MODEL), F32),
        grid=(n // MLP_TM, D_FF // MLP_TF),
        in_specs=[
            pl.BlockSpec((MLP_TM, D_MODEL), lambda i, j: (i, 0)),
            pl.BlockSpec((1, D_MODEL), lambda i, j: (0, 0)),
            pl.BlockSpec((D_MODEL, MLP_TF), lambda i, j: (0, j)),
            pl.BlockSpec((MLP_TF, D_MODEL), lambda i, j: (j, 0)),
            pl.BlockSpec((1, D_MODEL), lambda i, j: (0, 0)),
        ],
        out_specs=pl.BlockSpec((MLP_TM, D_MODEL), lambda i, j: (i, 0)),
        scratch_shapes=[pltpu.VMEM((MLP_TM, D_MODEL), BF16)],
        compiler_params=pltpu.CompilerParams(
            dimension_semantics=("arbitrary", "arbitrary"),
            vmem_limit_bytes=_vmem_limit(vmem)),
        name="mlp",
    )(x2d, norm_w, w_up, w_down, final_w)


def _rope_tables(seq):
    t = jnp.arange(seq, dtype=jnp.int32)
    row = (t // GRID_W).astype(F32)
    col = (t % GRID_W).astype(F32)
    inv_freq = 1.0 / (ROPE_THETA ** (jnp.arange(0, AXIS_DIM, 2, dtype=F32) / AXIS_DIM))
    ang_r = row[:, None] * inv_freq[None, :]
    ang_c = col[:, None] * inv_freq[None, :]
    cos_t = jnp.concatenate([jnp.cos(ang_r), jnp.cos(ang_r), jnp.cos(ang_c), jnp.cos(ang_c)], axis=-1)
    sin_t = jnp.concatenate([-jnp.sin(ang_r), jnp.sin(ang_r), -jnp.sin(ang_c), jnp.sin(ang_c)], axis=-1)
    return cos_t, sin_t


def kernel(x, mem, norm_mix_w, w_in, ret_decay_fwd, ret_decay_bwd, ret_gn_w, ret_gn_b, attn_q_norm_w, attn_k_norm_w, w_out, norm_cross_w, norm_mem_w, w_cross_q, w_cross_k, w_cross_v, w_cross_o, norm_mlp_w, w_mlp_up, w_mlp_down, norm_final_w):
    batch, seq, _ = x.shape
    assert w_in.shape[0] == 1, "single-layer block: per-layer parameters have a leading axis of 1"
    cos_t, sin_t = _rope_tables(seq)
    xs = x.reshape(batch * seq, D_MODEL)
    mem2d = mem.reshape(batch * MEM_TOKENS, D_MODEL)
    dec = jnp.stack([ret_decay_fwd[0], ret_decay_bwd[0]], axis=1)
    dec = jnp.broadcast_to(dec[:, :, None], (RET_HEADS, 2, RET_C)).astype(F32)
    proj = _in_proj(xs, norm_mix_w, w_in[0].astype(BF16), cos_t, sin_t,
                    attn_q_norm_w, attn_k_norm_w, seq)
    y_ret = _retention(proj, dec, ret_gn_w, ret_gn_b, batch, seq)
    y_attn = _attention(proj, batch, seq)
    w_kv = jnp.concatenate([w_cross_k[0], w_cross_v[0]], axis=1).astype(BF16)
    mem_kv = _mem_kv(mem2d, norm_mem_w, w_kv, batch)
    xs = _out_cross(xs, y_ret, y_attn, w_out[0].astype(BF16), norm_cross_w,
                    w_cross_q[0].astype(BF16), mem_kv, w_cross_o[0].astype(BF16), seq)
    xs = _mlp(xs, norm_mlp_w, w_mlp_up[0].astype(BF16), w_mlp_down[0].astype(BF16),
              norm_final_w[None, :])
    return xs.reshape(batch, seq, D_MODEL)
```

```python
import functools
import math

import jax
import jax.numpy as jnp
from jax import lax
from jax.experimental import pallas as pl
from jax.experimental.pallas import tpu as pltpu

D_MODEL = 2048
HEAD_DIM = 128
RET_WIDTH = 1024
ATTN_WIDTH = 1024
RET_HEADS = 8
ATTN_HEADS = 8
ATTN_KV_HEADS = 2
GQA_GROUP = ATTN_HEADS // ATTN_KV_HEADS
KV_WIDTH = ATTN_KV_HEADS * HEAD_DIM
IN_WIDTH = 4 * RET_WIDTH + ATTN_WIDTH + 2 * KV_WIDTH
GRID_W = 64
AXIS_DIM = HEAD_DIM // 2
ROPE_THETA = 10000.0
MEM_TOKENS = 256
CROSS_HEADS = 4
CROSS_HEAD_DIM = 128
CROSS_WIDTH = CROSS_HEADS * CROSS_HEAD_DIM
D_FF = 4 * D_MODEL
NORM_EPS = 1e-6

V7X_LANES = 128
V7X_BF16_SUBLANES = 16
V7X_VMEM_BYTES = 64 * 1024 * 1024

F32 = jnp.float32
BF16 = jnp.bfloat16
LOG2E = math.log2(math.e)
NEG_BIG = -1e30

IN_TM = 1024
IN_TN = 512
IN_NORM_RC = 256
RET_C = 256
RET_TS = 1024
RET_HG = 4
ATT_TQ = 512
ATT_TK = 4096
ATT_ONES = V7X_BF16_SUBLANES
ATT_DENOM_FLOOR = 2.0 ** -40
OC_TM = 512
OC_RC = 512
MLP_TM = 1024
MLP_TF = 512
MLP_RC = 1024


def _vmem_limit(nbytes):
    return int(min(nbytes + 16 * 1024 * 1024, V7X_VMEM_BYTES - 8 * 1024 * 1024))


def _rms_scale(y):
    return lax.rsqrt(jnp.mean(y * y, axis=-1, keepdims=True) + NORM_EPS)


def _rope(y, cos, sin_signed):
    return y * cos + pltpu.roll(y, HEAD_DIM // 2, 1) * sin_signed


def _in_proj_kernel(x_ref, nw_ref, w_ref, cos_ref, sin_ref, qg_ref, kg_ref, o_ref, h_ref):
    j = pl.program_id(1)

    @pl.when(j == 0)
    def _():
        def body(r, c):
            rows = pl.ds(pl.multiple_of(r * IN_NORM_RC, IN_NORM_RC), IN_NORM_RC)
            xr = x_ref[rows, :]
            h_ref[rows, :] = (xr * _rms_scale(xr) * nw_ref[...]).astype(BF16)
            return c
        lax.fori_loop(0, IN_TM // IN_NORM_RC, body, 0)

    ret_k_scale = HEAD_DIM ** -0.5
    attn_q_scale = (HEAD_DIM ** -0.5) * LOG2E

    n_slabs = IN_TN // HEAD_DIM
    slab_cols = [slice(s * HEAD_DIM, (s + 1) * HEAD_DIM) for s in range(n_slabs)]

    def run(epilogue):
        acc = jnp.dot(h_ref[...], w_ref[...], preferred_element_type=F32)
        slabs = epilogue([acc[:, cols] for cols in slab_cols], cos_ref[...], sin_ref[...])
        for cols, y in zip(slab_cols, slabs):
            o_ref[:, cols] = y.astype(BF16)

    def ret_q(ys, cos, sin):
        return [_rope(y, cos, sin) for y in ys]

    def ret_k(ys, cos, sin):
        return [_rope(y, cos, sin) * ret_k_scale for y in ys]

    def plain(ys, cos, sin):
        return ys

    def head_inv_rms(y_a, y_b):
        width = 2 * HEAD_DIM
        r = lax.broadcasted_iota(jnp.int32, (width, width), 0) // HEAD_DIM
        c = lax.broadcasted_iota(jnp.int32, (width, width), 1) // HEAD_DIM
        ones_bd = jnp.where(r == c, 1.0, 0.0).astype(BF16)
        sq = jnp.concatenate([(y_a * y_a).astype(BF16), (y_b * y_b).astype(BF16)], axis=1)
        ms = jnp.dot(sq, ones_bd, preferred_element_type=F32) * (1.0 / HEAD_DIM)
        inv = lax.rsqrt(ms + NORM_EPS)
        return inv[:, :HEAD_DIM], inv[:, HEAD_DIM:]

    def normed_rope(y_a, y_b, gain, scale, cos, sin):
        inv_a, inv_b = head_inv_rms(y_a, y_b)
        return [_rope(y_a * gain, cos, sin) * (inv_a * scale),
                _rope(y_b * gain, cos, sin) * (inv_b * scale)]

    def attn_q(ys, cos, sin):
        gain = qg_ref[...]
        return (normed_rope(ys[0], ys[1], gain, attn_q_scale, cos, sin)
                + normed_rope(ys[2], ys[3], gain, attn_q_scale, cos, sin))

    def attn_kv(ys, cos, sin):
        return normed_rope(ys[0], ys[1], kg_ref[...], 1.0, cos, sin) + ys[ATTN_KV_HEADS:]

    nb = RET_WIDTH // IN_TN
    pl.when(j < nb)(lambda: run(ret_q))
    pl.when((j >= nb) & (j < 2 * nb))(lambda: run(ret_k))
    pl.when((j >= 2 * nb) & (j < 4 * nb))(lambda: run(plain))
    pl.when((j >= 4 * nb) & (j < 5 * nb))(lambda: run(attn_q))
    pl.when(j == 5 * nb)(lambda: run(attn_kv))


def _in_proj(x2d, norm_w, w_in, cos_t, sin_t, q_gain, k_gain, seq):
    n = x2d.shape[0]
    assert IN_WIDTH % IN_TN == 0 and 2 * KV_WIDTH == IN_TN and n % IN_TM == 0 and seq % IN_TM == 0
    tiles_per_seq = seq // IN_TM
    vmem = (2 * IN_TM * D_MODEL * 4 + IN_TM * D_MODEL * 2 + 2 * D_MODEL * IN_TN * 2
            + 2 * IN_TM * IN_TN * 2 + 4 * IN_TM * HEAD_DIM * 4)
    return pl.pallas_call(
        _in_proj_kernel,
        out_shape=jax.ShapeDtypeStruct((n, IN_WIDTH), BF16),
        grid=(n // IN_TM, IN_WIDTH // IN_TN),
        in_specs=[
            pl.BlockSpec((IN_TM, D_MODEL), lambda i, j: (i, 0)),
            pl.BlockSpec((1, D_MODEL), lambda i, j: (0, 0)),
            pl.BlockSpec((D_MODEL, IN_TN), lambda i, j: (0, j)),
            pl.BlockSpec((IN_TM, HEAD_DIM), lambda i, j: (i % tiles_per_seq, 0)),
            pl.BlockSpec((IN_TM, HEAD_DIM), lambda i, j: (i % tiles_per_seq, 0)),
            pl.BlockSpec((1, HEAD_DIM), lambda i, j: (0, 0)),
            pl.BlockSpec((1, HEAD_DIM), lambda i, j: (0, 0)),
        ],
        out_specs=pl.BlockSpec((IN_TM, IN_TN), lambda i, j: (i, j)),
        scratch_shapes=[pltpu.VMEM((IN_TM, D_MODEL), BF16)],
        compiler_params=pltpu.CompilerParams(
            dimension_semantics=("arbitrary", "arbitrary"),
            vmem_limit_bytes=_vmem_limit(vmem)),
        name="in_proj",
    )(x2d, norm_w, w_in, cos_t, sin_t, q_gain, k_gain)


def _retention_kernel(dec_ref, q_ref, k_ref, v_ref, g_ref, gw_ref, gb_ref, o_ref,
                      d_ref, tab_ref, sb_ref, st_ref, *, n_seq):
    s = pl.program_id(2)
    c_len = RET_C
    cps = RET_TS // RET_C
    contract_rows = (((0,), (0,)), ((), ()))
    contract_cols = (((1,), (1,)), ((), ()))
    head_cols = [slice(h * HEAD_DIM, (h + 1) * HEAD_DIM) for h in range(RET_HG)]
    chunk_rows = [slice(c * c_len, (c + 1) * c_len) for c in range(cps)]

    def log_gammas(h):
        return -jnp.exp(dec_ref[h, 0:1, :]), -jnp.exp(dec_ref[h, 1:2, :])

    @pl.when(s == 0)
    def _():
        row = lax.broadcasted_iota(jnp.int32, (c_len, c_len), 0)
        col = lax.broadcasted_iota(jnp.int32, (c_len, c_len), 1)
        rel = (row - col).astype(F32)
        idx = lax.broadcasted_iota(jnp.int32, (c_len, HEAD_DIM), 0).astype(F32)
        for h in range(RET_HG):
            lg_f, lg_b = log_gammas(h)
            d_ref[h] = jnp.exp(jnp.where(rel >= 0, lg_f * rel, -lg_b * rel))
            lf = lg_f[:, :HEAD_DIM]
            lb = lg_b[:, :HEAD_DIM]
            tab_ref[h, 0] = jnp.exp(lf * (idx + 1.0))
            tab_ref[h, 1] = jnp.exp(lb * (c_len - idx))
            tab_ref[h, 2] = jnp.exp(lf * (c_len - 1.0 - idx))
            tab_ref[h, 3] = jnp.exp(lb * idx)
        st_ref[...] = jnp.zeros(st_ref.shape, F32)

    @pl.when(s < n_seq)
    def _():
        first_chunk = (n_seq - 1 - s) * cps
        for h in range(RET_HG):
            chunk_b = jnp.exp(log_gammas(h)[1][:, :HEAD_DIM] * c_len)
            state = st_ref[h]
            for c in reversed(range(cps)):
                sb_ref[first_chunk + c, h] = state.astype(BF16)
                k = k_ref[chunk_rows[c], head_cols[h]]
                kd = (k.astype(F32) * tab_ref[h, 3]).astype(BF16)
                upd = lax.dot_general(kd, v_ref[chunk_rows[c], head_cols[h]], contract_rows,
                                      preferred_element_type=F32)
                state = state * chunk_b + upd
            st_ref[h] = state

        @pl.when(s == n_seq - 1)
        def _():
            st_ref[...] = jnp.zeros(st_ref.shape, F32)

    @pl.when(s >= n_seq)
    def _():
        first_chunk = (s - n_seq) * cps
        for h in range(RET_HG):
            chunk_f = jnp.exp(log_gammas(h)[0][:, :HEAD_DIM] * c_len)
            gw = gw_ref[:, head_cols[h]]
            gb = gb_ref[:, head_cols[h]]
            state = st_ref[h]
            for c in range(cps):
                q = q_ref[chunk_rows[c], head_cols[h]]
                k = k_ref[chunk_rows[c], head_cols[h]]
                v = v_ref[chunk_rows[c], head_cols[h]]
                scores = lax.dot_general(q, k, contract_cols, preferred_element_type=F32)
                masked = (scores * d_ref[h]).astype(BF16)
                qf = q.astype(F32)
                lhs = jnp.concatenate(
                    [masked, (qf * tab_ref[h, 0]).astype(BF16), (qf * tab_ref[h, 1]).astype(BF16)],
                    axis=1)
                rhs = jnp.concatenate([v, state.astype(BF16), sb_ref[first_chunk + c, h]], axis=0)
                out = jnp.dot(lhs, rhs, preferred_element_type=F32)
                mu = jnp.mean(out, axis=-1, keepdims=True)
                cen = out - mu
                var = jnp.mean(cen * cen, axis=-1, keepdims=True)
                y = cen * lax.rsqrt(var + NORM_EPS) * gw + gb
                g = g_ref[chunk_rows[c], head_cols[h]].astype(F32)
                y = y * (g / (1.0 + jnp.exp(-g)))
                o_ref[chunk_rows[c], head_cols[h]] = y.astype(BF16)
                kd = (k.astype(F32) * tab_ref[h, 2]).astype(BF16)
                upd = lax.dot_general(kd, v, contract_rows, preferred_element_type=F32)
                state = state * chunk_f + upd
            st_ref[h] = state


def _retention(proj, dec, gn_w, gn_b, batch, seq):
    n = proj.shape[0]
    assert seq % RET_TS == 0 and RET_TS % RET_C == 0 and RET_HEADS % RET_HG == 0
    n_seq = seq // RET_TS
    n_chunks = seq // RET_C
    group_w = RET_HG * HEAD_DIM
    groups = RET_HEADS // RET_HG

    def kv_block(section):
        def index(b, hg, s):
            blk = jnp.where(s < n_seq, n_seq - 1 - s, s - n_seq)
            return (b * n_seq + blk, section * groups + hg)
        return index

    def fwd_block(section):
        return lambda b, hg, s: (b * n_seq + jnp.maximum(s - n_seq, 0), section * groups + hg)

    vmem = (2 * 5 * RET_TS * group_w * 2 + RET_HG * RET_C * RET_C * 4
            + RET_HG * 4 * RET_C * HEAD_DIM * 4 + n_chunks * RET_HG * HEAD_DIM * HEAD_DIM * 2
            + RET_HG * HEAD_DIM * HEAD_DIM * 4)
    return pl.pallas_call(
        functools.partial(_retention_kernel, n_seq=n_seq),
        out_shape=jax.ShapeDtypeStruct((n, RET_WIDTH), BF16),
        grid=(batch, groups, 2 * n_seq),
        in_specs=[
            pl.BlockSpec((RET_HG, 2, RET_C), lambda b, hg, s: (hg, 0, 0)),
            pl.BlockSpec((RET_TS, group_w), fwd_block(0)),
            pl.BlockSpec((RET_TS, group_w), kv_block(1)),
            pl.BlockSpec((RET_TS, group_w), kv_block(2)),
            pl.BlockSpec((RET_TS, group_w), fwd_block(3)),
            pl.BlockSpec((1, group_w), lambda b, hg, s: (0, hg)),
            pl.BlockSpec((1, group_w), lambda b, hg, s: (0, hg)),
        ],
        out_specs=pl.BlockSpec((RET_TS, group_w), fwd_block(0)),
        scratch_shapes=[
            pltpu.VMEM((RET_HG, RET_C, RET_C), F32),
            pltpu.VMEM((RET_HG, 4, RET_C, HEAD_DIM), F32),
            pltpu.VMEM((n_chunks, RET_HG, HEAD_DIM, HEAD_DIM), BF16),
            pltpu.VMEM((RET_HG, HEAD_DIM, HEAD_DIM), F32),
        ],
        compiler_params=pltpu.CompilerParams(
            dimension_semantics=("arbitrary", "arbitrary", "arbitrary"),
            vmem_limit_bytes=_vmem_limit(vmem)),
        name="retention",
    )(dec, proj, proj, proj, proj, gn_w, gn_b)


def _attention_kernel(q_ref, k_ref, v_ref, o_ref, vt_ref, kmax_ref, qa_ref, acc_ref, *, n_kv):
    i = pl.program_id(2)
    contract_cols = (((1,), (1,)), ((), ()))

    def kv_rows(c):
        return pl.ds(pl.multiple_of(c * ATT_TK, ATT_TK), ATT_TK)

    @pl.when(i == 0)
    def _():
        ones = jnp.ones((ATT_ONES, ATT_TK), BF16)

        def body(c, k2max):
            rows = kv_rows(c)
            vt_ref[c, 0:HEAD_DIM, :] = v_ref[rows, :].astype(F32).T.astype(BF16)
            vt_ref[c, HEAD_DIM:HEAD_DIM + ATT_ONES, :] = ones
            kc = k_ref[rows, :].astype(F32)
            k2 = jnp.max(jnp.sum(kc * kc, axis=1, keepdims=True), axis=0, keepdims=True)
            return jnp.maximum(k2max, k2)
        k2max = lax.fori_loop(0, n_kv, body, jnp.zeros((1, 1), F32))
        kmax_ref[...] = jnp.broadcast_to(jnp.sqrt(k2max), kmax_ref.shape)

    q_lane = lax.broadcasted_iota(jnp.int32, (ATT_TQ, HEAD_DIM), 1)
    k_lane = lax.broadcasted_iota(jnp.int32, (ATT_TK, HEAD_DIM), 1)
    k_one = jnp.where(k_lane == 0, 1.0, 0.0).astype(BF16)
    head_cols = [slice(h * HEAD_DIM, (h + 1) * HEAD_DIM) for h in range(GQA_GROUP)]

    def set_shift(h, shift_col):
        qa_ref[h, :, HEAD_DIM:] = jnp.where(q_lane == 0, -shift_col, 0.0).astype(BF16)

    def sweep():
        acc_ref[...] = jnp.zeros(acc_ref.shape, F32)

        def kv_step(c, carry):
            k_aug = jnp.concatenate([k_ref[kv_rows(c), :], k_one], axis=1)
            vt = vt_ref[c]
            for h in range(GQA_GROUP):
                s_t = lax.dot_general(k_aug, qa_ref[h], contract_cols,
                                      preferred_element_type=F32)
                p_t = jnp.exp2(s_t).astype(BF16)
                acc_ref[h] += jnp.dot(vt, p_t, preferred_element_type=F32)
            return carry
        lax.fori_loop(0, n_kv, kv_step, 0)

    for h in range(GQA_GROUP):
        qh = q_ref[:, head_cols[h]]
        qa_ref[h, :, 0:HEAD_DIM] = qh
        qf = qh.astype(F32)
        q_norm = jnp.sqrt(jnp.sum(qf * qf, axis=1, keepdims=True))
        set_shift(h, q_norm * kmax_ref[0:1, 0:1])
    sweep()

    denom_min = jnp.min(acc_ref[0, HEAD_DIM:HEAD_DIM + 1, :])
    for h in range(1, GQA_GROUP):
        denom_min = jnp.minimum(denom_min, jnp.min(acc_ref[h, HEAD_DIM:HEAD_DIM + 1, :]))

    @pl.when(denom_min < ATT_DENOM_FLOOR)
    def _():
        for h in range(GQA_GROUP):
            def body(c, m):
                s = lax.dot_general(q_ref[:, head_cols[h]], k_ref[kv_rows(c), :], contract_cols,
                                    preferred_element_type=F32)
                return jnp.maximum(m, jnp.max(s, axis=1, keepdims=True))
            set_shift(h, lax.fori_loop(0, n_kv, body, jnp.full((ATT_TQ, 1), NEG_BIG, F32)))
        sweep()

    for h in range(GQA_GROUP):
        acc = acc_ref[h]
        o_t = acc[0:HEAD_DIM, :] * (1.0 / acc[HEAD_DIM:HEAD_DIM + 1, :])
        o_ref[:, head_cols[h]] = o_t.T.astype(BF16)


def _attention(proj, batch, seq):
    n = proj.shape[0]
    assert seq % ATT_TQ == 0 and seq % ATT_TK == 0
    n_q = seq // ATT_TQ
    n_kv = seq // ATT_TK
    group_w = GQA_GROUP * HEAD_DIM
    q_col0 = 4 * RET_WIDTH // group_w
    k_col0 = (4 * RET_WIDTH + ATTN_WIDTH) // HEAD_DIM
    v_col0 = k_col0 + ATTN_KV_HEADS
    vmem = (2 * ATT_TQ * group_w * 2 * 2 + 2 * 2 * seq * HEAD_DIM * 2
            + (HEAD_DIM + ATT_ONES) * seq * 2 + GQA_GROUP * (HEAD_DIM + ATT_ONES) * ATT_TQ * 4
            + GQA_GROUP * ATT_TQ * 2 * HEAD_DIM * 2 + 2 * GQA_GROUP * ATT_TK * ATT_TQ * 4)
    return pl.pallas_call(
        functools.partial(_attention_kernel, n_kv=n_kv),
        out_shape=jax.ShapeDtypeStruct((n, ATTN_WIDTH), BF16),
        grid=(batch, ATTN_KV_HEADS, n_q),
        in_specs=[
            pl.BlockSpec((ATT_TQ, group_w), lambda b, g, i: (b * n_q + i, q_col0 + g)),
            pl.BlockSpec((seq, HEAD_DIM), lambda b, g, i: (b, k_col0 + g)),
            pl.BlockSpec((seq, HEAD_DIM), lambda b, g, i: (b, v_col0 + g)),
        ],
        out_specs=pl.BlockSpec((ATT_TQ, group_w), lambda b, g, i: (b * n_q + i, g)),
        scratch_shapes=[
            pltpu.VMEM((n_kv, HEAD_DIM + ATT_ONES, ATT_TK), BF16),
            pltpu.VMEM((8, HEAD_DIM), F32),
            pltpu.VMEM((GQA_GROUP, ATT_TQ, 2 * HEAD_DIM), BF16),
            pltpu.VMEM((GQA_GROUP, HEAD_DIM + ATT_ONES, ATT_TQ), F32),
        ],
        compiler_params=pltpu.CompilerParams(
            dimension_semantics=("arbitrary", "arbitrary", "arbitrary"),
            vmem_limit_bytes=_vmem_limit(vmem)),
        name="attention",
    )(proj, proj, proj)


def _mem_kv_kernel(m_ref, nw_ref, w_ref, o_ref):
    m = m_ref[...]
    h = (m * _rms_scale(m) * nw_ref[...]).astype(BF16)
    o_ref[...] = jnp.dot(h, w_ref[...], preferred_element_type=F32).astype(BF16)


def _mem_kv(mem2d, norm_w, w_kv, batch):
    rows = mem2d.shape[0] // batch
    vmem = 2 * rows * D_MODEL * 4 + 2 * D_MODEL * 2 * CROSS_WIDTH * 2 + 2 * rows * 2 * CROSS_WIDTH * 2
    return pl.pallas_call(
        _mem_kv_kernel,
        out_shape=jax.ShapeDtypeStruct((mem2d.shape[0], 2 * CROSS_WIDTH), BF16),
        grid=(batch,),
        in_specs=[
            pl.BlockSpec((rows, D_MODEL), lambda b: (b, 0)),
            pl.BlockSpec((1, D_MODEL), lambda b: (0, 0)),
            pl.BlockSpec((D_MODEL, 2 * CROSS_WIDTH), lambda b: (0, 0)),
        ],
        out_specs=pl.BlockSpec((rows, 2 * CROSS_WIDTH), lambda b: (b, 0)),
        compiler_params=pltpu.CompilerParams(
            dimension_semantics=("arbitrary",),
            vmem_limit_bytes=_vmem_limit(vmem)),
        name="mem_kv",
    )(mem2d, norm_w, w_kv)


def _out_cross_kernel(x_ref, yr_ref, ya_ref, wo_ref, nw_ref, wq_ref, kv_ref, wco_ref, o_ref):
    contract_cols = (((1,), (1,)), ((), ()))
    scale = CROSS_HEAD_DIM ** -0.5

    def body(r, carry):
        rows = pl.ds(pl.multiple_of(r * OC_RC, OC_RC), OC_RC)
        x1 = (x_ref[rows, :]
              + jnp.dot(yr_ref[rows, :], wo_ref[0:RET_WIDTH, :], preferred_element_type=F32)
              + jnp.dot(ya_ref[rows, :], wo_ref[RET_WIDTH:, :], preferred_element_type=F32))
        h = (x1 * _rms_scale(x1) * nw_ref[...]).astype(BF16)
        q = jnp.dot(h, wq_ref[...], preferred_element_type=F32).astype(BF16)
        heads = []
        for hd in range(CROSS_HEADS):
            cols = slice(hd * CROSS_HEAD_DIM, (hd + 1) * CROSS_HEAD_DIM)
            k = kv_ref[:, cols]
            v = kv_ref[:, CROSS_WIDTH + hd * CROSS_HEAD_DIM:CROSS_WIDTH + (hd + 1) * CROSS_HEAD_DIM]
            s = lax.dot_general(q[:, cols], k, contract_cols, preferred_element_type=F32) * scale
            e = jnp.exp(s - jnp.max(s, axis=-1, keepdims=True))
            p = (e / jnp.sum(e, axis=-1, keepdims=True)).astype(BF16)
            heads.append(jnp.dot(p, v, preferred_element_type=F32).astype(BF16))
        o = jnp.concatenate(heads, axis=1)
        o_ref[rows, :] = x1 + jnp.dot(o, wco_ref[...], preferred_element_type=F32)
        return carry
    lax.fori_loop(0, OC_TM // OC_RC, body, 0)


def _out_cross(x2d, y_ret, y_attn, w_out, norm_w, w_q, mem_kv, w_co, seq):
    n = x2d.shape[0]
    assert n % OC_TM == 0 and seq % OC_TM == 0
    tiles_per_seq = seq // OC_TM
    vmem = (2 * 2 * OC_TM * D_MODEL * 4 + 2 * 2 * OC_TM * RET_WIDTH * 2 + 2 * D_MODEL * D_MODEL * 2
            + 2 * 2 * D_MODEL * CROSS_WIDTH * 2 + 2 * MEM_TOKENS * 2 * CROSS_WIDTH * 2)
    return pl.pallas_call(
        _out_cross_kernel,
        out_shape=jax.ShapeDtypeStruct((n, D_MODEL), F32),
        grid=(n // OC_TM,),
        in_specs=[
            pl.BlockSpec((OC_TM, D_MODEL), lambda i: (i, 0)),
            pl.BlockSpec((OC_TM, RET_WIDTH), lambda i: (i, 0)),
            pl.BlockSpec((OC_TM, ATTN_WIDTH), lambda i: (i, 0)),
            pl.BlockSpec((D_MODEL, D_MODEL), lambda i: (0, 0)),
            pl.BlockSpec((1, D_MODEL), lambda i: (0, 0)),
            pl.BlockSpec((D_MODEL, CROSS_WIDTH), lambda i: (0, 0)),
            pl.BlockSpec((MEM_TOKENS, 2 * CROSS_WIDTH), lambda i: (i // tiles_per_seq, 0)),
            pl.BlockSpec((CROSS_WIDTH, D_MODEL), lambda i: (0, 0)),
        ],
        out_specs=pl.BlockSpec((OC_TM, D_MODEL), lambda i: (i, 0)),
        compiler_params=pltpu.CompilerParams(
            dimension_semantics=("arbitrary",),
            vmem_limit_bytes=_vmem_limit(vmem)),
        name="out_cross",
    )(x2d, y_ret, y_attn, w_out, norm_w, w_q, mem_kv, w_co)


def _mlp_kernel(x_ref, nw_ref, wu_ref, wd_ref, fw_ref, o_ref, h_ref):
    j = pl.program_id(1)

    def row_chunk(r):
        return pl.ds(pl.multiple_of(r * MLP_RC, MLP_RC), MLP_RC)

    @pl.when(j == 0)
    def _():
        def body(r, c):
            rows = row_chunk(r)
            xr = x_ref[rows, :]
            h_ref[rows, :] = (xr * _rms_scale(xr) * nw_ref[...]).astype(BF16)
            o_ref[rows, :] = xr
            return c
        lax.fori_loop(0, MLP_TM // MLP_RC, body, 0)

    def body(r, c):
        rows = row_chunk(r)
        u = jnp.dot(h_ref[rows, :], wu_ref[...], preferred_element_type=F32)
        u = jnp.maximum(u, 0.0)
        a = (u * u).astype(BF16)
        o_ref[rows, :] += jnp.dot(a, wd_ref[...], preferred_element_type=F32)
        return c
    lax.fori_loop(0, MLP_TM // MLP_RC, body, 0)

    @pl.when(j == pl.num_programs(1) - 1)
    def _():
        def body(r, c):
            rows = row_chunk(r)
            y = o_ref[rows, :]
            o_ref[rows, :] = y * _rms_scale(y) * fw_ref[...]
            return c
        lax.fori_loop(0, MLP_TM // MLP_RC, body, 0)


def _mlp(x2d, norm_w, w_up, w_down, final_w):
    n = x2d.shape[0]
    assert n % MLP_TM == 0 and D_FF % MLP_TF == 0
    vmem = (2 * 2 * MLP_TM * D_MODEL * 4 + MLP_TM * D_MODEL * 2 + 2 * 2 * D_MODEL * MLP_TF * 2)
    return pl.pallas_call(
        _mlp_kernel,
        out_shape=jax.ShapeDtypeStruct((n, D_MODEL), F32),
        grid=(n // MLP_TM, D_FF // MLP_TF),
        in_specs=[
            pl.BlockSpec((MLP_TM, D_MODEL), lambda i, j: (i, 0)),
            pl.BlockSpec((1, D_MODEL), lambda i, j: (0, 0)),
            pl.BlockSpec((D_MODEL, MLP_TF), lambda i, j: (0, j)),
            pl.BlockSpec((MLP_TF, D_MODEL), lambda i, j: (j, 0)),
            pl.BlockSpec((1, D_MODEL), lambda i, j: (0, 0)),
        ],
        out_specs=pl.BlockSpec((MLP_TM, D_MODEL), lambda i, j: (i, 0)),
        scratch_shapes=[pltpu.VMEM((MLP_TM, D_MODEL), BF16)],
        compiler_params=pltpu.CompilerParams(
            dimension_semantics=("arbitrary", "arbitrary"),
            vmem_limit_bytes=_vmem_limit(vmem)),
        name="mlp",
    )(x2d, norm_w, w_up, w_down, final_w)


def _rope_tables(seq):
    t = jnp.arange(seq, dtype=jnp.int32)
    row = (t // GRID_W).astype(F32)
    col = (t % GRID_W).astype(F32)
    inv_freq = 1.0 / (ROPE_THETA ** (jnp.arange(0, AXIS_DIM, 2, dtype=F32) / AXIS_DIM))
    ang_r = row[:, None] * inv_freq[None, :]
    ang_c = col[:, None] * inv_freq[None, :]
    cos_t = jnp.concatenate([jnp.cos(ang_r), jnp.cos(ang_c), jnp.cos(ang_r), jnp.cos(ang_c)], axis=-1)
    sin_t = jnp.concatenate([-jnp.sin(ang_r), -jnp.sin(ang_c), jnp.sin(ang_r), jnp.sin(ang_c)], axis=-1)
    return cos_t, sin_t


def _pair_heads(w):
    lead = w.shape[:-1]
    heads = w.shape[-1] // HEAD_DIM
    quarter = HEAD_DIM // 4
    w = w.reshape(lead + (heads, 2, 2, quarter))
    return jnp.swapaxes(w, -2, -3).reshape(lead + (heads * HEAD_DIM,))


def _in_proj_weights(w_in):
    c_rqk = 2 * RET_WIDTH
    c_aq = 4 * RET_WIDTH
    c_av = c_aq + ATTN_WIDTH + KV_WIDTH
    return jnp.concatenate([
        _pair_heads(w_in[:, :c_rqk]), w_in[:, c_rqk:c_aq],
        _pair_heads(w_in[:, c_aq:c_av]), w_in[:, c_av:]], axis=1).astype(BF16)


def kernel(x, mem, norm_mix_w, w_in, ret_decay_fwd, ret_decay_bwd, ret_gn_w, ret_gn_b, attn_q_norm_w, attn_k_norm_w, w_out, norm_cross_w, norm_mem_w, w_cross_q, w_cross_k, w_cross_v, w_cross_o, norm_mlp_w, w_mlp_up, w_mlp_down, norm_final_w):
    batch, seq, _ = x.shape
    assert w_in.shape[0] == 1, "single-layer block: per-layer parameters have a leading axis of 1"
    cos_t, sin_t = _rope_tables(seq)
    xs = x.reshape(batch * seq, D_MODEL)
    mem2d = mem.reshape(batch * MEM_TOKENS, D_MODEL)
    dec = jnp.stack([ret_decay_fwd[0], ret_decay_bwd[0]], axis=1)
    dec = jnp.broadcast_to(dec[:, :, None], (RET_HEADS, 2, RET_C)).astype(F32)
    proj = _in_proj(xs, norm_mix_w, _in_proj_weights(w_in[0]), cos_t, sin_t,
                    _pair_heads(attn_q_norm_w), _pair_heads(attn_k_norm_w), seq)
    y_ret = _retention(proj, dec, ret_gn_w, ret_gn_b, batch, seq)
    y_attn = _attention(proj, batch, seq)
    w_kv = jnp.concatenate([w_cross_k[0], w_cross_v[0]], axis=1).astype(BF16)
    mem_kv = _mem_kv(mem2d, norm_mem_w, w_kv, batch)
    xs = _out_cross(xs, y_ret, y_attn, w_out[0].astype(BF16), norm_cross_w,
                    w_cross_q[0].astype(BF16), mem_kv, w_cross_o[0].astype(BF16), seq)
    xs = _mlp(xs, norm_mlp_w, w_mlp_up[0].astype(BF16), w_mlp_down[0].astype(BF16),
              norm_final_w[None, :])
    return xs.reshape(batch, seq, D_MODEL)
```

```python
import functools
import math

import jax
import jax.numpy as jnp
from jax import lax
from jax.experimental import pallas as pl
from jax.experimental.pallas import tpu as pltpu

D_MODEL = 2048
HEAD_DIM = 128
RET_WIDTH = 1024
ATTN_WIDTH = 1024
RET_HEADS = 8
ATTN_HEADS = 8
ATTN_KV_HEADS = 2
GQA_GROUP = ATTN_HEADS // ATTN_KV_HEADS
KV_WIDTH = ATTN_KV_HEADS * HEAD_DIM
IN_WIDTH = 4 * RET_WIDTH + ATTN_WIDTH + 2 * KV_WIDTH
GRID_W = 64
AXIS_DIM = HEAD_DIM // 2
ROPE_THETA = 10000.0
MEM_TOKENS = 256
CROSS_HEADS = 4
CROSS_HEAD_DIM = 128
CROSS_WIDTH = CROSS_HEADS * CROSS_HEAD_DIM
D_FF = 4 * D_MODEL
NORM_EPS = 1e-6

V7X_LANES = 128
V7X_BF16_SUBLANES = 16
V7X_VMEM_BYTES = 64 * 1024 * 1024

F32 = jnp.float32
BF16 = jnp.bfloat16
LOG2E = math.log2(math.e)
NEG_BIG = -1e30
TINY = 1e-30

IN_TM = 512
IN_TN = 512
IN_NORM_RC = 256
WPREP_ROWS = 256
RET_C = 256
RET_TS = 1024
RET_HG = 4
ATT_TQ = 512
ATT_TK = 4096
ATT_DENOM_FLOOR = 2.0 ** -40
OC_TM = 512
OC_RC = 512
MLP_TM = 1024
MLP_TF = 512
MLP_RC = 1024


def _vmem_limit(nbytes):
    return int(min(nbytes + 16 * 1024 * 1024, V7X_VMEM_BYTES - 8 * 1024 * 1024))


def _rms_scale(y):
    return lax.rsqrt(jnp.mean(y * y, axis=-1, keepdims=True) + NORM_EPS)


def _rope(y, cos, sin_signed):
    return y * cos + pltpu.roll(y, HEAD_DIM // 2, 1) * sin_signed


def _in_proj_kernel(x_ref, nw_ref, w_ref, cos_ref, sin_ref, qg_ref, kg_ref, o_ref, h_ref):
    for r in range(IN_TM // IN_NORM_RC):
        rows = slice(r * IN_NORM_RC, (r + 1) * IN_NORM_RC)
        xr = x_ref[rows, :]
        h_ref[rows, :] = (xr * _rms_scale(xr) * nw_ref[...]).astype(BF16)

    ret_k_scale = HEAD_DIM ** -0.5
    attn_q_scale = (HEAD_DIM ** -0.5) * LOG2E

    n_slabs = IN_TN // HEAD_DIM

    def run(j, epilogue):
        tile = slice(j * IN_TN, (j + 1) * IN_TN)
        acc = jnp.dot(h_ref[...], w_ref[:, tile], preferred_element_type=F32)
        slabs = epilogue([acc[:, s * HEAD_DIM:(s + 1) * HEAD_DIM] for s in range(n_slabs)],
                         cos_ref[...], sin_ref[...])
        for s, y in enumerate(slabs):
            c0 = j * IN_TN + s * HEAD_DIM
            o_ref[:, c0:c0 + HEAD_DIM] = y.astype(BF16)

    def ret_q(ys, cos, sin):
        return [_rope(y, cos, sin) for y in ys]

    def ret_k(ys, cos, sin):
        return [_rope(y, cos, sin) * ret_k_scale for y in ys]

    def plain(ys, cos, sin):
        return ys

    def head_inv_rms(y_a, y_b):
        width = 2 * HEAD_DIM
        r = lax.broadcasted_iota(jnp.int32, (width, width), 0) // HEAD_DIM
        c = lax.broadcasted_iota(jnp.int32, (width, width), 1) // HEAD_DIM
        ones_bd = jnp.where(r == c, 1.0, 0.0).astype(BF16)
        sq = jnp.concatenate([(y_a * y_a).astype(BF16), (y_b * y_b).astype(BF16)], axis=1)
        ms = jnp.dot(sq, ones_bd, preferred_element_type=F32) * (1.0 / HEAD_DIM)
        inv = lax.rsqrt(ms + NORM_EPS)
        return inv[:, :HEAD_DIM], inv[:, HEAD_DIM:]

    def normed_rope(y_a, y_b, gain, scale, cos, sin):
        inv_a, inv_b = head_inv_rms(y_a, y_b)
        return [_rope(y_a * gain, cos, sin) * (inv_a * scale),
                _rope(y_b * gain, cos, sin) * (inv_b * scale)]

    def attn_q(ys, cos, sin):
        gain = qg_ref[...]
        return (normed_rope(ys[0], ys[1], gain, attn_q_scale, cos, sin)
                + normed_rope(ys[2], ys[3], gain, attn_q_scale, cos, sin))

    def attn_kv(ys, cos, sin):
        return normed_rope(ys[0], ys[1], kg_ref[...], 1.0, cos, sin) + ys[ATTN_KV_HEADS:]

    nb = RET_WIDTH // IN_TN
    for j in range(IN_WIDTH // IN_TN):
        if j < nb:
            run(j, ret_q)
        elif j < 2 * nb:
            run(j, ret_k)
        elif j < 4 * nb:
            run(j, plain)
        elif j < 5 * nb:
            run(j, attn_q)
        else:
            run(j, attn_kv)


def _in_proj(x2d, norm_w, w_in, cos_t, sin_t, q_gain, k_gain, seq):
    n = x2d.shape[0]
    assert IN_WIDTH % IN_TN == 0 and 2 * KV_WIDTH == IN_TN and n % IN_TM == 0 and seq % IN_TM == 0
    tiles_per_seq = seq // IN_TM
    vmem = (2 * IN_TM * D_MODEL * 4 + IN_TM * D_MODEL * 2 + D_MODEL * IN_WIDTH * 2
            + 2 * IN_TM * IN_WIDTH * 2 + 4 * IN_TM * HEAD_DIM * 4)
    resident = pl.Buffered(1)
    return pl.pallas_call(
        _in_proj_kernel,
        out_shape=jax.ShapeDtypeStruct((n, IN_WIDTH), BF16),
        grid=(n // IN_TM,),
        in_specs=[
            pl.BlockSpec((IN_TM, D_MODEL), lambda i: (i, 0)),
            pl.BlockSpec((1, D_MODEL), lambda i: (0, 0)),
            pl.BlockSpec((D_MODEL, IN_WIDTH), lambda i: (0, 0), pipeline_mode=resident),
            pl.BlockSpec((IN_TM, HEAD_DIM), lambda i: (i % tiles_per_seq, 0)),
            pl.BlockSpec((IN_TM, HEAD_DIM), lambda i: (i % tiles_per_seq, 0)),
            pl.BlockSpec((1, HEAD_DIM), lambda i: (0, 0)),
            pl.BlockSpec((1, HEAD_DIM), lambda i: (0, 0)),
        ],
        out_specs=pl.BlockSpec((IN_TM, IN_WIDTH), lambda i: (i, 0)),
        scratch_shapes=[pltpu.VMEM((IN_TM, D_MODEL), BF16)],
        compiler_params=pltpu.CompilerParams(
            dimension_semantics=("arbitrary",),
            vmem_limit_bytes=_vmem_limit(vmem)),
        name="in_proj",
    )(x2d, norm_w, w_in, cos_t, sin_t, q_gain, k_gain)


def _retention_kernel(dec_ref, q_ref, k_ref, v_ref, g_ref, gw_ref, gb_ref, o_ref,
                      d_ref, tab_ref, sb_ref, st_ref, *, n_seq):
    s = pl.program_id(2)
    c_len = RET_C
    cps = RET_TS // RET_C
    contract_rows = (((0,), (0,)), ((), ()))
    contract_cols = (((1,), (1,)), ((), ()))
    head_cols = [slice(h * HEAD_DIM, (h + 1) * HEAD_DIM) for h in range(RET_HG)]
    chunk_rows = [slice(c * c_len, (c + 1) * c_len) for c in range(cps)]

    def log_gammas(h):
        return -jnp.exp(dec_ref[h, 0:1, :]), -jnp.exp(dec_ref[h, 1:2, :])

    @pl.when(s == 0)
    def _():
        row = lax.broadcasted_iota(jnp.int32, (c_len, c_len), 0)
        col = lax.broadcasted_iota(jnp.int32, (c_len, c_len), 1)
        rel = (row - col).astype(F32)
        idx = lax.broadcasted_iota(jnp.int32, (c_len, HEAD_DIM), 0).astype(F32)
        for h in range(RET_HG):
            lg_f, lg_b = log_gammas(h)
            d_ref[h] = jnp.exp(jnp.where(rel >= 0, lg_f * rel, -lg_b * rel))
            lf = lg_f[:, :HEAD_DIM]
            lb = lg_b[:, :HEAD_DIM]
            tab_ref[h, 0] = jnp.exp(lf * (idx + 1.0))
            tab_ref[h, 1] = jnp.exp(lb * (c_len - idx))
            tab_ref[h, 2] = jnp.exp(lf * (c_len - 1.0 - idx))
            tab_ref[h, 3] = jnp.exp(lb * idx)
        st_ref[...] = jnp.zeros(st_ref.shape, F32)

    @pl.when(s < n_seq)
    def _():
        first_chunk = (n_seq - 1 - s) * cps
        for h in range(RET_HG):
            chunk_b = jnp.exp(log_gammas(h)[1][:, :HEAD_DIM] * c_len)
            state = st_ref[h]
            for c in reversed(range(cps)):
                sb_ref[first_chunk + c, h] = state.astype(BF16)
                k = k_ref[chunk_rows[c], head_cols[h]]
                kd = (k.astype(F32) * tab_ref[h, 3]).astype(BF16)
                upd = lax.dot_general(kd, v_ref[chunk_rows[c], head_cols[h]], contract_rows,
                                      preferred_element_type=F32)
                state = state * chunk_b + upd
            st_ref[h] = state

        @pl.when(s == n_seq - 1)
        def _():
            st_ref[...] = jnp.zeros(st_ref.shape, F32)

    @pl.when(s >= n_seq)
    def _():
        first_chunk = (s - n_seq) * cps
        for h in range(RET_HG):
            chunk_f = jnp.exp(log_gammas(h)[0][:, :HEAD_DIM] * c_len)
            gw = gw_ref[:, head_cols[h]]
            gb = gb_ref[:, head_cols[h]]
            state = st_ref[h]
            for c in range(cps):
                q = q_ref[chunk_rows[c], head_cols[h]]
                k = k_ref[chunk_rows[c], head_cols[h]]
                v = v_ref[chunk_rows[c], head_cols[h]]
                scores = lax.dot_general(q, k, contract_cols, preferred_element_type=F32)
                masked = (scores * d_ref[h]).astype(BF16)
                qf = q.astype(F32)
                lhs = jnp.concatenate(
                    [masked, (qf * tab_ref[h, 0]).astype(BF16), (qf * tab_ref[h, 1]).astype(BF16)],
                    axis=1)
                rhs = jnp.concatenate([v, state.astype(BF16), sb_ref[first_chunk + c, h]], axis=0)
                out = jnp.dot(lhs, rhs, preferred_element_type=F32)
                mu = jnp.mean(out, axis=-1, keepdims=True)
                cen = out - mu
                var = jnp.mean(cen * cen, axis=-1, keepdims=True)
                y = cen * lax.rsqrt(var + NORM_EPS) * gw + gb
                g = g_ref[chunk_rows[c], head_cols[h]].astype(F32)
                y = y * (g / (1.0 + jnp.exp(-g)))
                o_ref[chunk_rows[c], head_cols[h]] = y.astype(BF16)
                kd = (k.astype(F32) * tab_ref[h, 2]).astype(BF16)
                upd = lax.dot_general(kd, v, contract_rows, preferred_element_type=F32)
                state = state * chunk_f + upd
            st_ref[h] = state


def _retention(proj, dec, gn_w, gn_b, batch, seq):
    n = proj.shape[0]
    assert seq % RET_TS == 0 and RET_TS % RET_C == 0 and RET_HEADS % RET_HG == 0
    n_seq = seq // RET_TS
    n_chunks = seq // RET_C
    group_w = RET_HG * HEAD_DIM
    groups = RET_HEADS // RET_HG

    def kv_block(section):
        def index(b, hg, s):
            blk = jnp.where(s < n_seq, n_seq - 1 - s, s - n_seq)
            return (b * n_seq + blk, section * groups + hg)
        return index

    def fwd_block(section):
        return lambda b, hg, s: (b * n_seq + jnp.maximum(s - n_seq, 0), section * groups + hg)

    vmem = (2 * 5 * RET_TS * group_w * 2 + RET_HG * RET_C * RET_C * 4
            + RET_HG * 4 * RET_C * HEAD_DIM * 4 + n_chunks * RET_HG * HEAD_DIM * HEAD_DIM * 2
            + RET_HG * HEAD_DIM * HEAD_DIM * 4)
    return pl.pallas_call(
        functools.partial(_retention_kernel, n_seq=n_seq),
        out_shape=jax.ShapeDtypeStruct((n, RET_WIDTH), BF16),
        grid=(batch, groups, 2 * n_seq),
        in_specs=[
            pl.BlockSpec((RET_HG, 2, RET_C), lambda b, hg, s: (hg, 0, 0)),
            pl.BlockSpec((RET_TS, group_w), fwd_block(0)),
            pl.BlockSpec((RET_TS, group_w), kv_block(1)),
            pl.BlockSpec((RET_TS, group_w), kv_block(2)),
            pl.BlockSpec((RET_TS, group_w), fwd_block(3)),
            pl.BlockSpec((1, group_w), lambda b, hg, s: (0, hg)),
            pl.BlockSpec((1, group_w), lambda b, hg, s: (0, hg)),
        ],
        out_specs=pl.BlockSpec((RET_TS, group_w), fwd_block(0)),
        scratch_shapes=[
            pltpu.VMEM((RET_HG, RET_C, RET_C), F32),
            pltpu.VMEM((RET_HG, 4, RET_C, HEAD_DIM), F32),
            pltpu.VMEM((n_chunks, RET_HG, HEAD_DIM, HEAD_DIM), BF16),
            pltpu.VMEM((RET_HG, HEAD_DIM, HEAD_DIM), F32),
        ],
        compiler_params=pltpu.CompilerParams(
            dimension_semantics=("arbitrary", "arbitrary", "arbitrary"),
            vmem_limit_bytes=_vmem_limit(vmem)),
        name="retention",
    )(dec, proj, proj, proj, proj, gn_w, gn_b)


def _attention_kernel(q_ref, k_ref, v_ref, o_ref, vt_ref, k2max_ref, qa_ref, acc_ref, den_ref,
                      *, n_kv):
    i = pl.program_id(2)
    contract_cols = (((1,), (1,)), ((), ()))

    def kv_rows(c):
        return pl.ds(pl.multiple_of(c * ATT_TK, ATT_TK), ATT_TK)

    @pl.when(i == 0)
    def _():
        def body(c, k2max):
            rows = kv_rows(c)
            vt_ref[c] = v_ref[rows, :].astype(F32).T.astype(BF16)
            kc = k_ref[rows, :].astype(F32)
            k2 = jnp.max(jnp.sum(kc * kc, axis=1, keepdims=True), axis=0, keepdims=True)
            return jnp.maximum(k2max, k2)
        k2max = lax.fori_loop(0, n_kv, body, jnp.zeros((1, 1), F32))
        k2max_ref[...] = jnp.broadcast_to(k2max, k2max_ref.shape)

    q_lane = lax.broadcasted_iota(jnp.int32, (ATT_TQ, HEAD_DIM), 1)
    k_lane = lax.broadcasted_iota(jnp.int32, (ATT_TK, HEAD_DIM), 1)
    k_one = jnp.where(k_lane == 0, 1.0, 0.0).astype(BF16)
    head_cols = [slice(h * HEAD_DIM, (h + 1) * HEAD_DIM) for h in range(GQA_GROUP)]

    def set_shift(h, shift):
        qa_ref[h, :, HEAD_DIM:] = jnp.where(q_lane == 0, -shift, 0.0).astype(BF16)

    def sweep():
        acc_ref[...] = jnp.zeros(acc_ref.shape, F32)
        den_ref[...] = jnp.zeros(den_ref.shape, F32)

        def kv_step(c, carry):
            k_aug = jnp.concatenate([k_ref[kv_rows(c), :], k_one], axis=1)
            vt = vt_ref[c]
            for h in range(GQA_GROUP):
                s_t = lax.dot_general(k_aug, qa_ref[h], contract_cols,
                                      preferred_element_type=F32)
                p_t = jnp.exp2(s_t)
                den_ref[h] += jnp.sum(p_t.reshape(ATT_TK // 8, 8, ATT_TQ), axis=0)
                acc_ref[h] += jnp.dot(vt, p_t.astype(BF16), preferred_element_type=F32)
            return carry
        lax.fori_loop(0, n_kv, kv_step, 0)

    ones = jnp.ones((HEAD_DIM, HEAD_DIM), BF16)
    for h in range(GQA_GROUP):
        qh = q_ref[:, head_cols[h]]
        qa_ref[h, :, 0:HEAD_DIM] = qh
        qf = qh.astype(F32)
        q2 = jnp.dot((qf * qf).astype(BF16), ones, preferred_element_type=F32)
        bound2 = q2 * k2max_ref[0:1, :]
        set_shift(h, bound2 * lax.rsqrt(bound2 + TINY))
    sweep()

    def denominator(h):
        return jnp.sum(den_ref[h], axis=0, keepdims=True)

    denom_min = functools.reduce(jnp.minimum,
                                 [jnp.min(denominator(h)) for h in range(GQA_GROUP)])

    @pl.when(denom_min < ATT_DENOM_FLOOR)
    def _():
        for h in range(GQA_GROUP):
            def body(c, m):
                s = lax.dot_general(q_ref[:, head_cols[h]], k_ref[kv_rows(c), :], contract_cols,
                                    preferred_element_type=F32)
                return jnp.maximum(m, jnp.max(s, axis=1, keepdims=True))
            set_shift(h, lax.fori_loop(0, n_kv, body, jnp.full((ATT_TQ, 1), NEG_BIG, F32)))
        sweep()

    for h in range(GQA_GROUP):
        o_t = acc_ref[h] * (1.0 / denominator(h))
        o_ref[:, head_cols[h]] = o_t.T.astype(BF16)


def _attention(proj, batch, seq):
    n = proj.shape[0]
    assert seq % ATT_TQ == 0 and seq % ATT_TK == 0
    n_q = seq // ATT_TQ
    n_kv = seq // ATT_TK
    group_w = GQA_GROUP * HEAD_DIM
    q_col0 = 4 * RET_WIDTH // group_w
    k_col0 = (4 * RET_WIDTH + ATTN_WIDTH) // HEAD_DIM
    v_col0 = k_col0 + ATTN_KV_HEADS
    vmem = (2 * ATT_TQ * group_w * 2 * 2 + 2 * 2 * seq * HEAD_DIM * 2
            + HEAD_DIM * seq * 2 + GQA_GROUP * (HEAD_DIM + 8) * ATT_TQ * 4
            + GQA_GROUP * ATT_TQ * 2 * HEAD_DIM * 2 + 2 * GQA_GROUP * ATT_TK * ATT_TQ * 4)
    return pl.pallas_call(
        functools.partial(_attention_kernel, n_kv=n_kv),
        out_shape=jax.ShapeDtypeStruct((n, ATTN_WIDTH), BF16),
        grid=(batch, ATTN_KV_HEADS, n_q),
        in_specs=[
            pl.BlockSpec((ATT_TQ, group_w), lambda b, g, i: (b * n_q + i, q_col0 + g)),
            pl.BlockSpec((seq, HEAD_DIM), lambda b, g, i: (b, k_col0 + g)),
            pl.BlockSpec((seq, HEAD_DIM), lambda b, g, i: (b, v_col0 + g)),
        ],
        out_specs=pl.BlockSpec((ATT_TQ, group_w), lambda b, g, i: (b * n_q + i, g)),
        scratch_shapes=[
            pltpu.VMEM((n_kv, HEAD_DIM, ATT_TK), BF16),
            pltpu.VMEM((8, HEAD_DIM), F32),
            pltpu.VMEM((GQA_GROUP, ATT_TQ, 2 * HEAD_DIM), BF16),
            pltpu.VMEM((GQA_GROUP, HEAD_DIM, ATT_TQ), F32),
            pltpu.VMEM((GQA_GROUP, 8, ATT_TQ), F32),
        ],
        compiler_params=pltpu.CompilerParams(
            dimension_semantics=("arbitrary", "arbitrary", "arbitrary"),
            vmem_limit_bytes=_vmem_limit(vmem)),
        name="attention",
    )(proj, proj, proj)


def _mem_kv_kernel(m_ref, nw_ref, w_ref, o_ref):
    m = m_ref[...]
    h = (m * _rms_scale(m) * nw_ref[...]).astype(BF16)
    o_ref[...] = jnp.dot(h, w_ref[...], preferred_element_type=F32).astype(BF16)


def _mem_kv(mem2d, norm_w, w_kv, batch):
    rows = mem2d.shape[0] // batch
    vmem = 2 * rows * D_MODEL * 4 + 2 * D_MODEL * 2 * CROSS_WIDTH * 2 + 2 * rows * 2 * CROSS_WIDTH * 2
    return pl.pallas_call(
        _mem_kv_kernel,
        out_shape=jax.ShapeDtypeStruct((mem2d.shape[0], 2 * CROSS_WIDTH), BF16),
        grid=(batch,),
        in_specs=[
            pl.BlockSpec((rows, D_MODEL), lambda b: (b, 0)),
            pl.BlockSpec((1, D_MODEL), lambda b: (0, 0)),
            pl.BlockSpec((D_MODEL, 2 * CROSS_WIDTH), lambda b: (0, 0)),
        ],
        out_specs=pl.BlockSpec((rows, 2 * CROSS_WIDTH), lambda b: (b, 0)),
        compiler_params=pltpu.CompilerParams(
            dimension_semantics=("arbitrary",),
            vmem_limit_bytes=_vmem_limit(vmem)),
        name="mem_kv",
    )(mem2d, norm_w, w_kv)


def _out_cross_kernel(x_ref, yr_ref, ya_ref, wo_ref, nw_ref, wq_ref, kv_ref, wco_ref, o_ref):
    contract_cols = (((1,), (1,)), ((), ()))
    scale = CROSS_HEAD_DIM ** -0.5

    def body(r, carry):
        rows = pl.ds(pl.multiple_of(r * OC_RC, OC_RC), OC_RC)
        x1 = (x_ref[rows, :]
              + jnp.dot(yr_ref[rows, :], wo_ref[0:RET_WIDTH, :], preferred_element_type=F32)
              + jnp.dot(ya_ref[rows, :], wo_ref[RET_WIDTH:, :], preferred_element_type=F32))
        h = (x1 * _rms_scale(x1) * nw_ref[...]).astype(BF16)
        q = jnp.dot(h, wq_ref[...], preferred_element_type=F32).astype(BF16)
        heads = []
        for hd in range(CROSS_HEADS):
            cols = slice(hd * CROSS_HEAD_DIM, (hd + 1) * CROSS_HEAD_DIM)
            k = kv_ref[:, cols]
            v = kv_ref[:, CROSS_WIDTH + hd * CROSS_HEAD_DIM:CROSS_WIDTH + (hd + 1) * CROSS_HEAD_DIM]
            s = lax.dot_general(q[:, cols], k, contract_cols, preferred_element_type=F32) * scale
            e = jnp.exp(s - jnp.max(s, axis=-1, keepdims=True))
            p = (e / jnp.sum(e, axis=-1, keepdims=True)).astype(BF16)
            heads.append(jnp.dot(p, v, preferred_element_type=F32).astype(BF16))
        o = jnp.concatenate(heads, axis=1)
        o_ref[rows, :] = x1 + jnp.dot(o, wco_ref[...], preferred_element_type=F32)
        return carry
    lax.fori_loop(0, OC_TM // OC_RC, body, 0)


def _out_cross(x2d, y_ret, y_attn, w_out, norm_w, w_q, mem_kv, w_co, seq):
    n = x2d.shape[0]
    assert n % OC_TM == 0 and seq % OC_TM == 0
    tiles_per_seq = seq // OC_TM
    vmem = (2 * 2 * OC_TM * D_MODEL * 4 + 2 * 2 * OC_TM * RET_WIDTH * 2 + 2 * D_MODEL * D_MODEL * 2
            + 2 * 2 * D_MODEL * CROSS_WIDTH * 2 + 2 * MEM_TOKENS * 2 * CROSS_WIDTH * 2)
    return pl.pallas_call(
        _out_cross_kernel,
        out_shape=jax.ShapeDtypeStruct((n, D_MODEL), F32),
        grid=(n // OC_TM,),
        in_specs=[
            pl.BlockSpec((OC_TM, D_MODEL), lambda i: (i, 0)),
            pl.BlockSpec((OC_TM, RET_WIDTH), lambda i: (i, 0)),
            pl.BlockSpec((OC_TM, ATTN_WIDTH), lambda i: (i, 0)),
            pl.BlockSpec((D_MODEL, D_MODEL), lambda i: (0, 0)),
            pl.BlockSpec((1, D_MODEL), lambda i: (0, 0)),
            pl.BlockSpec((D_MODEL, CROSS_WIDTH), lambda i: (0, 0)),
            pl.BlockSpec((MEM_TOKENS, 2 * CROSS_WIDTH), lambda i: (i // tiles_per_seq, 0)),
            pl.BlockSpec((CROSS_WIDTH, D_MODEL), lambda i: (0, 0)),
        ],
        out_specs=pl.BlockSpec((OC_TM, D_MODEL), lambda i: (i, 0)),
        compiler_params=pltpu.CompilerParams(
            dimension_semantics=("arbitrary",),
            vmem_limit_bytes=_vmem_limit(vmem)),
        name="out_cross",
    )(x2d, y_ret, y_attn, w_out, norm_w, w_q, mem_kv, w_co)


def _mlp_kernel(x_ref, nw_ref, wu_ref, wd_ref, fw_ref, o_ref, h_ref):
    j = pl.program_id(1)

    def row_chunk(r):
        return pl.ds(pl.multiple_of(r * MLP_RC, MLP_RC), MLP_RC)

    @pl.when(j == 0)
    def _():
        def body(r, c):
            rows = row_chunk(r)
            xr = x_ref[rows, :]
            h_ref[rows, :] = (xr * _rms_scale(xr) * nw_ref[...]).astype(BF16)
            o_ref[rows, :] = xr
            return c
        lax.fori_loop(0, MLP_TM // MLP_RC, body, 0)

    def body(r, c):
        rows = row_chunk(r)
        u = jnp.dot(h_ref[rows, :], wu_ref[...], preferred_element_type=F32)
        u = jnp.maximum(u, 0.0)
        a = (u * u).astype(BF16)
        o_ref[rows, :] += jnp.dot(a, wd_ref[...], preferred_element_type=F32)
        return c
    lax.fori_loop(0, MLP_TM // MLP_RC, body, 0)

    @pl.when(j == pl.num_programs(1) - 1)
    def _():
        def body(r, c):
            rows = row_chunk(r)
            y = o_ref[rows, :]
            o_ref[rows, :] = y * _rms_scale(y) * fw_ref[...]
            return c
        lax.fori_loop(0, MLP_TM // MLP_RC, body, 0)


def _mlp(x2d, norm_w, w_up, w_down, final_w):
    n = x2d.shape[0]
    assert n % MLP_TM == 0 and D_FF % MLP_TF == 0
    vmem = (2 * 2 * MLP_TM * D_MODEL * 4 + MLP_TM * D_MODEL * 2 + 2 * 2 * D_MODEL * MLP_TF * 2)
    return pl.pallas_call(
        _mlp_kernel,
        out_shape=jax.ShapeDtypeStruct((n, D_MODEL), F32),
        grid=(n // MLP_TM, D_FF // MLP_TF),
        in_specs=[
            pl.BlockSpec((MLP_TM, D_MODEL), lambda i, j: (i, 0)),
            pl.BlockSpec((1, D_MODEL), lambda i, j: (0, 0)),
            pl.BlockSpec((D_MODEL, MLP_TF), lambda i, j: (0, j)),
            pl.BlockSpec((MLP_TF, D_MODEL), lambda i, j: (j, 0)),
            pl.BlockSpec((1, D_MODEL), lambda i, j: (0, 0)),
        ],
        out_specs=pl.BlockSpec((MLP_TM, D_MODEL), lambda i, j: (i, 0)),
        scratch_shapes=[pltpu.VMEM((MLP_TM, D_MODEL), BF16)],
        compiler_params=pltpu.CompilerParams(
            dimension_semantics=("arbitrary", "arbitrary"),
            vmem_limit_bytes=_vmem_limit(vmem)),
        name="mlp",
    )(x2d, norm_w, w_up, w_down, final_w)


def _rope_tables(seq):
    t = jnp.arange(seq, dtype=jnp.int32)
    row = (t // GRID_W).astype(F32)
    col = (t % GRID_W).astype(F32)
    inv_freq = 1.0 / (ROPE_THETA ** (jnp.arange(0, AXIS_DIM, 2, dtype=F32) / AXIS_DIM))
    ang_r = row[:, None] * inv_freq[None, :]
    ang_c = col[:, None] * inv_freq[None, :]
    cos_t = jnp.concatenate([jnp.cos(ang_r), jnp.cos(ang_c), jnp.cos(ang_r), jnp.cos(ang_c)], axis=-1)
    sin_t = jnp.concatenate([-jnp.sin(ang_r), -jnp.sin(ang_c), jnp.sin(ang_r), jnp.sin(ang_c)], axis=-1)
    return cos_t, sin_t


def _pair_heads(w):
    lead = w.shape[:-1]
    heads = w.shape[-1] // HEAD_DIM
    quarter = HEAD_DIM // 4
    w = w.reshape(lead + (heads, 2, 2, quarter))
    return jnp.swapaxes(w, -2, -3).reshape(lead + (heads * HEAD_DIM,))


def _w_in_prep_kernel(w_ref, o_ref):
    lane = lax.broadcasted_iota(jnp.int32, (WPREP_ROWS, HEAD_DIM), 1)
    quarter = HEAD_DIM // 4
    takes_c1 = (lane >= quarter) & (lane < 2 * quarter)
    takes_r2 = (lane >= 2 * quarter) & (lane < 3 * quarter)
    rope_slabs = (set(range(0, 2 * RET_HEADS))
                  | set(range(4 * RET_HEADS, 4 * RET_HEADS + ATTN_HEADS + ATTN_KV_HEADS)))
    for s in range(IN_WIDTH // HEAD_DIM):
        cols = slice(s * HEAD_DIM, (s + 1) * HEAD_DIM)
        y = w_ref[:, cols]
        if s in rope_slabs:
            y = jnp.where(takes_c1, pltpu.roll(y, HEAD_DIM - quarter, 1),
                          jnp.where(takes_r2, pltpu.roll(y, quarter, 1), y))
        o_ref[:, cols] = y.astype(BF16)


def _in_proj_weights(w_in):
    rows = w_in.shape[0]
    assert rows % WPREP_ROWS == 0
    vmem = 2 * WPREP_ROWS * IN_WIDTH * (4 + 2)
    return pl.pallas_call(
        _w_in_prep_kernel,
        out_shape=jax.ShapeDtypeStruct((rows, IN_WIDTH), BF16),
        grid=(rows // WPREP_ROWS,),
        in_specs=[pl.BlockSpec((WPREP_ROWS, IN_WIDTH), lambda i: (i, 0))],
        out_specs=pl.BlockSpec((WPREP_ROWS, IN_WIDTH), lambda i: (i, 0)),
        compiler_params=pltpu.CompilerParams(
            dimension_semantics=("arbitrary",),
            vmem_limit_bytes=_vmem_limit(vmem)),
        name="w_in_prep",
    )(w_in)


def kernel(x, mem, norm_mix_w, w_in, ret_decay_fwd, ret_decay_bwd, ret_gn_w, ret_gn_b, attn_q_norm_w, attn_k_norm_w, w_out, norm_cross_w, norm_mem_w, w_cross_q, w_cross_k, w_cross_v, w_cross_o, norm_mlp_w, w_mlp_up, w_mlp_down, norm_final_w):
    batch, seq, _ = x.shape
    assert w_in.shape[0] == 1, "single-layer block: per-layer parameters have a leading axis of 1"
    cos_t, sin_t = _rope_tables(seq)
    xs = x.reshape(batch * seq, D_MODEL)
    mem2d = mem.reshape(batch * MEM_TOKENS, D_MODEL)
    dec = jnp.stack([ret_decay_fwd[0], ret_decay_bwd[0]], axis=1)
    dec = jnp.broadcast_to(dec[:, :, None], (RET_HEADS, 2, RET_C)).astype(F32)
    proj = _in_proj(xs, norm_mix_w, _in_proj_weights(w_in[0]), cos_t, sin_t,
                    _pair_heads(attn_q_norm_w), _pair_heads(attn_k_norm_w), seq)
    y_ret = _retention(proj, dec, ret_gn_w, ret_gn_b, batch, seq)
    y_attn = _attention(proj, batch, seq)
    w_kv = jnp.concatenate([w_cross_k[0], w_cross_v[0]], axis=1).astype(BF16)
    mem_kv = _mem_kv(mem2d, norm_mem_w, w_kv, batch)
    xs = _out_cross(xs, y_ret, y_attn, w_out[0].astype(BF16), norm_cross_w,
                    w_cross_q[0].astype(BF16), mem_kv, w_cross_o[0].astype(BF16), seq)
    xs = _mlp(xs, norm_mlp_w, w_mlp_up[0].astype(BF16), w_mlp_down[0].astype(BF16),
              norm_final_w[None, :])
    return xs.reshape(batch, seq, D_MODEL)
```

```python
import functools
import math

import numpy as np
import jax
import jax.numpy as jnp
from jax import lax
from jax.experimental import pallas as pl
from jax.experimental.pallas import tpu as pltpu

D_MODEL = 2048
HEAD_DIM = 128
RET_WIDTH = 1024
ATTN_WIDTH = 1024
RET_HEADS = 8
ATTN_HEADS = 8
ATTN_KV_HEADS = 2
GQA_GROUP = ATTN_HEADS // ATTN_KV_HEADS
KV_WIDTH = ATTN_KV_HEADS * HEAD_DIM
IN_WIDTH = 4 * RET_WIDTH + ATTN_WIDTH + 2 * KV_WIDTH
GRID_W = 64
AXIS_DIM = HEAD_DIM // 2
ROPE_THETA = 10000.0
MEM_TOKENS = 256
CROSS_HEADS = 4
CROSS_HEAD_DIM = 128
CROSS_WIDTH = CROSS_HEADS * CROSS_HEAD_DIM
D_FF = 4 * D_MODEL
NORM_EPS = 1e-6

V7X_LANES = 128
V7X_BF16_SUBLANES = 16
V7X_VMEM_BYTES = 64 * 1024 * 1024

F32 = jnp.float32
BF16 = jnp.bfloat16
LOG2E = math.log2(math.e)
NEG_BIG = -1e30
TINY = 1e-30

IN_TM = 512
IN_TN = 512
IN_NORM_RC = 256
WPREP_ROWS = 256
RET_C = 256
RET_TS = 2048
RET_HG = 4
ATT_TQ = 512
ATT_TK = 4096
ATT_DENOM_FLOOR = 2.0 ** -40
OC_TM = 512
OC_RC = 512
MLP_TM = 1024
MLP_TF = 512
MLP_RC = 1024


def _vmem_limit(nbytes):
    return int(min(nbytes + 16 * 1024 * 1024, V7X_VMEM_BYTES - 8 * 1024 * 1024))


def _rms_scale(y):
    return lax.rsqrt(jnp.mean(y * y, axis=-1, keepdims=True) + NORM_EPS)


def _rope(y, cos, sin_signed):
    return y * cos + pltpu.roll(y, HEAD_DIM // 2, 1) * sin_signed


def _in_proj_kernel(x_ref, nw_ref, w_ref, cos_ref, sin_ref, qg_ref, kg_ref, o_ref, h_ref):
    for r in range(IN_TM // IN_NORM_RC):
        rows = slice(r * IN_NORM_RC, (r + 1) * IN_NORM_RC)
        xr = x_ref[rows, :]
        h_ref[rows, :] = (xr * _rms_scale(xr) * nw_ref[...]).astype(BF16)

    ret_k_scale = HEAD_DIM ** -0.5
    attn_q_scale = (HEAD_DIM ** -0.5) * LOG2E

    n_slabs = IN_TN // HEAD_DIM

    def run(j, epilogue):
        tile = slice(j * IN_TN, (j + 1) * IN_TN)
        acc = jnp.dot(h_ref[...], w_ref[:, tile], preferred_element_type=F32)
        slabs = epilogue([acc[:, s * HEAD_DIM:(s + 1) * HEAD_DIM] for s in range(n_slabs)],
                         cos_ref[...], sin_ref[...])
        for s, y in enumerate(slabs):
            c0 = j * IN_TN + s * HEAD_DIM
            o_ref[:, c0:c0 + HEAD_DIM] = y.astype(BF16)

    def ret_q(ys, cos, sin):
        return [_rope(y, cos, sin) for y in ys]

    def ret_k(ys, cos, sin):
        return [_rope(y, cos, sin) * ret_k_scale for y in ys]

    def plain(ys, cos, sin):
        return ys

    def head_inv_rms(y_a, y_b):
        width = 2 * HEAD_DIM
        r = lax.broadcasted_iota(jnp.int32, (width, width), 0) // HEAD_DIM
        c = lax.broadcasted_iota(jnp.int32, (width, width), 1) // HEAD_DIM
        ones_bd = jnp.where(r == c, 1.0, 0.0).astype(BF16)
        sq = jnp.concatenate([(y_a * y_a).astype(BF16), (y_b * y_b).astype(BF16)], axis=1)
        ms = jnp.dot(sq, ones_bd, preferred_element_type=F32) * (1.0 / HEAD_DIM)
        inv = lax.rsqrt(ms + NORM_EPS)
        return inv[:, :HEAD_DIM], inv[:, HEAD_DIM:]

    def normed_rope(y_a, y_b, gain, scale, cos, sin):
        inv_a, inv_b = head_inv_rms(y_a, y_b)
        return [_rope(y_a * gain, cos, sin) * (inv_a * scale),
                _rope(y_b * gain, cos, sin) * (inv_b * scale)]

    def attn_q(ys, cos, sin):
        gain = qg_ref[...]
        return (normed_rope(ys[0], ys[1], gain, attn_q_scale, cos, sin)
                + normed_rope(ys[2], ys[3], gain, attn_q_scale, cos, sin))

    def attn_kv(ys, cos, sin):
        return normed_rope(ys[0], ys[1], kg_ref[...], 1.0, cos, sin) + ys[ATTN_KV_HEADS:]

    nb = RET_WIDTH // IN_TN
    for j in range(IN_WIDTH // IN_TN):
        if j < nb:
            run(j, ret_q)
        elif j < 2 * nb:
            run(j, ret_k)
        elif j < 4 * nb:
            run(j, plain)
        elif j < 5 * nb:
            run(j, attn_q)
        else:
            run(j, attn_kv)


def _in_proj(x2d, norm_w, w_in, cos_t, sin_t, q_gain, k_gain, seq):
    n = x2d.shape[0]
    assert IN_WIDTH % IN_TN == 0 and 2 * KV_WIDTH == IN_TN and n % IN_TM == 0 and seq % IN_TM == 0
    tiles_per_seq = seq // IN_TM
    vmem = (2 * IN_TM * D_MODEL * 4 + IN_TM * D_MODEL * 2 + D_MODEL * IN_WIDTH * 2
            + 2 * IN_TM * IN_WIDTH * 2 + 4 * IN_TM * HEAD_DIM * 4)
    resident = pl.Buffered(1)
    return pl.pallas_call(
        _in_proj_kernel,
        out_shape=jax.ShapeDtypeStruct((n, IN_WIDTH), BF16),
        grid=(n // IN_TM,),
        in_specs=[
            pl.BlockSpec((IN_TM, D_MODEL), lambda i: (i, 0)),
            pl.BlockSpec((1, D_MODEL), lambda i: (0, 0)),
            pl.BlockSpec((D_MODEL, IN_WIDTH), lambda i: (0, 0), pipeline_mode=resident),
            pl.BlockSpec((IN_TM, HEAD_DIM), lambda i: (i % tiles_per_seq, 0)),
            pl.BlockSpec((IN_TM, HEAD_DIM), lambda i: (i % tiles_per_seq, 0)),
            pl.BlockSpec((1, HEAD_DIM), lambda i: (0, 0)),
            pl.BlockSpec((1, HEAD_DIM), lambda i: (0, 0)),
        ],
        out_specs=pl.BlockSpec((IN_TM, IN_WIDTH), lambda i: (i, 0)),
        scratch_shapes=[pltpu.VMEM((IN_TM, D_MODEL), BF16)],
        compiler_params=pltpu.CompilerParams(
            dimension_semantics=("arbitrary",),
            vmem_limit_bytes=_vmem_limit(vmem)),
        name="in_proj",
    )(x2d, norm_w, w_in, cos_t, sin_t, q_gain, k_gain)


def _retention_kernel(dec_ref, q_ref, k_ref, v_ref, g_ref, gw_ref, gb_ref, o_ref,
                      d_ref, tab_ref, sb_ref, st_ref, *, n_seq):
    s = pl.program_id(2)
    c_len = RET_C
    cps = RET_TS // RET_C
    contract_rows = (((0,), (0,)), ((), ()))
    contract_cols = (((1,), (1,)), ((), ()))
    head_cols = [slice(h * HEAD_DIM, (h + 1) * HEAD_DIM) for h in range(RET_HG)]
    chunk_rows = [slice(c * c_len, (c + 1) * c_len) for c in range(cps)]

    def log_gammas(h):
        return -jnp.exp(dec_ref[h, 0:1, :]), -jnp.exp(dec_ref[h, 1:2, :])

    @pl.when(s == 0)
    def _():
        row = lax.broadcasted_iota(jnp.int32, (c_len, c_len), 0)
        col = lax.broadcasted_iota(jnp.int32, (c_len, c_len), 1)
        rel = (row - col).astype(F32)
        idx = lax.broadcasted_iota(jnp.int32, (c_len, HEAD_DIM), 0).astype(F32)
        for h in range(RET_HG):
            lg_f, lg_b = log_gammas(h)
            d_ref[h] = jnp.exp(jnp.where(rel >= 0, lg_f * rel, -lg_b * rel))
            lf = lg_f[:, :HEAD_DIM]
            lb = lg_b[:, :HEAD_DIM]
            tab_ref[h, 0] = jnp.exp(lf * (idx + 1.0))
            tab_ref[h, 1] = jnp.exp(lb * (c_len - idx))
            tab_ref[h, 2] = jnp.exp(lf * (c_len - 1.0 - idx))
            tab_ref[h, 3] = jnp.exp(lb * idx)
        st_ref[...] = jnp.zeros(st_ref.shape, F32)

    @pl.when(s < n_seq)
    def _():
        first_chunk = (n_seq - 1 - s) * cps
        for h in range(RET_HG):
            chunk_b = jnp.exp(log_gammas(h)[1][:, :HEAD_DIM] * c_len)
            state = st_ref[h]
            for c in reversed(range(cps)):
                sb_ref[first_chunk + c, h] = state.astype(BF16)
                k = k_ref[chunk_rows[c], head_cols[h]]
                kd = (k.astype(F32) * tab_ref[h, 3]).astype(BF16)
                upd = lax.dot_general(kd, v_ref[chunk_rows[c], head_cols[h]], contract_rows,
                                      preferred_element_type=F32)
                state = state * chunk_b + upd
            st_ref[h] = state

        @pl.when(s == n_seq - 1)
        def _():
            st_ref[...] = jnp.zeros(st_ref.shape, F32)

    @pl.when(s >= n_seq)
    def _():
        first_chunk = (s - n_seq) * cps
        for h in range(RET_HG):
            chunk_f = jnp.exp(log_gammas(h)[0][:, :HEAD_DIM] * c_len)
            gw = gw_ref[:, head_cols[h]]
            gb = gb_ref[:, head_cols[h]]
            state = st_ref[h]
            for c in range(cps):
                q = q_ref[chunk_rows[c], head_cols[h]]
                k = k_ref[chunk_rows[c], head_cols[h]]
                v = v_ref[chunk_rows[c], head_cols[h]]
                scores = lax.dot_general(q, k, contract_cols, preferred_element_type=F32)
                masked = (scores * d_ref[h]).astype(BF16)
                qf = q.astype(F32)
                lhs = jnp.concatenate(
                    [masked, (qf * tab_ref[h, 0]).astype(BF16), (qf * tab_ref[h, 1]).astype(BF16)],
                    axis=1)
                rhs = jnp.concatenate([v, state.astype(BF16), sb_ref[first_chunk + c, h]], axis=0)
                out = jnp.dot(lhs, rhs, preferred_element_type=F32)
                mu = jnp.mean(out, axis=-1, keepdims=True)
                cen = out - mu
                var = jnp.mean(cen * cen, axis=-1, keepdims=True)
                y = cen * lax.rsqrt(var + NORM_EPS) * gw + gb
                g = g_ref[chunk_rows[c], head_cols[h]].astype(F32)
                y = y * (g / (1.0 + jnp.exp(-g)))
                o_ref[chunk_rows[c], head_cols[h]] = y.astype(BF16)
                kd = (k.astype(F32) * tab_ref[h, 2]).astype(BF16)
                upd = lax.dot_general(kd, v, contract_rows, preferred_element_type=F32)
                state = state * chunk_f + upd
            st_ref[h] = state


def _retention(proj, dec, gn_w, gn_b, batch, seq):
    n = proj.shape[0]
    assert seq % RET_TS == 0 and RET_TS % RET_C == 0 and RET_HEADS % RET_HG == 0
    n_seq = seq // RET_TS
    n_chunks = seq // RET_C
    group_w = RET_HG * HEAD_DIM
    groups = RET_HEADS // RET_HG

    def kv_block(section):
        def index(b, hg, s):
            blk = jnp.where(s < n_seq, n_seq - 1 - s, s - n_seq)
            return (b * n_seq + blk, section * groups + hg)
        return index

    def fwd_block(section):
        return lambda b, hg, s: (b * n_seq + jnp.maximum(s - n_seq, 0), section * groups + hg)

    vmem = (2 * 5 * RET_TS * group_w * 2 + RET_HG * RET_C * RET_C * 4
            + RET_HG * 4 * RET_C * HEAD_DIM * 4 + n_chunks * RET_HG * HEAD_DIM * HEAD_DIM * 2
            + RET_HG * HEAD_DIM * HEAD_DIM * 4)
    return pl.pallas_call(
        functools.partial(_retention_kernel, n_seq=n_seq),
        out_shape=jax.ShapeDtypeStruct((n, RET_WIDTH), BF16),
        grid=(batch, groups, 2 * n_seq),
        in_specs=[
            pl.BlockSpec((RET_HG, 2, RET_C), lambda b, hg, s: (hg, 0, 0)),
            pl.BlockSpec((RET_TS, group_w), fwd_block(0)),
            pl.BlockSpec((RET_TS, group_w), kv_block(1)),
            pl.BlockSpec((RET_TS, group_w), kv_block(2)),
            pl.BlockSpec((RET_TS, group_w), fwd_block(3)),
            pl.BlockSpec((1, group_w), lambda b, hg, s: (0, hg)),
            pl.BlockSpec((1, group_w), lambda b, hg, s: (0, hg)),
        ],
        out_specs=pl.BlockSpec((RET_TS, group_w), fwd_block(0)),
        scratch_shapes=[
            pltpu.VMEM((RET_HG, RET_C, RET_C), F32),
            pltpu.VMEM((RET_HG, 4, RET_C, HEAD_DIM), F32),
            pltpu.VMEM((n_chunks, RET_HG, HEAD_DIM, HEAD_DIM), BF16),
            pltpu.VMEM((RET_HG, HEAD_DIM, HEAD_DIM), F32),
        ],
        compiler_params=pltpu.CompilerParams(
            dimension_semantics=("arbitrary", "arbitrary", "arbitrary"),
            vmem_limit_bytes=_vmem_limit(vmem)),
        name="retention",
    )(dec, proj, proj, proj, proj, gn_w, gn_b)


def _attention_kernel(q_ref, k_ref, v_ref, o_ref, vt_ref, k2max_ref, qa_ref, acc_ref, den_ref,
                      *, n_kv):
    i = pl.program_id(2)
    contract_cols = (((1,), (1,)), ((), ()))

    def kv_rows(c):
        return pl.ds(pl.multiple_of(c * ATT_TK, ATT_TK), ATT_TK)

    @pl.when(i == 0)
    def _():
        def body(c, k2max):
            rows = kv_rows(c)
            vt_ref[c] = v_ref[rows, :].astype(F32).T.astype(BF16)
            kc = k_ref[rows, :].astype(F32)
            k2 = jnp.max(jnp.sum(kc * kc, axis=1, keepdims=True), axis=0, keepdims=True)
            return jnp.maximum(k2max, k2)
        k2max = lax.fori_loop(0, n_kv, body, jnp.zeros((1, 1), F32))
        k2max_ref[...] = jnp.broadcast_to(k2max, k2max_ref.shape)

    q_lane = lax.broadcasted_iota(jnp.int32, (ATT_TQ, HEAD_DIM), 1)
    k_lane = lax.broadcasted_iota(jnp.int32, (ATT_TK, HEAD_DIM), 1)
    k_one = jnp.where(k_lane == 0, 1.0, 0.0).astype(BF16)
    head_cols = [slice(h * HEAD_DIM, (h + 1) * HEAD_DIM) for h in range(GQA_GROUP)]

    def set_shift(h, shift):
        qa_ref[h, :, HEAD_DIM:] = jnp.where(q_lane == 0, -shift, 0.0).astype(BF16)

    def sweep():
        acc_ref[...] = jnp.zeros(acc_ref.shape, F32)
        den_ref[...] = jnp.zeros(den_ref.shape, F32)

        def kv_step(c, carry):
            k_aug = jnp.concatenate([k_ref[kv_rows(c), :], k_one], axis=1)
            vt = vt_ref[c]
            for h in range(GQA_GROUP):
                s_t = lax.dot_general(k_aug, qa_ref[h], contract_cols,
                                      preferred_element_type=F32)
                p_t = jnp.exp2(s_t)
                den_ref[h] += jnp.sum(p_t.reshape(ATT_TK // 8, 8, ATT_TQ), axis=0)
                acc_ref[h] += jnp.dot(vt, p_t.astype(BF16), preferred_element_type=F32)
            return carry
        lax.fori_loop(0, n_kv, kv_step, 0)

    ones = jnp.ones((HEAD_DIM, HEAD_DIM), BF16)
    for h in range(GQA_GROUP):
        qh = q_ref[:, head_cols[h]]
        qa_ref[h, :, 0:HEAD_DIM] = qh
        qf = qh.astype(F32)
        q2 = jnp.dot((qf * qf).astype(BF16), ones, preferred_element_type=F32)
        bound2 = q2 * k2max_ref[0:1, :]
        set_shift(h, bound2 * lax.rsqrt(bound2 + TINY))
    sweep()

    def denominator(h):
        return jnp.sum(den_ref[h], axis=0, keepdims=True)

    denom_min = functools.reduce(jnp.minimum,
                                 [jnp.min(denominator(h)) for h in range(GQA_GROUP)])

    @pl.when(denom_min < ATT_DENOM_FLOOR)
    def _():
        for h in range(GQA_GROUP):
            def body(c, m):
                s = lax.dot_general(q_ref[:, head_cols[h]], k_ref[kv_rows(c), :], contract_cols,
                                    preferred_element_type=F32)
                return jnp.maximum(m, jnp.max(s, axis=1, keepdims=True))
            set_shift(h, lax.fori_loop(0, n_kv, body, jnp.full((ATT_TQ, 1), NEG_BIG, F32)))
        sweep()

    for h in range(GQA_GROUP):
        o_t = acc_ref[h] * (1.0 / denominator(h))
        o_ref[:, head_cols[h]] = o_t.T.astype(BF16)


def _attention(proj, batch, seq):
    n = proj.shape[0]
    assert seq % ATT_TQ == 0 and seq % ATT_TK == 0
    n_q = seq // ATT_TQ
    n_kv = seq // ATT_TK
    group_w = GQA_GROUP * HEAD_DIM
    q_col0 = 4 * RET_WIDTH // group_w
    k_col0 = (4 * RET_WIDTH + ATTN_WIDTH) // HEAD_DIM
    v_col0 = k_col0 + ATTN_KV_HEADS
    vmem = (2 * ATT_TQ * group_w * 2 * 2 + 2 * 2 * seq * HEAD_DIM * 2
            + HEAD_DIM * seq * 2 + GQA_GROUP * (HEAD_DIM + 8) * ATT_TQ * 4
            + GQA_GROUP * ATT_TQ * 2 * HEAD_DIM * 2 + 2 * GQA_GROUP * ATT_TK * ATT_TQ * 4)
    return pl.pallas_call(
        functools.partial(_attention_kernel, n_kv=n_kv),
        out_shape=jax.ShapeDtypeStruct((n, ATTN_WIDTH), BF16),
        grid=(batch, ATTN_KV_HEADS, n_q),
        in_specs=[
            pl.BlockSpec((ATT_TQ, group_w), lambda b, g, i: (b * n_q + i, q_col0 + g)),
            pl.BlockSpec((seq, HEAD_DIM), lambda b, g, i: (b, k_col0 + g)),
            pl.BlockSpec((seq, HEAD_DIM), lambda b, g, i: (b, v_col0 + g)),
        ],
        out_specs=pl.BlockSpec((ATT_TQ, group_w), lambda b, g, i: (b * n_q + i, g)),
        scratch_shapes=[
            pltpu.VMEM((n_kv, HEAD_DIM, ATT_TK), BF16),
            pltpu.VMEM((8, HEAD_DIM), F32),
            pltpu.VMEM((GQA_GROUP, ATT_TQ, 2 * HEAD_DIM), BF16),
            pltpu.VMEM((GQA_GROUP, HEAD_DIM, ATT_TQ), F32),
            pltpu.VMEM((GQA_GROUP, 8, ATT_TQ), F32),
        ],
        compiler_params=pltpu.CompilerParams(
            dimension_semantics=("arbitrary", "arbitrary", "arbitrary"),
            vmem_limit_bytes=_vmem_limit(vmem)),
        name="attention",
    )(proj, proj, proj)


def _mem_kv_kernel(m_ref, nw_ref, wk_ref, wv_ref, o_ref):
    m = m_ref[...]
    h = (m * _rms_scale(m) * nw_ref[...]).astype(BF16)
    o_ref[:, :CROSS_WIDTH] = jnp.dot(h, wk_ref[...].astype(BF16),
                                     preferred_element_type=F32).astype(BF16)
    o_ref[:, CROSS_WIDTH:] = jnp.dot(h, wv_ref[...].astype(BF16),
                                     preferred_element_type=F32).astype(BF16)


def _mem_kv(mem2d, norm_w, w_k, w_v):
    rows = mem2d.shape[0]
    vmem = 2 * (rows * D_MODEL * 4 + 2 * D_MODEL * CROSS_WIDTH * 4 + rows * 2 * CROSS_WIDTH * 2)
    return pl.pallas_call(
        _mem_kv_kernel,
        out_shape=jax.ShapeDtypeStruct((rows, 2 * CROSS_WIDTH), BF16),
        grid=(1,),
        in_specs=[
            pl.BlockSpec((rows, D_MODEL), lambda i: (0, 0)),
            pl.BlockSpec((1, D_MODEL), lambda i: (0, 0)),
            pl.BlockSpec((D_MODEL, CROSS_WIDTH), lambda i: (0, 0)),
            pl.BlockSpec((D_MODEL, CROSS_WIDTH), lambda i: (0, 0)),
        ],
        out_specs=pl.BlockSpec((rows, 2 * CROSS_WIDTH), lambda i: (0, 0)),
        compiler_params=pltpu.CompilerParams(
            dimension_semantics=("arbitrary",),
            vmem_limit_bytes=_vmem_limit(vmem)),
        name="mem_kv",
    )(mem2d, norm_w, w_k, w_v)


def _out_cross_kernel(x_ref, yr_ref, ya_ref, wo_ref, nw_ref, wq_ref, kv_ref, wco_ref, o_ref):
    contract_cols = (((1,), (1,)), ((), ()))
    scale = CROSS_HEAD_DIM ** -0.5

    def body(r, carry):
        rows = pl.ds(pl.multiple_of(r * OC_RC, OC_RC), OC_RC)
        x1 = (x_ref[rows, :]
              + jnp.dot(yr_ref[rows, :], wo_ref[0:RET_WIDTH, :], preferred_element_type=F32)
              + jnp.dot(ya_ref[rows, :], wo_ref[RET_WIDTH:, :], preferred_element_type=F32))
        h = (x1 * _rms_scale(x1) * nw_ref[...]).astype(BF16)
        q = jnp.dot(h, wq_ref[...], preferred_element_type=F32).astype(BF16)
        heads = []
        for hd in range(CROSS_HEADS):
            cols = slice(hd * CROSS_HEAD_DIM, (hd + 1) * CROSS_HEAD_DIM)
            k = kv_ref[:, cols]
            v = kv_ref[:, CROSS_WIDTH + hd * CROSS_HEAD_DIM:CROSS_WIDTH + (hd + 1) * CROSS_HEAD_DIM]
            s = lax.dot_general(q[:, cols], k, contract_cols, preferred_element_type=F32) * scale
            e = jnp.exp(s - jnp.max(s, axis=-1, keepdims=True))
            p = (e / jnp.sum(e, axis=-1, keepdims=True)).astype(BF16)
            heads.append(jnp.dot(p, v, preferred_element_type=F32).astype(BF16))
        o = jnp.concatenate(heads, axis=1)
        o_ref[rows, :] = x1 + jnp.dot(o, wco_ref[...], preferred_element_type=F32)
        return carry
    lax.fori_loop(0, OC_TM // OC_RC, body, 0)


def _out_cross(x2d, y_ret, y_attn, w_out, norm_w, w_q, mem_kv, w_co, seq):
    n = x2d.shape[0]
    assert n % OC_TM == 0 and seq % OC_TM == 0
    tiles_per_seq = seq // OC_TM
    vmem = (2 * 2 * OC_TM * D_MODEL * 4 + 2 * 2 * OC_TM * RET_WIDTH * 2 + 2 * D_MODEL * D_MODEL * 2
            + 2 * 2 * D_MODEL * CROSS_WIDTH * 2 + 2 * MEM_TOKENS * 2 * CROSS_WIDTH * 2)
    return pl.pallas_call(
        _out_cross_kernel,
        out_shape=jax.ShapeDtypeStruct((n, D_MODEL), F32),
        grid=(n // OC_TM,),
        in_specs=[
            pl.BlockSpec((OC_TM, D_MODEL), lambda i: (i, 0)),
            pl.BlockSpec((OC_TM, RET_WIDTH), lambda i: (i, 0)),
            pl.BlockSpec((OC_TM, ATTN_WIDTH), lambda i: (i, 0)),
            pl.BlockSpec((D_MODEL, D_MODEL), lambda i: (0, 0)),
            pl.BlockSpec((1, D_MODEL), lambda i: (0, 0)),
            pl.BlockSpec((D_MODEL, CROSS_WIDTH), lambda i: (0, 0)),
            pl.BlockSpec((MEM_TOKENS, 2 * CROSS_WIDTH), lambda i: (i // tiles_per_seq, 0)),
            pl.BlockSpec((CROSS_WIDTH, D_MODEL), lambda i: (0, 0)),
        ],
        out_specs=pl.BlockSpec((OC_TM, D_MODEL), lambda i: (i, 0)),
        compiler_params=pltpu.CompilerParams(
            dimension_semantics=("arbitrary",),
            vmem_limit_bytes=_vmem_limit(vmem)),
        name="out_cross",
    )(x2d, y_ret, y_attn, w_out, norm_w, w_q, mem_kv, w_co)


def _mlp_kernel(x_ref, nw_ref, wu_ref, wd_ref, fw_ref, o_ref, h_ref):
    j = pl.program_id(1)

    def row_chunk(r):
        return pl.ds(pl.multiple_of(r * MLP_RC, MLP_RC), MLP_RC)

    @pl.when(j == 0)
    def _():
        def body(r, c):
            rows = row_chunk(r)
            xr = x_ref[rows, :]
            h_ref[rows, :] = (xr * _rms_scale(xr) * nw_ref[...]).astype(BF16)
            o_ref[rows, :] = xr
            return c
        lax.fori_loop(0, MLP_TM // MLP_RC, body, 0)

    def body(r, c):
        rows = row_chunk(r)
        u = jnp.dot(h_ref[rows, :], wu_ref[...], preferred_element_type=F32)
        u = jnp.maximum(u, 0.0)
        a = (u * u).astype(BF16)
        o_ref[rows, :] += jnp.dot(a, wd_ref[...], preferred_element_type=F32)
        return c
    lax.fori_loop(0, MLP_TM // MLP_RC, body, 0)

    @pl.when(j == pl.num_programs(1) - 1)
    def _():
        def body(r, c):
            rows = row_chunk(r)
            y = o_ref[rows, :]
            o_ref[rows, :] = y * _rms_scale(y) * fw_ref[...]
            return c
        lax.fori_loop(0, MLP_TM // MLP_RC, body, 0)


def _mlp(x2d, norm_w, w_up, w_down, final_w):
    n = x2d.shape[0]
    assert n % MLP_TM == 0 and D_FF % MLP_TF == 0
    vmem = (2 * 2 * MLP_TM * D_MODEL * 4 + MLP_TM * D_MODEL * 2 + 2 * 2 * D_MODEL * MLP_TF * 2)
    return pl.pallas_call(
        _mlp_kernel,
        out_shape=jax.ShapeDtypeStruct((n, D_MODEL), F32),
        grid=(n // MLP_TM, D_FF // MLP_TF),
        in_specs=[
            pl.BlockSpec((MLP_TM, D_MODEL), lambda i, j: (i, 0)),
            pl.BlockSpec((1, D_MODEL), lambda i, j: (0, 0)),
            pl.BlockSpec((D_MODEL, MLP_TF), lambda i, j: (0, j)),
            pl.BlockSpec((MLP_TF, D_MODEL), lambda i, j: (j, 0)),
            pl.BlockSpec((1, D_MODEL), lambda i, j: (0, 0)),
        ],
        out_specs=pl.BlockSpec((MLP_TM, D_MODEL), lambda i, j: (i, 0)),
        scratch_shapes=[pltpu.VMEM((MLP_TM, D_MODEL), BF16)],
        compiler_params=pltpu.CompilerParams(
            dimension_semantics=("arbitrary", "arbitrary"),
            vmem_limit_bytes=_vmem_limit(vmem)),
        name="mlp",
    )(x2d, norm_w, w_up, w_down, final_w)


def _rope_tables(seq):
    t = np.arange(seq)
    row = (t // GRID_W).astype(np.float64)
    col = (t % GRID_W).astype(np.float64)
    inv_freq = 1.0 / (ROPE_THETA ** (np.arange(0, AXIS_DIM, 2, dtype=np.float64) / AXIS_DIM))
    ang_r = row[:, None] * inv_freq[None, :]
    ang_c = col[:, None] * inv_freq[None, :]
    cos_t = np.concatenate([np.cos(ang_r), np.cos(ang_c), np.cos(ang_r), np.cos(ang_c)], axis=-1)
    sin_t = np.concatenate([-np.sin(ang_r), -np.sin(ang_c), np.sin(ang_r), np.sin(ang_c)], axis=-1)
    return jnp.asarray(cos_t, F32), jnp.asarray(sin_t, F32)


def _pair_heads(w):
    lead = w.shape[:-1]
    heads = w.shape[-1] // HEAD_DIM
    quarter = HEAD_DIM // 4
    w = w.reshape(lead + (heads, 2, 2, quarter))
    return jnp.swapaxes(w, -2, -3).reshape(lead + (heads * HEAD_DIM,))


def _w_in_prep_kernel(w_ref, o_ref):
    lane = lax.broadcasted_iota(jnp.int32, (WPREP_ROWS, HEAD_DIM), 1)
    quarter = HEAD_DIM // 4
    takes_c1 = (lane >= quarter) & (lane < 2 * quarter)
    takes_r2 = (lane >= 2 * quarter) & (lane < 3 * quarter)
    rope_slabs = (set(range(0, 2 * RET_HEADS))
                  | set(range(4 * RET_HEADS, 4 * RET_HEADS + ATTN_HEADS + ATTN_KV_HEADS)))
    for s in range(IN_WIDTH // HEAD_DIM):
        cols = slice(s * HEAD_DIM, (s + 1) * HEAD_DIM)
        y = w_ref[:, cols]
        if s in rope_slabs:
            y = jnp.where(takes_c1, pltpu.roll(y, HEAD_DIM - quarter, 1),
                          jnp.where(takes_r2, pltpu.roll(y, quarter, 1), y))
        o_ref[:, cols] = y.astype(BF16)


def _in_proj_weights(w_in):
    rows = w_in.shape[0]
    assert rows % WPREP_ROWS == 0
    vmem = 2 * WPREP_ROWS * IN_WIDTH * (4 + 2)
    return pl.pallas_call(
        _w_in_prep_kernel,
        out_shape=jax.ShapeDtypeStruct((rows, IN_WIDTH), BF16),
        grid=(rows // WPREP_ROWS,),
        in_specs=[pl.BlockSpec((WPREP_ROWS, IN_WIDTH), lambda i: (i, 0))],
        out_specs=pl.BlockSpec((WPREP_ROWS, IN_WIDTH), lambda i: (i, 0)),
        compiler_params=pltpu.CompilerParams(
            dimension_semantics=("arbitrary",),
            vmem_limit_bytes=_vmem_limit(vmem)),
        name="w_in_prep",
    )(w_in)


def kernel(x, mem, norm_mix_w, w_in, ret_decay_fwd, ret_decay_bwd, ret_gn_w, ret_gn_b, attn_q_norm_w, attn_k_norm_w, w_out, norm_cross_w, norm_mem_w, w_cross_q, w_cross_k, w_cross_v, w_cross_o, norm_mlp_w, w_mlp_up, w_mlp_down, norm_final_w):
    batch, seq, _ = x.shape
    assert w_in.shape[0] == 1, "single-layer block: per-layer parameters have a leading axis of 1"
    cos_t, sin_t = _rope_tables(seq)
    xs = x.reshape(batch * seq, D_MODEL)
    mem2d = mem.reshape(batch * MEM_TOKENS, D_MODEL)
    dec = jnp.stack([ret_decay_fwd[0], ret_decay_bwd[0]], axis=1)
    dec = jnp.broadcast_to(dec[:, :, None], (RET_HEADS, 2, RET_C)).astype(F32)
    proj = _in_proj(xs, norm_mix_w, _in_proj_weights(w_in[0]), cos_t, sin_t,
                    _pair_heads(attn_q_norm_w), _pair_heads(attn_k_norm_w), seq)
    y_ret = _retention(proj, dec, ret_gn_w, ret_gn_b, batch, seq)
    y_attn = _attention(proj, batch, seq)
    mem_kv = _mem_kv(mem2d, norm_mem_w, w_cross_k[0], w_cross_v[0])
    xs = _out_cross(xs, y_ret, y_attn, w_out[0].astype(BF16), norm_cross_w,
                    w_cross_q[0].astype(BF16), mem_kv, w_cross_o[0].astype(BF16), seq)
    xs = _mlp(xs, norm_mlp_w, w_mlp_up[0].astype(BF16), w_mlp_down[0].astype(BF16),
              norm_final_w[None, :])
    return xs.reshape(batch, seq, D_MODEL)
```

```python
import functools
import math

import numpy as np
import jax
import jax.numpy as jnp
from jax import lax
from jax.experimental import pallas as pl
from jax.experimental.pallas import tpu as pltpu

D_MODEL = 2048
HEAD_DIM = 128
RET_WIDTH = 1024
ATTN_WIDTH = 1024
RET_HEADS = 8
ATTN_HEADS = 8
ATTN_KV_HEADS = 2
GQA_GROUP = ATTN_HEADS // ATTN_KV_HEADS
KV_WIDTH = ATTN_KV_HEADS * HEAD_DIM
IN_WIDTH = 4 * RET_WIDTH + ATTN_WIDTH + 2 * KV_WIDTH
GRID_W = 64
AXIS_DIM = HEAD_DIM // 2
ROPE_THETA = 10000.0
MEM_TOKENS = 256
CROSS_HEADS = 4
CROSS_HEAD_DIM = 128
CROSS_WIDTH = CROSS_HEADS * CROSS_HEAD_DIM
D_FF = 4 * D_MODEL
NORM_EPS = 1e-6

V7X_LANES = 128
V7X_BF16_SUBLANES = 16
V7X_VMEM_BYTES = 64 * 1024 * 1024

F32 = jnp.float32
BF16 = jnp.bfloat16
LOG2E = math.log2(math.e)
NEG_BIG = -1e30
TINY = 1e-30

IN_TM = 512
IN_TN = 512
IN_NORM_RC = 256
WPREP_ROWS = 256
RET_C = 256
RET_TS = 2048
RET_HG = 4
ATT_TQ = 512
ATT_TK = 4096
ATT_DENOM_FLOOR = 2.0 ** -40
OC_TM = 512
OC_RC = 512
MLP_TM = 1024
MLP_TF = 512
MLP_RC = 1024


def _vmem_limit(nbytes):
    return int(min(nbytes + 16 * 1024 * 1024, V7X_VMEM_BYTES - 8 * 1024 * 1024))


def _rms_scale(y):
    return lax.rsqrt(jnp.mean(y * y, axis=-1, keepdims=True) + NORM_EPS)


def _slab_cast_specs(weights, steps, flat_step):
    specs, shapes, nbytes = [], [], 0
    for w in weights:
        assert w.shape[0] % steps == 0 and (w.shape[0] // steps) % V7X_BF16_SUBLANES == 0
        block = (w.shape[0] // steps, w.shape[1])
        specs.append(pl.BlockSpec(block, lambda *ids: (flat_step(*ids), 0)))
        shapes.append(jax.ShapeDtypeStruct(w.shape, BF16))
        nbytes += 2 * block[0] * block[1] * (4 + 2)
    return specs, shapes, nbytes


def _cast_slabs(src_refs, dst_refs):
    for src, dst in zip(src_refs, dst_refs):
        dst[...] = src[...].astype(BF16)


def _rope(y, cos, sin_signed):
    return y * cos + pltpu.roll(y, HEAD_DIM // 2, 1) * sin_signed


def _in_proj_kernel(x_ref, nw_ref, w_ref, cos_ref, sin_ref, qg_ref, kg_ref, o_ref, h_ref):
    for r in range(IN_TM // IN_NORM_RC):
        rows = slice(r * IN_NORM_RC, (r + 1) * IN_NORM_RC)
        xr = x_ref[rows, :]
        h_ref[rows, :] = (xr * _rms_scale(xr) * nw_ref[...]).astype(BF16)

    ret_k_scale = HEAD_DIM ** -0.5
    attn_q_scale = (HEAD_DIM ** -0.5) * LOG2E

    n_slabs = IN_TN // HEAD_DIM

    def run(j, epilogue):
        tile = slice(j * IN_TN, (j + 1) * IN_TN)
        acc = jnp.dot(h_ref[...], w_ref[:, tile], preferred_element_type=F32)
        slabs = epilogue([acc[:, s * HEAD_DIM:(s + 1) * HEAD_DIM] for s in range(n_slabs)],
                         cos_ref[...], sin_ref[...])
        for s, y in enumerate(slabs):
            c0 = j * IN_TN + s * HEAD_DIM
            o_ref[:, c0:c0 + HEAD_DIM] = y.astype(BF16)

    def ret_q(ys, cos, sin):
        return [_rope(y, cos, sin) for y in ys]

    def ret_k(ys, cos, sin):
        return [_rope(y, cos, sin) * ret_k_scale for y in ys]

    def plain(ys, cos, sin):
        return ys

    def head_inv_rms(y_a, y_b):
        width = 2 * HEAD_DIM
        r = lax.broadcasted_iota(jnp.int32, (width, width), 0) // HEAD_DIM
        c = lax.broadcasted_iota(jnp.int32, (width, width), 1) // HEAD_DIM
        ones_bd = jnp.where(r == c, 1.0, 0.0).astype(BF16)
        sq = jnp.concatenate([(y_a * y_a).astype(BF16), (y_b * y_b).astype(BF16)], axis=1)
        ms = jnp.dot(sq, ones_bd, preferred_element_type=F32) * (1.0 / HEAD_DIM)
        inv = lax.rsqrt(ms + NORM_EPS)
        return inv[:, :HEAD_DIM], inv[:, HEAD_DIM:]

    def normed_rope(y_a, y_b, gain, scale, cos, sin):
        inv_a, inv_b = head_inv_rms(y_a, y_b)
        return [_rope(y_a * gain, cos, sin) * (inv_a * scale),
                _rope(y_b * gain, cos, sin) * (inv_b * scale)]

    def attn_q(ys, cos, sin):
        gain = qg_ref[...]
        return (normed_rope(ys[0], ys[1], gain, attn_q_scale, cos, sin)
                + normed_rope(ys[2], ys[3], gain, attn_q_scale, cos, sin))

    def attn_kv(ys, cos, sin):
        return normed_rope(ys[0], ys[1], kg_ref[...], 1.0, cos, sin) + ys[ATTN_KV_HEADS:]

    nb = RET_WIDTH // IN_TN
    for j in range(IN_WIDTH // IN_TN):
        if j < nb:
            run(j, ret_q)
        elif j < 2 * nb:
            run(j, ret_k)
        elif j < 4 * nb:
            run(j, plain)
        elif j < 5 * nb:
            run(j, attn_q)
        else:
            run(j, attn_kv)


def _in_proj(x2d, norm_w, w_in, cos_t, sin_t, q_gain, k_gain, seq):
    n = x2d.shape[0]
    assert IN_WIDTH % IN_TN == 0 and 2 * KV_WIDTH == IN_TN and n % IN_TM == 0 and seq % IN_TM == 0
    tiles_per_seq = seq // IN_TM
    vmem = (2 * IN_TM * D_MODEL * 4 + IN_TM * D_MODEL * 2 + D_MODEL * IN_WIDTH * 2
            + 2 * IN_TM * IN_WIDTH * 2 + 4 * IN_TM * HEAD_DIM * 4)
    resident = pl.Buffered(1)
    return pl.pallas_call(
        _in_proj_kernel,
        out_shape=jax.ShapeDtypeStruct((n, IN_WIDTH), BF16),
        grid=(n // IN_TM,),
        in_specs=[
            pl.BlockSpec((IN_TM, D_MODEL), lambda i: (i, 0)),
            pl.BlockSpec((1, D_MODEL), lambda i: (0, 0)),
            pl.BlockSpec((D_MODEL, IN_WIDTH), lambda i: (0, 0), pipeline_mode=resident),
            pl.BlockSpec((IN_TM, HEAD_DIM), lambda i: (i % tiles_per_seq, 0)),
            pl.BlockSpec((IN_TM, HEAD_DIM), lambda i: (i % tiles_per_seq, 0)),
            pl.BlockSpec((1, HEAD_DIM), lambda i: (0, 0)),
            pl.BlockSpec((1, HEAD_DIM), lambda i: (0, 0)),
        ],
        out_specs=pl.BlockSpec((IN_TM, IN_WIDTH), lambda i: (i, 0)),
        scratch_shapes=[pltpu.VMEM((IN_TM, D_MODEL), BF16)],
        compiler_params=pltpu.CompilerParams(
            dimension_semantics=("arbitrary",),
            vmem_limit_bytes=_vmem_limit(vmem)),
        name="in_proj",
    )(x2d, norm_w, w_in, cos_t, sin_t, q_gain, k_gain)


def _retention_kernel(dec_ref, q_ref, k_ref, v_ref, g_ref, gw_ref, gb_ref,
                      wout_ref, wq_ref, wco_ref, o_ref, wout_o_ref, wq_o_ref, wco_o_ref,
                      d_ref, tab_ref, sb_ref, st_ref, *, n_seq):
    _cast_slabs((wout_ref, wq_ref, wco_ref), (wout_o_ref, wq_o_ref, wco_o_ref))
    s = pl.program_id(2)
    c_len = RET_C
    cps = RET_TS // RET_C
    contract_rows = (((0,), (0,)), ((), ()))
    contract_cols = (((1,), (1,)), ((), ()))
    head_cols = [slice(h * HEAD_DIM, (h + 1) * HEAD_DIM) for h in range(RET_HG)]
    chunk_rows = [slice(c * c_len, (c + 1) * c_len) for c in range(cps)]

    def log_gammas(h):
        return -jnp.exp(dec_ref[h, 0:1, :]), -jnp.exp(dec_ref[h, 1:2, :])

    @pl.when(s == 0)
    def _():
        row = lax.broadcasted_iota(jnp.int32, (c_len, c_len), 0)
        col = lax.broadcasted_iota(jnp.int32, (c_len, c_len), 1)
        rel = (row - col).astype(F32)
        idx = lax.broadcasted_iota(jnp.int32, (c_len, HEAD_DIM), 0).astype(F32)
        for h in range(RET_HG):
            lg_f, lg_b = log_gammas(h)
            d_ref[h] = jnp.exp(jnp.where(rel >= 0, lg_f * rel, -lg_b * rel))
            lf = lg_f[:, :HEAD_DIM]
            lb = lg_b[:, :HEAD_DIM]
            tab_ref[h, 0] = jnp.exp(lf * (idx + 1.0))
            tab_ref[h, 1] = jnp.exp(lb * (c_len - idx))
            tab_ref[h, 2] = jnp.exp(lf * (c_len - 1.0 - idx))
            tab_ref[h, 3] = jnp.exp(lb * idx)
        st_ref[...] = jnp.zeros(st_ref.shape, F32)

    @pl.when(s < n_seq)
    def _():
        first_chunk = (n_seq - 1 - s) * cps
        for h in range(RET_HG):
            chunk_b = jnp.exp(log_gammas(h)[1][:, :HEAD_DIM] * c_len)
            state = st_ref[h]
            for c in reversed(range(cps)):
                sb_ref[first_chunk + c, h] = state.astype(BF16)
                k = k_ref[chunk_rows[c], head_cols[h]]
                kd = (k.astype(F32) * tab_ref[h, 3]).astype(BF16)
                upd = lax.dot_general(kd, v_ref[chunk_rows[c], head_cols[h]], contract_rows,
                                      preferred_element_type=F32)
                state = state * chunk_b + upd
            st_ref[h] = state

        @pl.when(s == n_seq - 1)
        def _():
            st_ref[...] = jnp.zeros(st_ref.shape, F32)

    @pl.when(s >= n_seq)
    def _():
        first_chunk = (s - n_seq) * cps
        for h in range(RET_HG):
            chunk_f = jnp.exp(log_gammas(h)[0][:, :HEAD_DIM] * c_len)
            gw = gw_ref[:, head_cols[h]]
            gb = gb_ref[:, head_cols[h]]
            state = st_ref[h]
            for c in range(cps):
                q = q_ref[chunk_rows[c], head_cols[h]]
                k = k_ref[chunk_rows[c], head_cols[h]]
                v = v_ref[chunk_rows[c], head_cols[h]]
                scores = lax.dot_general(q, k, contract_cols, preferred_element_type=F32)
                masked = (scores * d_ref[h]).astype(BF16)
                qf = q.astype(F32)
                lhs = jnp.concatenate(
                    [masked, (qf * tab_ref[h, 0]).astype(BF16), (qf * tab_ref[h, 1]).astype(BF16)],
                    axis=1)
                rhs = jnp.concatenate([v, state.astype(BF16), sb_ref[first_chunk + c, h]], axis=0)
                out = jnp.dot(lhs, rhs, preferred_element_type=F32)
                mu = jnp.mean(out, axis=-1, keepdims=True)
                cen = out - mu
                var = jnp.mean(cen * cen, axis=-1, keepdims=True)
                y = cen * lax.rsqrt(var + NORM_EPS) * gw + gb
                g = g_ref[chunk_rows[c], head_cols[h]].astype(F32)
                y = y * (g / (1.0 + jnp.exp(-g)))
                o_ref[chunk_rows[c], head_cols[h]] = y.astype(BF16)
                kd = (k.astype(F32) * tab_ref[h, 2]).astype(BF16)
                upd = lax.dot_general(kd, v, contract_rows, preferred_element_type=F32)
                state = state * chunk_f + upd
            st_ref[h] = state


def _retention(proj, dec, gn_w, gn_b, cross_weights, batch, seq):
    n = proj.shape[0]
    assert seq % RET_TS == 0 and RET_TS % RET_C == 0 and RET_HEADS % RET_HG == 0
    n_seq = seq // RET_TS
    n_chunks = seq // RET_C
    group_w = RET_HG * HEAD_DIM
    groups = RET_HEADS // RET_HG

    def kv_block(section):
        def index(b, hg, s):
            blk = jnp.where(s < n_seq, n_seq - 1 - s, s - n_seq)
            return (b * n_seq + blk, section * groups + hg)
        return index

    def fwd_block(section):
        return lambda b, hg, s: (b * n_seq + jnp.maximum(s - n_seq, 0), section * groups + hg)

    cast_specs, cast_shapes, cast_bytes = _slab_cast_specs(
        cross_weights, batch * groups * 2 * n_seq,
        lambda b, hg, s: (b * groups + hg) * 2 * n_seq + s)
    vmem = (2 * 5 * RET_TS * group_w * 2 + RET_HG * RET_C * RET_C * 4
            + RET_HG * 4 * RET_C * HEAD_DIM * 4 + n_chunks * RET_HG * HEAD_DIM * HEAD_DIM * 2
            + RET_HG * HEAD_DIM * HEAD_DIM * 4 + cast_bytes)
    y_ret, *cast = pl.pallas_call(
        functools.partial(_retention_kernel, n_seq=n_seq),
        out_shape=(jax.ShapeDtypeStruct((n, RET_WIDTH), BF16), *cast_shapes),
        grid=(batch, groups, 2 * n_seq),
        in_specs=[
            pl.BlockSpec((RET_HG, 2, RET_C), lambda b, hg, s: (hg, 0, 0)),
            pl.BlockSpec((RET_TS, group_w), fwd_block(0)),
            pl.BlockSpec((RET_TS, group_w), kv_block(1)),
            pl.BlockSpec((RET_TS, group_w), kv_block(2)),
            pl.BlockSpec((RET_TS, group_w), fwd_block(3)),
            pl.BlockSpec((1, group_w), lambda b, hg, s: (0, hg)),
            pl.BlockSpec((1, group_w), lambda b, hg, s: (0, hg)),
            *cast_specs,
        ],
        out_specs=(pl.BlockSpec((RET_TS, group_w), fwd_block(0)), *cast_specs),
        scratch_shapes=[
            pltpu.VMEM((RET_HG, RET_C, RET_C), F32),
            pltpu.VMEM((RET_HG, 4, RET_C, HEAD_DIM), F32),
            pltpu.VMEM((n_chunks, RET_HG, HEAD_DIM, HEAD_DIM), BF16),
            pltpu.VMEM((RET_HG, HEAD_DIM, HEAD_DIM), F32),
        ],
        compiler_params=pltpu.CompilerParams(
            dimension_semantics=("arbitrary", "arbitrary", "arbitrary"),
            vmem_limit_bytes=_vmem_limit(vmem)),
        name="retention",
    )(dec, proj, proj, proj, proj, gn_w, gn_b, *cross_weights)
    return y_ret, cast


def _attention_kernel(q_ref, k_ref, v_ref, wu_ref, wd_ref, o_ref, wu_o_ref, wd_o_ref,
                      vt_ref, k2max_ref, qa_ref, acc_ref, den_ref, *, n_kv):
    _cast_slabs((wu_ref, wd_ref), (wu_o_ref, wd_o_ref))

    i = pl.program_id(2)
    contract_cols = (((1,), (1,)), ((), ()))

    def kv_rows(c):
        return pl.ds(pl.multiple_of(c * ATT_TK, ATT_TK), ATT_TK)

    @pl.when(i == 0)
    def _():
        def body(c, k2max):
            rows = kv_rows(c)
            vt_ref[c] = v_ref[rows, :].astype(F32).T.astype(BF16)
            kc = k_ref[rows, :].astype(F32)
            k2 = jnp.max(jnp.sum(kc * kc, axis=1, keepdims=True), axis=0, keepdims=True)
            return jnp.maximum(k2max, k2)
        k2max = lax.fori_loop(0, n_kv, body, jnp.zeros((1, 1), F32))
        k2max_ref[...] = jnp.broadcast_to(k2max, k2max_ref.shape)

    q_lane = lax.broadcasted_iota(jnp.int32, (ATT_TQ, HEAD_DIM), 1)
    k_lane = lax.broadcasted_iota(jnp.int32, (ATT_TK, HEAD_DIM), 1)
    k_one = jnp.where(k_lane == 0, 1.0, 0.0).astype(BF16)
    head_cols = [slice(h * HEAD_DIM, (h + 1) * HEAD_DIM) for h in range(GQA_GROUP)]

    def set_shift(h, shift):
        qa_ref[h, :, HEAD_DIM:] = jnp.where(q_lane == 0, -shift, 0.0).astype(BF16)

    def sweep():
        acc_ref[...] = jnp.zeros(acc_ref.shape, F32)
        den_ref[...] = jnp.zeros(den_ref.shape, F32)

        def kv_step(c, carry):
            k_aug = jnp.concatenate([k_ref[kv_rows(c), :], k_one], axis=1)
            vt = vt_ref[c]
            for h in range(GQA_GROUP):
                s_t = lax.dot_general(k_aug, qa_ref[h], contract_cols,
                                      preferred_element_type=F32)
                p_t = jnp.exp2(s_t)
                den_ref[h] += jnp.sum(p_t.reshape(ATT_TK // 8, 8, ATT_TQ), axis=0)
                acc_ref[h] += jnp.dot(vt, p_t.astype(BF16), preferred_element_type=F32)
            return carry
        lax.fori_loop(0, n_kv, kv_step, 0)

    ones = jnp.ones((HEAD_DIM, HEAD_DIM), BF16)
    for h in range(GQA_GROUP):
        qh = q_ref[:, head_cols[h]]
        qa_ref[h, :, 0:HEAD_DIM] = qh
        qf = qh.astype(F32)
        q2 = jnp.dot((qf * qf).astype(BF16), ones, preferred_element_type=F32)
        bound2 = q2 * k2max_ref[0:1, :]
        set_shift(h, bound2 * lax.rsqrt(bound2 + TINY))
    sweep()

    def denominator(h):
        return jnp.sum(den_ref[h], axis=0, keepdims=True)

    denom_min = functools.reduce(jnp.minimum,
                                 [jnp.min(denominator(h)) for h in range(GQA_GROUP)])

    @pl.when(denom_min < ATT_DENOM_FLOOR)
    def _():
        for h in range(GQA_GROUP):
            def body(c, m):
                s = lax.dot_general(q_ref[:, head_cols[h]], k_ref[kv_rows(c), :], contract_cols,
                                    preferred_element_type=F32)
                return jnp.maximum(m, jnp.max(s, axis=1, keepdims=True))
            set_shift(h, lax.fori_loop(0, n_kv, body, jnp.full((ATT_TQ, 1), NEG_BIG, F32)))
        sweep()

    for h in range(GQA_GROUP):
        o_t = acc_ref[h] * (1.0 / denominator(h))
        o_ref[:, head_cols[h]] = o_t.T.astype(BF16)


def _attention(proj, w_up, w_down, batch, seq):
    n = proj.shape[0]
    assert seq % ATT_TQ == 0 and seq % ATT_TK == 0
    n_q = seq // ATT_TQ
    n_kv = seq // ATT_TK
    group_w = GQA_GROUP * HEAD_DIM
    q_col0 = 4 * RET_WIDTH // group_w
    k_col0 = (4 * RET_WIDTH + ATTN_WIDTH) // HEAD_DIM
    v_col0 = k_col0 + ATTN_KV_HEADS
    cast_specs, cast_shapes, cast_bytes = _slab_cast_specs(
        (w_up, w_down), batch * ATTN_KV_HEADS * n_q,
        lambda b, g, i: (b * ATTN_KV_HEADS + g) * n_q + i)
    vmem = (2 * ATT_TQ * group_w * 2 * 2 + 2 * 2 * seq * HEAD_DIM * 2
            + HEAD_DIM * seq * 2 + GQA_GROUP * (HEAD_DIM + 8) * ATT_TQ * 4
            + GQA_GROUP * ATT_TQ * 2 * HEAD_DIM * 2 + 2 * GQA_GROUP * ATT_TK * ATT_TQ * 4
            + cast_bytes)
    return pl.pallas_call(
        functools.partial(_attention_kernel, n_kv=n_kv),
        out_shape=(jax.ShapeDtypeStruct((n, ATTN_WIDTH), BF16), *cast_shapes),
        grid=(batch, ATTN_KV_HEADS, n_q),
        in_specs=[
            pl.BlockSpec((ATT_TQ, group_w), lambda b, g, i: (b * n_q + i, q_col0 + g)),
            pl.BlockSpec((seq, HEAD_DIM), lambda b, g, i: (b, k_col0 + g)),
            pl.BlockSpec((seq, HEAD_DIM), lambda b, g, i: (b, v_col0 + g)),
            *cast_specs,
        ],
        out_specs=(pl.BlockSpec((ATT_TQ, group_w), lambda b, g, i: (b * n_q + i, g)),
                   *cast_specs),
        scratch_shapes=[
            pltpu.VMEM((n_kv, HEAD_DIM, ATT_TK), BF16),
            pltpu.VMEM((8, HEAD_DIM), F32),
            pltpu.VMEM((GQA_GROUP, ATT_TQ, 2 * HEAD_DIM), BF16),
            pltpu.VMEM((GQA_GROUP, HEAD_DIM, ATT_TQ), F32),
            pltpu.VMEM((GQA_GROUP, 8, ATT_TQ), F32),
        ],
        compiler_params=pltpu.CompilerParams(
            dimension_semantics=("arbitrary", "arbitrary", "arbitrary"),
            vmem_limit_bytes=_vmem_limit(vmem)),
        name="attention",
    )(proj, proj, proj, w_up, w_down)


def _mem_kv_kernel(m_ref, nw_ref, wk_ref, wv_ref, o_ref):
    m = m_ref[...]
    h = (m * _rms_scale(m) * nw_ref[...]).astype(BF16)
    o_ref[:, :CROSS_WIDTH] = jnp.dot(h, wk_ref[...].astype(BF16),
                                     preferred_element_type=F32).astype(BF16)
    o_ref[:, CROSS_WIDTH:] = jnp.dot(h, wv_ref[...].astype(BF16),
                                     preferred_element_type=F32).astype(BF16)


def _mem_kv(mem2d, norm_w, w_k, w_v):
    rows = mem2d.shape[0]
    vmem = 2 * (rows * D_MODEL * 4 + 2 * D_MODEL * CROSS_WIDTH * 4 + rows * 2 * CROSS_WIDTH * 2)
    return pl.pallas_call(
        _mem_kv_kernel,
        out_shape=jax.ShapeDtypeStruct((rows, 2 * CROSS_WIDTH), BF16),
        grid=(1,),
        in_specs=[
            pl.BlockSpec((rows, D_MODEL), lambda i: (0, 0)),
            pl.BlockSpec((1, D_MODEL), lambda i: (0, 0)),
            pl.BlockSpec((D_MODEL, CROSS_WIDTH), lambda i: (0, 0)),
            pl.BlockSpec((D_MODEL, CROSS_WIDTH), lambda i: (0, 0)),
        ],
        out_specs=pl.BlockSpec((rows, 2 * CROSS_WIDTH), lambda i: (0, 0)),
        compiler_params=pltpu.CompilerParams(
            dimension_semantics=("arbitrary",),
            vmem_limit_bytes=_vmem_limit(vmem)),
        name="mem_kv",
    )(mem2d, norm_w, w_k, w_v)


def _out_cross_kernel(x_ref, yr_ref, ya_ref, wo_ref, nw_ref, wq_ref, kv_ref, wco_ref, o_ref):
    contract_cols = (((1,), (1,)), ((), ()))
    scale = CROSS_HEAD_DIM ** -0.5

    for r in range(OC_TM // OC_RC):
        rows = slice(r * OC_RC, (r + 1) * OC_RC)
        x1 = (x_ref[rows, :]
              + jnp.dot(yr_ref[rows, :], wo_ref[0:RET_WIDTH, :], preferred_element_type=F32)
              + jnp.dot(ya_ref[rows, :], wo_ref[RET_WIDTH:, :], preferred_element_type=F32))
        h = (x1 * _rms_scale(x1) * nw_ref[...]).astype(BF16)
        q = (jnp.dot(h, wq_ref[...], preferred_element_type=F32) * scale).astype(BF16)
        heads = []
        for hd in range(CROSS_HEADS):
            cols = slice(hd * CROSS_HEAD_DIM, (hd + 1) * CROSS_HEAD_DIM)
            k = kv_ref[:, cols]
            v = kv_ref[:, CROSS_WIDTH + hd * CROSS_HEAD_DIM:CROSS_WIDTH + (hd + 1) * CROSS_HEAD_DIM]
            s = lax.dot_general(q[:, cols], k, contract_cols, preferred_element_type=F32)
            e = jnp.exp(s - jnp.max(s, axis=-1, keepdims=True))
            inv = 1.0 / jnp.sum(e, axis=-1, keepdims=True)
            pv = jnp.dot(e.astype(BF16), v, preferred_element_type=F32)
            heads.append((pv * inv).astype(BF16))
        o = jnp.concatenate(heads, axis=1)
        o_ref[rows, :] = x1 + jnp.dot(o, wco_ref[...], preferred_element_type=F32)


def _out_cross(x2d, y_ret, y_attn, w_out, norm_w, w_q, mem_kv, w_co, seq):
    n = x2d.shape[0]
    assert n % OC_TM == 0 and seq % OC_TM == 0
    tiles_per_seq = seq // OC_TM
    vmem = (2 * 2 * OC_TM * D_MODEL * 4 + 2 * 2 * OC_TM * RET_WIDTH * 2 + 2 * D_MODEL * D_MODEL * 2
            + 2 * 2 * D_MODEL * CROSS_WIDTH * 2 + 2 * MEM_TOKENS * 2 * CROSS_WIDTH * 2)
    return pl.pallas_call(
        _out_cross_kernel,
        out_shape=jax.ShapeDtypeStruct((n, D_MODEL), F32),
        grid=(n // OC_TM,),
        in_specs=[
            pl.BlockSpec((OC_TM, D_MODEL), lambda i: (i, 0)),
            pl.BlockSpec((OC_TM, RET_WIDTH), lambda i: (i, 0)),
            pl.BlockSpec((OC_TM, ATTN_WIDTH), lambda i: (i, 0)),
            pl.BlockSpec((D_MODEL, D_MODEL), lambda i: (0, 0)),
            pl.BlockSpec((1, D_MODEL), lambda i: (0, 0)),
            pl.BlockSpec((D_MODEL, CROSS_WIDTH), lambda i: (0, 0)),
            pl.BlockSpec((MEM_TOKENS, 2 * CROSS_WIDTH), lambda i: (i // tiles_per_seq, 0)),
            pl.BlockSpec((CROSS_WIDTH, D_MODEL), lambda i: (0, 0)),
        ],
        out_specs=pl.BlockSpec((OC_TM, D_MODEL), lambda i: (i, 0)),
        compiler_params=pltpu.CompilerParams(
            dimension_semantics=("arbitrary",),
            vmem_limit_bytes=_vmem_limit(vmem)),
        name="out_cross",
    )(x2d, y_ret, y_attn, w_out, norm_w, w_q, mem_kv, w_co)


def _mlp_kernel(x_ref, nw_ref, wu_ref, wd_ref, fw_ref, o_ref, h_ref):
    j = pl.program_id(1)

    def row_chunk(r):
        return pl.ds(pl.multiple_of(r * MLP_RC, MLP_RC), MLP_RC)

    @pl.when(j == 0)
    def _():
        def body(r, c):
            rows = row_chunk(r)
            xr = x_ref[rows, :]
            h_ref[rows, :] = (xr * _rms_scale(xr) * nw_ref[...]).astype(BF16)
            o_ref[rows, :] = xr
            return c
        lax.fori_loop(0, MLP_TM // MLP_RC, body, 0)

    def body(r, c):
        rows = row_chunk(r)
        u = jnp.dot(h_ref[rows, :], wu_ref[...], preferred_element_type=F32)
        u = jnp.maximum(u, 0.0)
        a = (u * u).astype(BF16)
        o_ref[rows, :] += jnp.dot(a, wd_ref[...], preferred_element_type=F32)
        return c
    lax.fori_loop(0, MLP_TM // MLP_RC, body, 0)

    @pl.when(j == pl.num_programs(1) - 1)
    def _():
        def body(r, c):
            rows = row_chunk(r)
            y = o_ref[rows, :]
            o_ref[rows, :] = y * _rms_scale(y) * fw_ref[...]
            return c
        lax.fori_loop(0, MLP_TM // MLP_RC, body, 0)


def _mlp(x2d, norm_w, w_up, w_down, final_w):
    n = x2d.shape[0]
    assert n % MLP_TM == 0 and D_FF % MLP_TF == 0
    vmem = (2 * 2 * MLP_TM * D_MODEL * 4 + MLP_TM * D_MODEL * 2 + 2 * 2 * D_MODEL * MLP_TF * 2)
    return pl.pallas_call(
        _mlp_kernel,
        out_shape=jax.ShapeDtypeStruct((n, D_MODEL), F32),
        grid=(n // MLP_TM, D_FF // MLP_TF),
        in_specs=[
            pl.BlockSpec((MLP_TM, D_MODEL), lambda i, j: (i, 0)),
            pl.BlockSpec((1, D_MODEL), lambda i, j: (0, 0)),
            pl.BlockSpec((D_MODEL, MLP_TF), lambda i, j: (0, j)),
            pl.BlockSpec((MLP_TF, D_MODEL), lambda i, j: (j, 0)),
            pl.BlockSpec((1, D_MODEL), lambda i, j: (0, 0)),
        ],
        out_specs=pl.BlockSpec((MLP_TM, D_MODEL), lambda i, j: (i, 0)),
        scratch_shapes=[pltpu.VMEM((MLP_TM, D_MODEL), BF16)],
        compiler_params=pltpu.CompilerParams(
            dimension_semantics=("arbitrary", "arbitrary"),
            vmem_limit_bytes=_vmem_limit(vmem)),
        name="mlp",
    )(x2d, norm_w, w_up, w_down, final_w)


def _rope_tables(seq):
    t = np.arange(seq)
    row = (t // GRID_W).astype(np.float64)
    col = (t % GRID_W).astype(np.float64)
    inv_freq = 1.0 / (ROPE_THETA ** (np.arange(0, AXIS_DIM, 2, dtype=np.float64) / AXIS_DIM))
    ang_r = row[:, None] * inv_freq[None, :]
    ang_c = col[:, None] * inv_freq[None, :]
    cos_t = np.concatenate([np.cos(ang_r), np.cos(ang_c), np.cos(ang_r), np.cos(ang_c)], axis=-1)
    sin_t = np.concatenate([-np.sin(ang_r), -np.sin(ang_c), np.sin(ang_r), np.sin(ang_c)], axis=-1)
    return jnp.asarray(cos_t, F32), jnp.asarray(sin_t, F32)


def _pair_heads(w):
    lead = w.shape[:-1]
    heads = w.shape[-1] // HEAD_DIM
    quarter = HEAD_DIM // 4
    w = w.reshape(lead + (heads, 2, 2, quarter))
    return jnp.swapaxes(w, -2, -3).reshape(lead + (heads * HEAD_DIM,))


def _w_in_prep_kernel(w_ref, o_ref):
    lane = lax.broadcasted_iota(jnp.int32, (WPREP_ROWS, HEAD_DIM), 1)
    quarter = HEAD_DIM // 4
    takes_c1 = (lane >= quarter) & (lane < 2 * quarter)
    takes_r2 = (lane >= 2 * quarter) & (lane < 3 * quarter)
    rope_slabs = (set(range(0, 2 * RET_HEADS))
                  | set(range(4 * RET_HEADS, 4 * RET_HEADS + ATTN_HEADS + ATTN_KV_HEADS)))
    for s in range(IN_WIDTH // HEAD_DIM):
        cols = slice(s * HEAD_DIM, (s + 1) * HEAD_DIM)
        y = w_ref[:, cols]
        if s in rope_slabs:
            y = jnp.where(takes_c1, pltpu.roll(y, HEAD_DIM - quarter, 1),
                          jnp.where(takes_r2, pltpu.roll(y, quarter, 1), y))
        o_ref[:, cols] = y.astype(BF16)


def _in_proj_weights(w_in):
    rows = w_in.shape[0]
    assert rows % WPREP_ROWS == 0
    vmem = 2 * WPREP_ROWS * IN_WIDTH * (4 + 2)
    return pl.pallas_call(
        _w_in_prep_kernel,
        out_shape=jax.ShapeDtypeStruct((rows, IN_WIDTH), BF16),
        grid=(rows // WPREP_ROWS,),
        in_specs=[pl.BlockSpec((WPREP_ROWS, IN_WIDTH), lambda i: (i, 0))],
        out_specs=pl.BlockSpec((WPREP_ROWS, IN_WIDTH), lambda i: (i, 0)),
        compiler_params=pltpu.CompilerParams(
            dimension_semantics=("arbitrary",),
            vmem_limit_bytes=_vmem_limit(vmem)),
        name="w_in_prep",
    )(w_in)


def kernel(x, mem, norm_mix_w, w_in, ret_decay_fwd, ret_decay_bwd, ret_gn_w, ret_gn_b, attn_q_norm_w, attn_k_norm_w, w_out, norm_cross_w, norm_mem_w, w_cross_q, w_cross_k, w_cross_v, w_cross_o, norm_mlp_w, w_mlp_up, w_mlp_down, norm_final_w):
    batch, seq, _ = x.shape
    assert w_in.shape[0] == 1, "single-layer block: per-layer parameters have a leading axis of 1"
    cos_t, sin_t = _rope_tables(seq)
    xs = x.reshape(batch * seq, D_MODEL)
    mem2d = mem.reshape(batch * MEM_TOKENS, D_MODEL)
    dec = jnp.stack([ret_decay_fwd[0], ret_decay_bwd[0]], axis=1)
    dec = jnp.broadcast_to(dec[:, :, None], (RET_HEADS, 2, RET_C)).astype(F32)
    proj = _in_proj(xs, norm_mix_w, _in_proj_weights(w_in[0]), cos_t, sin_t,
                    _pair_heads(attn_q_norm_w), _pair_heads(attn_k_norm_w), seq)
    y_ret, (w_out_bf16, w_cq_bf16, w_co_bf16) = _retention(
        proj, dec, ret_gn_w, ret_gn_b, (w_out[0], w_cross_q[0], w_cross_o[0]), batch, seq)
    y_attn, w_up_bf16, w_down_bf16 = _attention(proj, w_mlp_up[0], w_mlp_down[0], batch, seq)
    mem_kv = _mem_kv(mem2d, norm_mem_w, w_cross_k[0], w_cross_v[0])
    xs = _out_cross(xs, y_ret, y_attn, w_out_bf16, norm_cross_w, w_cq_bf16, mem_kv, w_co_bf16, seq)
    xs = _mlp(xs, norm_mlp_w, w_up_bf16, w_down_bf16, norm_final_w[None, :])
    return xs.reshape(batch, seq, D_MODEL)
```

```python
import functools
import math

import numpy as np
import jax
import jax.numpy as jnp
from jax import lax
from jax.experimental import pallas as pl
from jax.experimental.pallas import tpu as pltpu

D_MODEL = 2048
HEAD_DIM = 128
RET_WIDTH = 1024
ATTN_WIDTH = 1024
RET_HEADS = 8
ATTN_HEADS = 8
ATTN_KV_HEADS = 2
GQA_GROUP = ATTN_HEADS // ATTN_KV_HEADS
KV_WIDTH = ATTN_KV_HEADS * HEAD_DIM
IN_WIDTH = 4 * RET_WIDTH + ATTN_WIDTH + 2 * KV_WIDTH
GRID_W = 64
AXIS_DIM = HEAD_DIM // 2
ROPE_THETA = 10000.0
MEM_TOKENS = 256
CROSS_HEADS = 4
CROSS_HEAD_DIM = 128
CROSS_WIDTH = CROSS_HEADS * CROSS_HEAD_DIM
D_FF = 4 * D_MODEL
NORM_EPS = 1e-6

V7X_LANES = 128
V7X_BF16_SUBLANES = 16
V7X_VMEM_BYTES = 64 * 1024 * 1024

F32 = jnp.float32
BF16 = jnp.bfloat16
LOG2E = math.log2(math.e)
NEG_BIG = -1e30
TINY = 1e-30

IN_TM = 512
IN_TN = 512
IN_NORM_RC = 256
WPREP_ROWS = 256
RET_C = 256
RET_TS = 2048
RET_HG = 4
ATT_TQ = 512
ATT_TK = 2048
ATT_DENOM_FLOOR = 2.0 ** -40
OC_TM = 512
OC_RC = 512
MLP_TM = 1024
MLP_TF = 512
MLP_EDGE_RC = 512


def _vmem_limit(nbytes):
    return int(min(nbytes + 16 * 1024 * 1024, V7X_VMEM_BYTES - 8 * 1024 * 1024))


def _rms_scale(y):
    return lax.rsqrt(jnp.mean(y * y, axis=-1, keepdims=True) + NORM_EPS)


def _slab_cast_specs(weights, steps, flat_step):
    specs, shapes, nbytes = [], [], 0
    for w in weights:
        assert w.shape[0] % steps == 0 and (w.shape[0] // steps) % V7X_BF16_SUBLANES == 0
        block = (w.shape[0] // steps, w.shape[1])
        specs.append(pl.BlockSpec(block, lambda *ids: (flat_step(*ids), 0)))
        shapes.append(jax.ShapeDtypeStruct(w.shape, BF16))
        nbytes += 2 * block[0] * block[1] * (4 + 2)
    return specs, shapes, nbytes


def _cast_slabs(src_refs, dst_refs):
    for src, dst in zip(src_refs, dst_refs):
        dst[...] = src[...].astype(BF16)


def _rope(y, cos, sin_signed):
    return y * cos + pltpu.roll(y, HEAD_DIM // 2, 1) * sin_signed


def _in_proj_kernel(x_ref, nw_ref, w_ref, cos_ref, sin_ref, qg_ref, kg_ref, o_ref, h_ref):
    for r in range(IN_TM // IN_NORM_RC):
        rows = slice(r * IN_NORM_RC, (r + 1) * IN_NORM_RC)
        xr = x_ref[rows, :]
        h_ref[rows, :] = (xr * _rms_scale(xr) * nw_ref[...]).astype(BF16)

    ret_k_scale = HEAD_DIM ** -0.5
    attn_q_scale = (HEAD_DIM ** -0.5) * LOG2E

    n_slabs = IN_TN // HEAD_DIM

    def run(j, epilogue):
        tile = slice(j * IN_TN, (j + 1) * IN_TN)
        acc = jnp.dot(h_ref[...], w_ref[:, tile], preferred_element_type=F32)
        slabs = epilogue([acc[:, s * HEAD_DIM:(s + 1) * HEAD_DIM] for s in range(n_slabs)],
                         cos_ref[...], sin_ref[...])
        for s, y in enumerate(slabs):
            c0 = j * IN_TN + s * HEAD_DIM
            o_ref[:, c0:c0 + HEAD_DIM] = y.astype(BF16)

    def ret_q(ys, cos, sin):
        return [_rope(y, cos, sin) for y in ys]

    def ret_k(ys, cos, sin):
        return [_rope(y, cos, sin) * ret_k_scale for y in ys]

    def plain(ys, cos, sin):
        return ys

    def head_inv_rms(y_a, y_b):
        width = 2 * HEAD_DIM
        r = lax.broadcasted_iota(jnp.int32, (width, width), 0) // HEAD_DIM
        c = lax.broadcasted_iota(jnp.int32, (width, width), 1) // HEAD_DIM
        ones_bd = jnp.where(r == c, 1.0, 0.0).astype(BF16)
        sq = jnp.concatenate([(y_a * y_a).astype(BF16), (y_b * y_b).astype(BF16)], axis=1)
        ms = jnp.dot(sq, ones_bd, preferred_element_type=F32) * (1.0 / HEAD_DIM)
        inv = lax.rsqrt(ms + NORM_EPS)
        return inv[:, :HEAD_DIM], inv[:, HEAD_DIM:]

    def normed_rope(y_a, y_b, gain, scale, cos, sin):
        inv_a, inv_b = head_inv_rms(y_a, y_b)
        return [_rope(y_a * gain, cos, sin) * (inv_a * scale),
                _rope(y_b * gain, cos, sin) * (inv_b * scale)]

    def attn_q(ys, cos, sin):
        gain = qg_ref[...]
        return (normed_rope(ys[0], ys[1], gain, attn_q_scale, cos, sin)
                + normed_rope(ys[2], ys[3], gain, attn_q_scale, cos, sin))

    def attn_kv(ys, cos, sin):
        return normed_rope(ys[0], ys[1], kg_ref[...], 1.0, cos, sin) + ys[ATTN_KV_HEADS:]

    nb = RET_WIDTH // IN_TN
    for j in range(IN_WIDTH // IN_TN):
        if j < nb:
            run(j, ret_q)
        elif j < 2 * nb:
            run(j, ret_k)
        elif j < 4 * nb:
            run(j, plain)
        elif j < 5 * nb:
            run(j, attn_q)
        else:
            run(j, attn_kv)


def _in_proj(x2d, norm_w, w_in, cos_t, sin_t, q_gain, k_gain, seq):
    n = x2d.shape[0]
    assert IN_WIDTH % IN_TN == 0 and 2 * KV_WIDTH == IN_TN and n % IN_TM == 0 and seq % IN_TM == 0
    tiles_per_seq = seq // IN_TM
    vmem = (2 * IN_TM * D_MODEL * 4 + IN_TM * D_MODEL * 2 + D_MODEL * IN_WIDTH * 2
            + 2 * IN_TM * IN_WIDTH * 2 + 4 * IN_TM * HEAD_DIM * 4)
    resident = pl.Buffered(1)
    return pl.pallas_call(
        _in_proj_kernel,
        out_shape=jax.ShapeDtypeStruct((n, IN_WIDTH), BF16),
        grid=(n // IN_TM,),
        in_specs=[
            pl.BlockSpec((IN_TM, D_MODEL), lambda i: (i, 0)),
            pl.BlockSpec((1, D_MODEL), lambda i: (0, 0)),
            pl.BlockSpec((D_MODEL, IN_WIDTH), lambda i: (0, 0), pipeline_mode=resident),
            pl.BlockSpec((IN_TM, HEAD_DIM), lambda i: (i % tiles_per_seq, 0)),
            pl.BlockSpec((IN_TM, HEAD_DIM), lambda i: (i % tiles_per_seq, 0)),
            pl.BlockSpec((1, HEAD_DIM), lambda i: (0, 0)),
            pl.BlockSpec((1, HEAD_DIM), lambda i: (0, 0)),
        ],
        out_specs=pl.BlockSpec((IN_TM, IN_WIDTH), lambda i: (i, 0)),
        scratch_shapes=[pltpu.VMEM((IN_TM, D_MODEL), BF16)],
        compiler_params=pltpu.CompilerParams(
            dimension_semantics=("arbitrary",),
            vmem_limit_bytes=_vmem_limit(vmem)),
        name="in_proj",
    )(x2d, norm_w, w_in, cos_t, sin_t, q_gain, k_gain)


def _retention_kernel(dec_ref, q_ref, k_ref, v_ref, g_ref, gw_ref, gb_ref,
                      wout_ref, wq_ref, wco_ref, o_ref, wout_o_ref, wq_o_ref, wco_o_ref,
                      d_ref, tab_ref, sb_ref, st_ref, *, n_seq):
    _cast_slabs((wout_ref, wq_ref, wco_ref), (wout_o_ref, wq_o_ref, wco_o_ref))
    s = pl.program_id(2)
    c_len = RET_C
    cps = RET_TS // RET_C
    contract_rows = (((0,), (0,)), ((), ()))
    contract_cols = (((1,), (1,)), ((), ()))
    head_cols = [slice(h * HEAD_DIM, (h + 1) * HEAD_DIM) for h in range(RET_HG)]
    chunk_rows = [slice(c * c_len, (c + 1) * c_len) for c in range(cps)]

    def log_gammas(h):
        return -jnp.exp(dec_ref[h, 0:1, :]), -jnp.exp(dec_ref[h, 1:2, :])

    @pl.when(s == 0)
    def _():
        row = lax.broadcasted_iota(jnp.int32, (c_len, c_len), 0)
        col = lax.broadcasted_iota(jnp.int32, (c_len, c_len), 1)
        rel = (row - col).astype(F32)
        idx = lax.broadcasted_iota(jnp.int32, (c_len, HEAD_DIM), 0).astype(F32)
        for h in range(RET_HG):
            lg_f, lg_b = log_gammas(h)
            d_ref[h] = jnp.exp(jnp.where(rel >= 0, lg_f * rel, -lg_b * rel))
            lf = lg_f[:, :HEAD_DIM]
            lb = lg_b[:, :HEAD_DIM]
            tab_ref[h, 0] = jnp.exp(lf * (idx + 1.0))
            tab_ref[h, 1] = jnp.exp(lb * (c_len - idx))
            tab_ref[h, 2] = jnp.exp(lf * (c_len - 1.0 - idx))
            tab_ref[h, 3] = jnp.exp(lb * idx)
        st_ref[...] = jnp.zeros(st_ref.shape, F32)

    @pl.when(s < n_seq)
    def _():
        first_chunk = (n_seq - 1 - s) * cps
        for h in range(RET_HG):
            chunk_b = jnp.exp(log_gammas(h)[1][:, :HEAD_DIM] * c_len)
            state = st_ref[h]
            for c in reversed(range(cps)):
                sb_ref[first_chunk + c, h] = state.astype(BF16)
                k = k_ref[chunk_rows[c], head_cols[h]]
                kd = (k.astype(F32) * tab_ref[h, 3]).astype(BF16)
                upd = lax.dot_general(kd, v_ref[chunk_rows[c], head_cols[h]], contract_rows,
                                      preferred_element_type=F32)
                state = state * chunk_b + upd
            st_ref[h] = state

        @pl.when(s == n_seq - 1)
        def _():
            st_ref[...] = jnp.zeros(st_ref.shape, F32)

    @pl.when(s >= n_seq)
    def _():
        first_chunk = (s - n_seq) * cps
        for h in range(RET_HG):
            chunk_f = jnp.exp(log_gammas(h)[0][:, :HEAD_DIM] * c_len)
            gw = gw_ref[:, head_cols[h]]
            gb = gb_ref[:, head_cols[h]]
            state = st_ref[h]
            for c in range(cps):
                q = q_ref[chunk_rows[c], head_cols[h]]
                k = k_ref[chunk_rows[c], head_cols[h]]
                v = v_ref[chunk_rows[c], head_cols[h]]
                scores = lax.dot_general(q, k, contract_cols, preferred_element_type=F32)
                masked = (scores * d_ref[h]).astype(BF16)
                qf = q.astype(F32)
                lhs = jnp.concatenate(
                    [masked, (qf * tab_ref[h, 0]).astype(BF16), (qf * tab_ref[h, 1]).astype(BF16)],
                    axis=1)
                rhs = jnp.concatenate([v, state.astype(BF16), sb_ref[first_chunk + c, h]], axis=0)
                out = jnp.dot(lhs, rhs, preferred_element_type=F32)
                mu = jnp.mean(out, axis=-1, keepdims=True)
                cen = out - mu
                var = jnp.mean(cen * cen, axis=-1, keepdims=True)
                y = cen * lax.rsqrt(var + NORM_EPS) * gw + gb
                g = g_ref[chunk_rows[c], head_cols[h]].astype(F32)
                y = y * (g / (1.0 + jnp.exp(-g)))
                o_ref[chunk_rows[c], head_cols[h]] = y.astype(BF16)
                kd = (k.astype(F32) * tab_ref[h, 2]).astype(BF16)
                upd = lax.dot_general(kd, v, contract_rows, preferred_element_type=F32)
                state = state * chunk_f + upd
            st_ref[h] = state


def _retention(proj, dec, gn_w, gn_b, cross_weights, batch, seq):
    n = proj.shape[0]
    assert seq % RET_TS == 0 and RET_TS % RET_C == 0 and RET_HEADS % RET_HG == 0
    n_seq = seq // RET_TS
    n_chunks = seq // RET_C
    group_w = RET_HG * HEAD_DIM
    groups = RET_HEADS // RET_HG

    def kv_block(section):
        def index(b, hg, s):
            blk = jnp.where(s < n_seq, n_seq - 1 - s, s - n_seq)
            return (b * n_seq + blk, section * groups + hg)
        return index

    def fwd_block(section):
        return lambda b, hg, s: (b * n_seq + jnp.maximum(s - n_seq, 0), section * groups + hg)

    cast_specs, cast_shapes, cast_bytes = _slab_cast_specs(
        cross_weights, batch * groups * 2 * n_seq,
        lambda b, hg, s: (b * groups + hg) * 2 * n_seq + s)
    vmem = (2 * 5 * RET_TS * group_w * 2 + RET_HG * RET_C * RET_C * 4
            + RET_HG * 4 * RET_C * HEAD_DIM * 4 + n_chunks * RET_HG * HEAD_DIM * HEAD_DIM * 2
            + RET_HG * HEAD_DIM * HEAD_DIM * 4 + cast_bytes)
    y_ret, *cast = pl.pallas_call(
        functools.partial(_retention_kernel, n_seq=n_seq),
        out_shape=(jax.ShapeDtypeStruct((n, RET_WIDTH), BF16), *cast_shapes),
        grid=(batch, groups, 2 * n_seq),
        in_specs=[
            pl.BlockSpec((RET_HG, 2, RET_C), lambda b, hg, s: (hg, 0, 0)),
            pl.BlockSpec((RET_TS, group_w), fwd_block(0)),
            pl.BlockSpec((RET_TS, group_w), kv_block(1)),
            pl.BlockSpec((RET_TS, group_w), kv_block(2)),
            pl.BlockSpec((RET_TS, group_w), fwd_block(3)),
            pl.BlockSpec((1, group_w), lambda b, hg, s: (0, hg)),
            pl.BlockSpec((1, group_w), lambda b, hg, s: (0, hg)),
            *cast_specs,
        ],
        out_specs=(pl.BlockSpec((RET_TS, group_w), fwd_block(0)), *cast_specs),
        scratch_shapes=[
            pltpu.VMEM((RET_HG, RET_C, RET_C), F32),
            pltpu.VMEM((RET_HG, 4, RET_C, HEAD_DIM), F32),
            pltpu.VMEM((n_chunks, RET_HG, HEAD_DIM, HEAD_DIM), BF16),
            pltpu.VMEM((RET_HG, HEAD_DIM, HEAD_DIM), F32),
        ],
        compiler_params=pltpu.CompilerParams(
            dimension_semantics=("arbitrary", "arbitrary", "arbitrary"),
            vmem_limit_bytes=_vmem_limit(vmem)),
        name="retention",
    )(dec, proj, proj, proj, proj, gn_w, gn_b, *cross_weights)
    return y_ret, cast


def _attention_kernel(q_ref, k_ref, v_ref, wu_ref, wd_ref, o_ref, wu_o_ref, wd_o_ref,
                      vt_ref, k2max_ref, qa_ref, acc_ref, den_ref, *, n_kv):
    _cast_slabs((wu_ref, wd_ref), (wu_o_ref, wd_o_ref))

    i = pl.program_id(2)
    contract_cols = (((1,), (1,)), ((), ()))

    def kv_rows(c):
        return pl.ds(pl.multiple_of(c * ATT_TK, ATT_TK), ATT_TK)

    @pl.when(i == 0)
    def _():
        def body(c, k2max):
            rows = kv_rows(c)
            vt_ref[c] = v_ref[rows, :].astype(F32).T.astype(BF16)
            kc = k_ref[rows, :].astype(F32)
            k2 = jnp.max(jnp.sum(kc * kc, axis=1, keepdims=True), axis=0, keepdims=True)
            return jnp.maximum(k2max, k2)
        k2max = lax.fori_loop(0, n_kv, body, jnp.zeros((1, 1), F32))
        k2max_ref[...] = jnp.broadcast_to(k2max, k2max_ref.shape)

    q_lane = lax.broadcasted_iota(jnp.int32, (ATT_TQ, HEAD_DIM), 1)
    k_lane = lax.broadcasted_iota(jnp.int32, (ATT_TK, HEAD_DIM), 1)
    k_one = jnp.where(k_lane == 0, 1.0, 0.0).astype(BF16)
    head_cols = [slice(h * HEAD_DIM, (h + 1) * HEAD_DIM) for h in range(GQA_GROUP)]

    def set_shift(h, shift):
        qa_ref[h, :, HEAD_DIM:] = jnp.where(q_lane == 0, -shift, 0.0).astype(BF16)

    def sweep():
        acc_ref[...] = jnp.zeros(acc_ref.shape, F32)
        den_ref[...] = jnp.zeros(den_ref.shape, F32)

        def kv_step(c, carry):
            k_aug = jnp.concatenate([k_ref[kv_rows(c), :], k_one], axis=1)
            vt = vt_ref[c]
            for h in range(GQA_GROUP):
                s_t = lax.dot_general(k_aug, qa_ref[h], contract_cols,
                                      preferred_element_type=F32)
                p_t = jnp.exp2(s_t)
                den_ref[h] += jnp.sum(p_t.reshape(ATT_TK // 8, 8, ATT_TQ), axis=0)
                acc_ref[h] += jnp.dot(vt, p_t.astype(BF16), preferred_element_type=F32)
            return carry
        lax.fori_loop(0, n_kv, kv_step, 0)

    ones = jnp.ones((HEAD_DIM, HEAD_DIM), BF16)
    for h in range(GQA_GROUP):
        qh = q_ref[:, head_cols[h]]
        qa_ref[h, :, 0:HEAD_DIM] = qh
        qf = qh.astype(F32)
        q2 = jnp.dot((qf * qf).astype(BF16), ones, preferred_element_type=F32)
        bound2 = q2 * k2max_ref[0:1, :]
        set_shift(h, bound2 * lax.rsqrt(bound2 + TINY))
    sweep()

    def denominator(h):
        return jnp.sum(den_ref[h], axis=0, keepdims=True)

    denom_min = functools.reduce(jnp.minimum,
                                 [jnp.min(denominator(h)) for h in range(GQA_GROUP)])

    @pl.when(denom_min < ATT_DENOM_FLOOR)
    def _():
        for h in range(GQA_GROUP):
            def body(c, m):
                s = lax.dot_general(q_ref[:, head_cols[h]], k_ref[kv_rows(c), :], contract_cols,
                                    preferred_element_type=F32)
                return jnp.maximum(m, jnp.max(s, axis=1, keepdims=True))
            set_shift(h, lax.fori_loop(0, n_kv, body, jnp.full((ATT_TQ, 1), NEG_BIG, F32)))
        sweep()

    for h in range(GQA_GROUP):
        o_t = acc_ref[h] * (1.0 / denominator(h))
        o_ref[:, head_cols[h]] = o_t.T.astype(BF16)


def _attention(proj, w_up, w_down, batch, seq):
    n = proj.shape[0]
    assert seq % ATT_TQ == 0 and seq % ATT_TK == 0
    n_q = seq // ATT_TQ
    n_kv = seq // ATT_TK
    group_w = GQA_GROUP * HEAD_DIM
    q_col0 = 4 * RET_WIDTH // group_w
    k_col0 = (4 * RET_WIDTH + ATTN_WIDTH) // HEAD_DIM
    v_col0 = k_col0 + ATTN_KV_HEADS
    cast_specs, cast_shapes, cast_bytes = _slab_cast_specs(
        (w_up, w_down), batch * ATTN_KV_HEADS * n_q,
        lambda b, g, i: (b * ATTN_KV_HEADS + g) * n_q + i)
    vmem = (2 * ATT_TQ * group_w * 2 * 2 + 2 * 2 * seq * HEAD_DIM * 2
            + HEAD_DIM * seq * 2 + GQA_GROUP * (HEAD_DIM + 8) * ATT_TQ * 4
            + GQA_GROUP * ATT_TQ * 2 * HEAD_DIM * 2 + 2 * GQA_GROUP * ATT_TK * ATT_TQ * 4
            + cast_bytes)
    return pl.pallas_call(
        functools.partial(_attention_kernel, n_kv=n_kv),
        out_shape=(jax.ShapeDtypeStruct((n, ATTN_WIDTH), BF16), *cast_shapes),
        grid=(batch, ATTN_KV_HEADS, n_q),
        in_specs=[
            pl.BlockSpec((ATT_TQ, group_w), lambda b, g, i: (b * n_q + i, q_col0 + g)),
            pl.BlockSpec((seq, HEAD_DIM), lambda b, g, i: (b, k_col0 + g)),
            pl.BlockSpec((seq, HEAD_DIM), lambda b, g, i: (b, v_col0 + g)),
            *cast_specs,
        ],
        out_specs=(pl.BlockSpec((ATT_TQ, group_w), lambda b, g, i: (b * n_q + i, g)),
                   *cast_specs),
        scratch_shapes=[
            pltpu.VMEM((n_kv, HEAD_DIM, ATT_TK), BF16),
            pltpu.VMEM((8, HEAD_DIM), F32),
            pltpu.VMEM((GQA_GROUP, ATT_TQ, 2 * HEAD_DIM), BF16),
            pltpu.VMEM((GQA_GROUP, HEAD_DIM, ATT_TQ), F32),
            pltpu.VMEM((GQA_GROUP, 8, ATT_TQ), F32),
        ],
        compiler_params=pltpu.CompilerParams(
            dimension_semantics=("arbitrary", "arbitrary", "arbitrary"),
            vmem_limit_bytes=_vmem_limit(vmem)),
        name="attention",
    )(proj, proj, proj, w_up, w_down)


def _mem_kv_kernel(m_ref, nw_ref, wk_ref, wv_ref, o_ref):
    m = m_ref[...]
    h = (m * _rms_scale(m) * nw_ref[...]).astype(BF16)
    o_ref[:, :CROSS_WIDTH] = jnp.dot(h, wk_ref[...].astype(BF16),
                                     preferred_element_type=F32).astype(BF16)
    o_ref[:, CROSS_WIDTH:] = jnp.dot(h, wv_ref[...].astype(BF16),
                                     preferred_element_type=F32).astype(BF16)


def _mem_kv(mem2d, norm_w, w_k, w_v):
    rows = mem2d.shape[0]
    vmem = 2 * (rows * D_MODEL * 4 + 2 * D_MODEL * CROSS_WIDTH * 4 + rows * 2 * CROSS_WIDTH * 2)
    return pl.pallas_call(
        _mem_kv_kernel,
        out_shape=jax.ShapeDtypeStruct((rows, 2 * CROSS_WIDTH), BF16),
        grid=(1,),
        in_specs=[
            pl.BlockSpec((rows, D_MODEL), lambda i: (0, 0)),
            pl.BlockSpec((1, D_MODEL), lambda i: (0, 0)),
            pl.BlockSpec((D_MODEL, CROSS_WIDTH), lambda i: (0, 0)),
            pl.BlockSpec((D_MODEL, CROSS_WIDTH), lambda i: (0, 0)),
        ],
        out_specs=pl.BlockSpec((rows, 2 * CROSS_WIDTH), lambda i: (0, 0)),
        compiler_params=pltpu.CompilerParams(
            dimension_semantics=("arbitrary",),
            vmem_limit_bytes=_vmem_limit(vmem)),
        name="mem_kv",
    )(mem2d, norm_w, w_k, w_v)


def _out_cross_kernel(x_ref, yr_ref, ya_ref, wo_ref, nw_ref, wq_ref, kv_ref, wco_ref, o_ref):
    contract_cols = (((1,), (1,)), ((), ()))
    scale = CROSS_HEAD_DIM ** -0.5

    for r in range(OC_TM // OC_RC):
        rows = slice(r * OC_RC, (r + 1) * OC_RC)
        x1 = (x_ref[rows, :]
              + jnp.dot(yr_ref[rows, :], wo_ref[0:RET_WIDTH, :], preferred_element_type=F32)
              + jnp.dot(ya_ref[rows, :], wo_ref[RET_WIDTH:, :], preferred_element_type=F32))
        h = (x1 * _rms_scale(x1) * nw_ref[...]).astype(BF16)
        q = (jnp.dot(h, wq_ref[...], preferred_element_type=F32) * scale).astype(BF16)
        heads = []
        for hd in range(CROSS_HEADS):
            cols = slice(hd * CROSS_HEAD_DIM, (hd + 1) * CROSS_HEAD_DIM)
            k = kv_ref[:, cols]
            v = kv_ref[:, CROSS_WIDTH + hd * CROSS_HEAD_DIM:CROSS_WIDTH + (hd + 1) * CROSS_HEAD_DIM]
            s = lax.dot_general(q[:, cols], k, contract_cols, preferred_element_type=F32)
            e = jnp.exp(s - jnp.max(s, axis=-1, keepdims=True))
            inv = 1.0 / jnp.sum(e, axis=-1, keepdims=True)
            pv = jnp.dot(e.astype(BF16), v, preferred_element_type=F32)
            heads.append((pv * inv).astype(BF16))
        o = jnp.concatenate(heads, axis=1)
        o_ref[rows, :] = x1 + jnp.dot(o, wco_ref[...], preferred_element_type=F32)


def _out_cross(x2d, y_ret, y_attn, w_out, norm_w, w_q, mem_kv, w_co, seq):
    n = x2d.shape[0]
    assert n % OC_TM == 0 and seq % OC_TM == 0
    tiles_per_seq = seq // OC_TM
    vmem = (2 * 2 * OC_TM * D_MODEL * 4 + 2 * 2 * OC_TM * RET_WIDTH * 2 + 2 * D_MODEL * D_MODEL * 2
            + 2 * 2 * D_MODEL * CROSS_WIDTH * 2 + 2 * MEM_TOKENS * 2 * CROSS_WIDTH * 2)
    return pl.pallas_call(
        _out_cross_kernel,
        out_shape=jax.ShapeDtypeStruct((n, D_MODEL), F32),
        grid=(n // OC_TM,),
        in_specs=[
            pl.BlockSpec((OC_TM, D_MODEL), lambda i: (i, 0)),
            pl.BlockSpec((OC_TM, RET_WIDTH), lambda i: (i, 0)),
            pl.BlockSpec((OC_TM, ATTN_WIDTH), lambda i: (i, 0)),
            pl.BlockSpec((D_MODEL, D_MODEL), lambda i: (0, 0)),
            pl.BlockSpec((1, D_MODEL), lambda i: (0, 0)),
            pl.BlockSpec((D_MODEL, CROSS_WIDTH), lambda i: (0, 0)),
            pl.BlockSpec((MEM_TOKENS, 2 * CROSS_WIDTH), lambda i: (i // tiles_per_seq, 0)),
            pl.BlockSpec((CROSS_WIDTH, D_MODEL), lambda i: (0, 0)),
        ],
        out_specs=pl.BlockSpec((OC_TM, D_MODEL), lambda i: (i, 0)),
        compiler_params=pltpu.CompilerParams(
            dimension_semantics=("arbitrary",),
            vmem_limit_bytes=_vmem_limit(vmem)),
        name="out_cross",
    )(x2d, y_ret, y_attn, w_out, norm_w, w_q, mem_kv, w_co)


def _mlp_kernel(x_ref, nw_ref, wu_ref, wd_ref, fw_ref, o_ref, h_ref):
    j = pl.program_id(1)
    last = pl.num_programs(1) - 1

    def step(rows, is_first, is_last):
        if is_first:
            xr = x_ref[rows, :]
            h = (xr * _rms_scale(xr) * nw_ref[...]).astype(BF16)
            h_ref[rows, :] = h
            base = xr
        else:
            h = h_ref[rows, :]
            base = o_ref[rows, :]
        u = jnp.maximum(jnp.dot(h, wu_ref[...], preferred_element_type=F32), 0.0)
        y = base + jnp.dot((u * u).astype(BF16), wd_ref[...], preferred_element_type=F32)
        if is_last:
            y = y * _rms_scale(y) * fw_ref[...]
        o_ref[rows, :] = y

    edge_chunks = [slice(r * MLP_EDGE_RC, (r + 1) * MLP_EDGE_RC)
                   for r in range(MLP_TM // MLP_EDGE_RC)]

    @pl.when(j == 0)
    def _():
        for rows in edge_chunks:
            step(rows, True, False)

    @pl.when((j > 0) & (j < last))
    def _():
        step(slice(None), False, False)

    @pl.when(j == last)
    def _():
        for rows in edge_chunks:
            step(rows, False, True)


def _mlp(x2d, norm_w, w_up, w_down, final_w):
    n = x2d.shape[0]
    assert n % MLP_TM == 0 and D_FF % MLP_TF == 0
    vmem = (2 * 2 * MLP_TM * D_MODEL * 4 + MLP_TM * D_MODEL * 2 + 2 * 2 * D_MODEL * MLP_TF * 2)
    return pl.pallas_call(
        _mlp_kernel,
        out_shape=jax.ShapeDtypeStruct((n, D_MODEL), F32),
        grid=(n // MLP_TM, D_FF // MLP_TF),
        in_specs=[
            pl.BlockSpec((MLP_TM, D_MODEL), lambda i, j: (i, 0)),
            pl.BlockSpec((1, D_MODEL), lambda i, j: (0, 0)),
            pl.BlockSpec((D_MODEL, MLP_TF), lambda i, j: (0, j)),
            pl.BlockSpec((MLP_TF, D_MODEL), lambda i, j: (j, 0)),
            pl.BlockSpec((1, D_MODEL), lambda i, j: (0, 0)),
        ],
        out_specs=pl.BlockSpec((MLP_TM, D_MODEL), lambda i, j: (i, 0)),
        scratch_shapes=[pltpu.VMEM((MLP_TM, D_MODEL), BF16)],
        compiler_params=pltpu.CompilerParams(
            dimension_semantics=("arbitrary", "arbitrary"),
            vmem_limit_bytes=_vmem_limit(vmem)),
        name="mlp",
    )(x2d, norm_w, w_up, w_down, final_w)


def _rope_tables(seq):
    t = np.arange(seq)
    row = (t // GRID_W).astype(np.float64)
    col = (t % GRID_W).astype(np.float64)
    inv_freq = 1.0 / (ROPE_THETA ** (np.arange(0, AXIS_DIM, 2, dtype=np.float64) / AXIS_DIM))
    ang_r = row[:, None] * inv_freq[None, :]
    ang_c = col[:, None] * inv_freq[None, :]
    cos_t = np.concatenate([np.cos(ang_r), np.cos(ang_c), np.cos(ang_r), np.cos(ang_c)], axis=-1)
    sin_t = np.concatenate([-np.sin(ang_r), -np.sin(ang_c), np.sin(ang_r), np.sin(ang_c)], axis=-1)
    return jnp.asarray(cos_t, F32), jnp.asarray(sin_t, F32)


def _pair_heads(w):
    lead = w.shape[:-1]
    heads = w.shape[-1] // HEAD_DIM
    quarter = HEAD_DIM // 4
    w = w.reshape(lead + (heads, 2, 2, quarter))
    return jnp.swapaxes(w, -2, -3).reshape(lead + (heads * HEAD_DIM,))


def _w_in_prep_kernel(w_ref, o_ref):
    lane = lax.broadcasted_iota(jnp.int32, (WPREP_ROWS, HEAD_DIM), 1)
    quarter = HEAD_DIM // 4
    takes_c1 = (lane >= quarter) & (lane < 2 * quarter)
    takes_r2 = (lane >= 2 * quarter) & (lane < 3 * quarter)
    rope_slabs = (set(range(0, 2 * RET_HEADS))
                  | set(range(4 * RET_HEADS, 4 * RET_HEADS + ATTN_HEADS + ATTN_KV_HEADS)))
    for s in range(IN_WIDTH // HEAD_DIM):
        cols = slice(s * HEAD_DIM, (s + 1) * HEAD_DIM)
        y = w_ref[:, cols]
        if s in rope_slabs:
            y = jnp.where(takes_c1, pltpu.roll(y, HEAD_DIM - quarter, 1),
                          jnp.where(takes_r2, pltpu.roll(y, quarter, 1), y))
        o_ref[:, cols] = y.astype(BF16)


def _in_proj_weights(w_in):
    rows = w_in.shape[0]
    assert rows % WPREP_ROWS == 0
    vmem = 2 * WPREP_ROWS * IN_WIDTH * (4 + 2)
    return pl.pallas_call(
        _w_in_prep_kernel,
        out_shape=jax.ShapeDtypeStruct((rows, IN_WIDTH), BF16),
        grid=(rows // WPREP_ROWS,),
        in_specs=[pl.BlockSpec((WPREP_ROWS, IN_WIDTH), lambda i: (i, 0))],
        out_specs=pl.BlockSpec((WPREP_ROWS, IN_WIDTH), lambda i: (i, 0)),
        compiler_params=pltpu.CompilerParams(
            dimension_semantics=("arbitrary",),
            vmem_limit_bytes=_vmem_limit(vmem)),
        name="w_in_prep",
    )(w_in)


def kernel(x, mem, norm_mix_w, w_in, ret_decay_fwd, ret_decay_bwd, ret_gn_w, ret_gn_b, attn_q_norm_w, attn_k_norm_w, w_out, norm_cross_w, norm_mem_w, w_cross_q, w_cross_k, w_cross_v, w_cross_o, norm_mlp_w, w_mlp_up, w_mlp_down, norm_final_w):
    batch, seq, _ = x.shape
    assert w_in.shape[0] == 1, "single-layer block: per-layer parameters have a leading axis of 1"
    cos_t, sin_t = _rope_tables(seq)
    xs = x.reshape(batch * seq, D_MODEL)
    mem2d = mem.reshape(batch * MEM_TOKENS, D_MODEL)
    dec = jnp.stack([ret_decay_fwd[0], ret_decay_bwd[0]], axis=1)
    dec = jnp.broadcast_to(dec[:, :, None], (RET_HEADS, 2, RET_C)).astype(F32)
    proj = _in_proj(xs, norm_mix_w, _in_proj_weights(w_in[0]), cos_t, sin_t,
                    _pair_heads(attn_q_norm_w), _pair_heads(attn_k_norm_w), seq)
    y_ret, (w_out_bf16, w_cq_bf16, w_co_bf16) = _retention(
        proj, dec, ret_gn_w, ret_gn_b, (w_out[0], w_cross_q[0], w_cross_o[0]), batch, seq)
    y_attn, w_up_bf16, w_down_bf16 = _attention(proj, w_mlp_up[0], w_mlp_down[0], batch, seq)
    mem_kv = _mem_kv(mem2d, norm_mem_w, w_cross_k[0], w_cross_v[0])
    xs = _out_cross(xs, y_ret, y_attn, w_out_bf16, norm_cross_w, w_cq_bf16, mem_kv, w_co_bf16, seq)
    xs = _mlp(xs, norm_mlp_w, w_up_bf16, w_down_bf16, norm_final_w[None, :])
    return xs.reshape(batch, seq, D_MODEL)
```

```python
import functools
import math

import numpy as np
import jax
import jax.numpy as jnp
from jax import lax
from jax.experimental import pallas as pl
from jax.experimental.pallas import tpu as pltpu

D_MODEL = 2048
HEAD_DIM = 128
RET_WIDTH = 1024
ATTN_WIDTH = 1024
RET_HEADS = 8
ATTN_HEADS = 8
ATTN_KV_HEADS = 2
GQA_GROUP = ATTN_HEADS // ATTN_KV_HEADS
KV_WIDTH = ATTN_KV_HEADS * HEAD_DIM
IN_WIDTH = 4 * RET_WIDTH + ATTN_WIDTH + 2 * KV_WIDTH
GRID_W = 64
AXIS_DIM = HEAD_DIM // 2
ROPE_THETA = 10000.0
MEM_TOKENS = 256
CROSS_HEADS = 4
CROSS_HEAD_DIM = 128
CROSS_WIDTH = CROSS_HEADS * CROSS_HEAD_DIM
D_FF = 4 * D_MODEL
NORM_EPS = 1e-6

V7X_LANES = 128
V7X_BF16_SUBLANES = 16
V7X_VMEM_BYTES = 64 * 1024 * 1024

F32 = jnp.float32
BF16 = jnp.bfloat16
LOG2E = math.log2(math.e)
NEG_BIG = -1e30
TINY = 1e-30

IN_TM = 512
IN_TN = 512
IN_NORM_RC = 256
WPREP_ROWS = 256
RET_C = 256
RET_TS = 2048
RET_HG = 4
ATT_TQ = 1024
ATT_TK = 4096
ATT_DENOM_FLOOR = 2.0 ** -40
OC_TM = 512
OC_RC = 512
MLP_TM = 1024
MLP_TF = 512
MLP_EDGE_RC = 512


def _vmem_limit(nbytes):
    return int(min(nbytes + 16 * 1024 * 1024, V7X_VMEM_BYTES - 8 * 1024 * 1024))


def _rms_scale(y):
    return lax.rsqrt(jnp.mean(y * y, axis=-1, keepdims=True) + NORM_EPS)


def _slab_cast_specs(weights, steps, flat_step):
    specs, shapes, nbytes = [], [], 0
    for w in weights:
        assert w.shape[0] % steps == 0 and (w.shape[0] // steps) % V7X_BF16_SUBLANES == 0
        block = (w.shape[0] // steps, w.shape[1])
        specs.append(pl.BlockSpec(block, lambda *ids: (flat_step(*ids), 0)))
        shapes.append(jax.ShapeDtypeStruct(w.shape, BF16))
        nbytes += 2 * block[0] * block[1] * (4 + 2)
    return specs, shapes, nbytes


def _cast_slabs(src_refs, dst_refs):
    for src, dst in zip(src_refs, dst_refs):
        dst[...] = src[...].astype(BF16)


def _rope(y, cos, sin_signed):
    return y * cos + pltpu.roll(y, HEAD_DIM // 2, 1) * sin_signed


def _in_proj_kernel(x_ref, nw_ref, w_ref, cos_ref, sin_ref, qg_ref, kg_ref, o_ref, h_ref):
    for r in range(IN_TM // IN_NORM_RC):
        rows = slice(r * IN_NORM_RC, (r + 1) * IN_NORM_RC)
        xr = x_ref[rows, :]
        h_ref[rows, :] = (xr * _rms_scale(xr) * nw_ref[...]).astype(BF16)

    ret_k_scale = HEAD_DIM ** -0.5
    attn_q_scale = (HEAD_DIM ** -0.5) * LOG2E

    n_slabs = IN_TN // HEAD_DIM

    def run(j, epilogue):
        tile = slice(j * IN_TN, (j + 1) * IN_TN)
        acc = jnp.dot(h_ref[...], w_ref[:, tile], preferred_element_type=F32)
        slabs = epilogue([acc[:, s * HEAD_DIM:(s + 1) * HEAD_DIM] for s in range(n_slabs)],
                         cos_ref[...], sin_ref[...])
        for s, y in enumerate(slabs):
            c0 = j * IN_TN + s * HEAD_DIM
            o_ref[:, c0:c0 + HEAD_DIM] = y.astype(BF16)

    def ret_q(ys, cos, sin):
        return [_rope(y, cos, sin) for y in ys]

    def ret_k(ys, cos, sin):
        return [_rope(y, cos, sin) * ret_k_scale for y in ys]

    def plain(ys, cos, sin):
        return ys

    def head_inv_rms(y_a, y_b):
        width = 2 * HEAD_DIM
        r = lax.broadcasted_iota(jnp.int32, (width, width), 0) // HEAD_DIM
        c = lax.broadcasted_iota(jnp.int32, (width, width), 1) // HEAD_DIM
        ones_bd = jnp.where(r == c, 1.0, 0.0).astype(BF16)
        sq = jnp.concatenate([(y_a * y_a).astype(BF16), (y_b * y_b).astype(BF16)], axis=1)
        ms = jnp.dot(sq, ones_bd, preferred_element_type=F32) * (1.0 / HEAD_DIM)
        inv = lax.rsqrt(ms + NORM_EPS)
        return inv[:, :HEAD_DIM], inv[:, HEAD_DIM:]

    def normed_rope(y_a, y_b, gain, scale, cos, sin):
        inv_a, inv_b = head_inv_rms(y_a, y_b)
        return [_rope(y_a * gain, cos, sin) * (inv_a * scale),
                _rope(y_b * gain, cos, sin) * (inv_b * scale)]

    def attn_q(ys, cos, sin):
        gain = qg_ref[...]
        return (normed_rope(ys[0], ys[1], gain, attn_q_scale, cos, sin)
                + normed_rope(ys[2], ys[3], gain, attn_q_scale, cos, sin))

    def attn_kv(ys, cos, sin):
        return normed_rope(ys[0], ys[1], kg_ref[...], 1.0, cos, sin) + ys[ATTN_KV_HEADS:]

    nb = RET_WIDTH // IN_TN
    for j in range(IN_WIDTH // IN_TN):
        if j < nb:
            run(j, ret_q)
        elif j < 2 * nb:
            run(j, ret_k)
        elif j < 4 * nb:
            run(j, plain)
        elif j < 5 * nb:
            run(j, attn_q)
        else:
            run(j, attn_kv)


def _in_proj(x2d, norm_w, w_in, cos_t, sin_t, q_gain, k_gain, seq):
    n = x2d.shape[0]
    assert IN_WIDTH % IN_TN == 0 and 2 * KV_WIDTH == IN_TN and n % IN_TM == 0 and seq % IN_TM == 0
    tiles_per_seq = seq // IN_TM
    vmem = (2 * IN_TM * D_MODEL * 4 + IN_TM * D_MODEL * 2 + D_MODEL * IN_WIDTH * 2
            + 2 * IN_TM * IN_WIDTH * 2 + 4 * IN_TM * HEAD_DIM * 4)
    resident = pl.Buffered(1)
    return pl.pallas_call(
        _in_proj_kernel,
        out_shape=jax.ShapeDtypeStruct((n, IN_WIDTH), BF16),
        grid=(n // IN_TM,),
        in_specs=[
            pl.BlockSpec((IN_TM, D_MODEL), lambda i: (i, 0)),
            pl.BlockSpec((1, D_MODEL), lambda i: (0, 0)),
            pl.BlockSpec((D_MODEL, IN_WIDTH), lambda i: (0, 0), pipeline_mode=resident),
            pl.BlockSpec((IN_TM, HEAD_DIM), lambda i: (i % tiles_per_seq, 0)),
            pl.BlockSpec((IN_TM, HEAD_DIM), lambda i: (i % tiles_per_seq, 0)),
            pl.BlockSpec((1, HEAD_DIM), lambda i: (0, 0)),
            pl.BlockSpec((1, HEAD_DIM), lambda i: (0, 0)),
        ],
        out_specs=pl.BlockSpec((IN_TM, IN_WIDTH), lambda i: (i, 0)),
        scratch_shapes=[pltpu.VMEM((IN_TM, D_MODEL), BF16)],
        compiler_params=pltpu.CompilerParams(
            dimension_semantics=("arbitrary",),
            vmem_limit_bytes=_vmem_limit(vmem)),
        name="in_proj",
    )(x2d, norm_w, w_in, cos_t, sin_t, q_gain, k_gain)


def _retention_kernel(dec_ref, q_ref, k_ref, v_ref, g_ref, gw_ref, gb_ref,
                      wout_ref, wq_ref, wco_ref, o_ref, wout_o_ref, wq_o_ref, wco_o_ref,
                      d_ref, tab_ref, sb_ref, st_ref, *, n_seq):
    _cast_slabs((wout_ref, wq_ref, wco_ref), (wout_o_ref, wq_o_ref, wco_o_ref))
    s = pl.program_id(2)
    c_len = RET_C
    cps = RET_TS // RET_C
    contract_rows = (((0,), (0,)), ((), ()))
    contract_cols = (((1,), (1,)), ((), ()))
    head_cols = [slice(h * HEAD_DIM, (h + 1) * HEAD_DIM) for h in range(RET_HG)]
    chunk_rows = [slice(c * c_len, (c + 1) * c_len) for c in range(cps)]

    def log_gammas(h):
        return -jnp.exp(dec_ref[h, 0:1, :]), -jnp.exp(dec_ref[h, 1:2, :])

    @pl.when(s == 0)
    def _():
        row = lax.broadcasted_iota(jnp.int32, (c_len, c_len), 0)
        col = lax.broadcasted_iota(jnp.int32, (c_len, c_len), 1)
        rel = (row - col).astype(F32)
        idx = lax.broadcasted_iota(jnp.int32, (c_len, HEAD_DIM), 0).astype(F32)
        for h in range(RET_HG):
            lg_f, lg_b = log_gammas(h)
            d_ref[h] = jnp.exp(jnp.where(rel >= 0, lg_f * rel, -lg_b * rel))
            lf = lg_f[:, :HEAD_DIM]
            lb = lg_b[:, :HEAD_DIM]
            tab_ref[h, 0] = jnp.exp(lf * (idx + 1.0))
            tab_ref[h, 1] = jnp.exp(lb * (c_len - idx))
            tab_ref[h, 2] = jnp.exp(lf * (c_len - 1.0 - idx))
            tab_ref[h, 3] = jnp.exp(lb * idx)
        st_ref[...] = jnp.zeros(st_ref.shape, F32)

    @pl.when(s < n_seq)
    def _():
        first_chunk = (n_seq - 1 - s) * cps
        for h in range(RET_HG):
            chunk_b = jnp.exp(log_gammas(h)[1][:, :HEAD_DIM] * c_len)
            state = st_ref[h]
            for c in reversed(range(cps)):
                sb_ref[first_chunk + c, h] = state.astype(BF16)
                k = k_ref[chunk_rows[c], head_cols[h]]
                kd = (k.astype(F32) * tab_ref[h, 3]).astype(BF16)
                upd = lax.dot_general(kd, v_ref[chunk_rows[c], head_cols[h]], contract_rows,
                                      preferred_element_type=F32)
                state = state * chunk_b + upd
            st_ref[h] = state

        @pl.when(s == n_seq - 1)
        def _():
            st_ref[...] = jnp.zeros(st_ref.shape, F32)

    @pl.when(s >= n_seq)
    def _():
        first_chunk = (s - n_seq) * cps
        for h in range(RET_HG):
            chunk_f = jnp.exp(log_gammas(h)[0][:, :HEAD_DIM] * c_len)
            gw = gw_ref[:, head_cols[h]]
            gb = gb_ref[:, head_cols[h]]
            state = st_ref[h]
            for c in range(cps):
                q = q_ref[chunk_rows[c], head_cols[h]]
                k = k_ref[chunk_rows[c], head_cols[h]]
                v = v_ref[chunk_rows[c], head_cols[h]]
                scores = lax.dot_general(q, k, contract_cols, preferred_element_type=F32)
                masked = (scores * d_ref[h]).astype(BF16)
                qf = q.astype(F32)
                lhs = jnp.concatenate(
                    [masked, (qf * tab_ref[h, 0]).astype(BF16), (qf * tab_ref[h, 1]).astype(BF16)],
                    axis=1)
                rhs = jnp.concatenate([v, state.astype(BF16), sb_ref[first_chunk + c, h]], axis=0)
                out = jnp.dot(lhs, rhs, preferred_element_type=F32)
                mu = jnp.mean(out, axis=-1, keepdims=True)
                cen = out - mu
                var = jnp.mean(cen * cen, axis=-1, keepdims=True)
                y = cen * lax.rsqrt(var + NORM_EPS) * gw + gb
                g = g_ref[chunk_rows[c], head_cols[h]].astype(F32)
                y = y * (g / (1.0 + jnp.exp(-g)))
                o_ref[chunk_rows[c], head_cols[h]] = y.astype(BF16)
                kd = (k.astype(F32) * tab_ref[h, 2]).astype(BF16)
                upd = lax.dot_general(kd, v, contract_rows, preferred_element_type=F32)
                state = state * chunk_f + upd
            st_ref[h] = state


def _retention(proj, dec, gn_w, gn_b, cross_weights, batch, seq):
    n = proj.shape[0]
    assert seq % RET_TS == 0 and RET_TS % RET_C == 0 and RET_HEADS % RET_HG == 0
    n_seq = seq // RET_TS
    n_chunks = seq // RET_C
    group_w = RET_HG * HEAD_DIM
    groups = RET_HEADS // RET_HG

    def kv_block(section):
        def index(b, hg, s):
            blk = jnp.where(s < n_seq, n_seq - 1 - s, s - n_seq)
            return (b * n_seq + blk, section * groups + hg)
        return index

    def fwd_block(section):
        return lambda b, hg, s: (b * n_seq + jnp.maximum(s - n_seq, 0), section * groups + hg)

    cast_specs, cast_shapes, cast_bytes = _slab_cast_specs(
        cross_weights, batch * groups * 2 * n_seq,
        lambda b, hg, s: (b * groups + hg) * 2 * n_seq + s)
    vmem = (2 * 5 * RET_TS * group_w * 2 + RET_HG * RET_C * RET_C * 4
            + RET_HG * 4 * RET_C * HEAD_DIM * 4 + n_chunks * RET_HG * HEAD_DIM * HEAD_DIM * 2
            + RET_HG * HEAD_DIM * HEAD_DIM * 4 + cast_bytes)
    y_ret, *cast = pl.pallas_call(
        functools.partial(_retention_kernel, n_seq=n_seq),
        out_shape=(jax.ShapeDtypeStruct((n, RET_WIDTH), BF16), *cast_shapes),
        grid=(batch, groups, 2 * n_seq),
        in_specs=[
            pl.BlockSpec((RET_HG, 2, RET_C), lambda b, hg, s: (hg, 0, 0)),
            pl.BlockSpec((RET_TS, group_w), fwd_block(0)),
            pl.BlockSpec((RET_TS, group_w), kv_block(1)),
            pl.BlockSpec((RET_TS, group_w), kv_block(2)),
            pl.BlockSpec((RET_TS, group_w), fwd_block(3)),
            pl.BlockSpec((1, group_w), lambda b, hg, s: (0, hg)),
            pl.BlockSpec((1, group_w), lambda b, hg, s: (0, hg)),
            *cast_specs,
        ],
        out_specs=(pl.BlockSpec((RET_TS, group_w), fwd_block(0)), *cast_specs),
        scratch_shapes=[
            pltpu.VMEM((RET_HG, RET_C, RET_C), F32),
            pltpu.VMEM((RET_HG, 4, RET_C, HEAD_DIM), F32),
            pltpu.VMEM((n_chunks, RET_HG, HEAD_DIM, HEAD_DIM), BF16),
            pltpu.VMEM((RET_HG, HEAD_DIM, HEAD_DIM), F32),
        ],
        compiler_params=pltpu.CompilerParams(
            dimension_semantics=("arbitrary", "arbitrary", "arbitrary"),
            vmem_limit_bytes=_vmem_limit(vmem)),
        name="retention",
    )(dec, proj, proj, proj, proj, gn_w, gn_b, *cross_weights)
    return y_ret, cast


def _attention_kernel(q_ref, k_ref, v_ref, wu_ref, wd_ref, o_ref, wu_o_ref, wd_o_ref,
                      vt_ref, k2max_ref, qa_ref, acc_ref, den_ref, *, n_kv):
    _cast_slabs((wu_ref, wd_ref), (wu_o_ref, wd_o_ref))

    i = pl.program_id(2)
    contract_cols = (((1,), (1,)), ((), ()))

    def kv_rows(c):
        return pl.ds(pl.multiple_of(c * ATT_TK, ATT_TK), ATT_TK)

    @pl.when(i == 0)
    def _():
        def body(c, k2max):
            rows = kv_rows(c)
            vt_ref[c] = v_ref[rows, :].astype(F32).T.astype(BF16)
            kc = k_ref[rows, :].astype(F32)
            k2 = jnp.max(jnp.sum(kc * kc, axis=1, keepdims=True), axis=0, keepdims=True)
            return jnp.maximum(k2max, k2)
        k2max = lax.fori_loop(0, n_kv, body, jnp.zeros((1, 1), F32))
        k2max_ref[...] = jnp.broadcast_to(k2max, k2max_ref.shape)

    q_lane = lax.broadcasted_iota(jnp.int32, (ATT_TQ, HEAD_DIM), 1)
    k_lane = lax.broadcasted_iota(jnp.int32, (ATT_TK, HEAD_DIM), 1)
    k_one = jnp.where(k_lane == 0, 1.0, 0.0).astype(BF16)
    head_cols = [slice(h * HEAD_DIM, (h + 1) * HEAD_DIM) for h in range(GQA_GROUP)]

    def set_shift(h, shift):
        qa_ref[h, :, HEAD_DIM:] = jnp.where(q_lane == 0, -shift, 0.0).astype(BF16)

    def sweep():
        acc_ref[...] = jnp.zeros(acc_ref.shape, F32)
        den_ref[...] = jnp.zeros(den_ref.shape, F32)

        def kv_step(c, carry):
            k_aug = jnp.concatenate([k_ref[kv_rows(c), :], k_one], axis=1)
            vt = vt_ref[c]
            for h in range(GQA_GROUP):
                s_t = lax.dot_general(k_aug, qa_ref[h], contract_cols,
                                      preferred_element_type=F32)
                p_t = jnp.exp2(s_t)
                den_ref[h] += jnp.sum(p_t.reshape(ATT_TK // 8, 8, ATT_TQ), axis=0)
                acc_ref[h] += jnp.dot(vt, p_t.astype(BF16), preferred_element_type=F32)
            return carry
        lax.fori_loop(0, n_kv, kv_step, 0)

    ones = jnp.ones((HEAD_DIM, HEAD_DIM), BF16)
    for h in range(GQA_GROUP):
        qh = q_ref[:, head_cols[h]]
        qa_ref[h, :, 0:HEAD_DIM] = qh
        qf = qh.astype(F32)
        q2 = jnp.dot((qf * qf).astype(BF16), ones, preferred_element_type=F32)
        bound2 = q2 * k2max_ref[0:1, :]
        set_shift(h, bound2 * lax.rsqrt(bound2 + TINY))
    sweep()

    def denominator(h):
        return jnp.sum(den_ref[h], axis=0, keepdims=True)

    denom_min = functools.reduce(jnp.minimum,
                                 [jnp.min(denominator(h)) for h in range(GQA_GROUP)])

    @pl.when(denom_min < ATT_DENOM_FLOOR)
    def _():
        for h in range(GQA_GROUP):
            def body(c, m):
                s = lax.dot_general(q_ref[:, head_cols[h]], k_ref[kv_rows(c), :], contract_cols,
                                    preferred_element_type=F32)
                return jnp.maximum(m, jnp.max(s, axis=1, keepdims=True))
            set_shift(h, lax.fori_loop(0, n_kv, body, jnp.full((ATT_TQ, 1), NEG_BIG, F32)))
        sweep()

    for h in range(GQA_GROUP):
        o_t = acc_ref[h] * (1.0 / denominator(h))
        o_ref[:, head_cols[h]] = o_t.T.astype(BF16)


def _attention(proj, w_up, w_down, batch, seq):
    n = proj.shape[0]
    assert seq % ATT_TQ == 0 and seq % ATT_TK == 0
    n_q = seq // ATT_TQ
    n_kv = seq // ATT_TK
    group_w = GQA_GROUP * HEAD_DIM
    q_col0 = 4 * RET_WIDTH // group_w
    k_col0 = (4 * RET_WIDTH + ATTN_WIDTH) // HEAD_DIM
    v_col0 = k_col0 + ATTN_KV_HEADS
    cast_specs, cast_shapes, cast_bytes = _slab_cast_specs(
        (w_up, w_down), batch * ATTN_KV_HEADS * n_q,
        lambda b, g, i: (b * ATTN_KV_HEADS + g) * n_q + i)
    vmem = (2 * ATT_TQ * group_w * 2 * 2 + 2 * 2 * seq * HEAD_DIM * 2
            + HEAD_DIM * seq * 2 + GQA_GROUP * (HEAD_DIM + 8) * ATT_TQ * 4
            + GQA_GROUP * ATT_TQ * 2 * HEAD_DIM * 2 + 2 * GQA_GROUP * ATT_TK * ATT_TQ * 4
            + cast_bytes)
    return pl.pallas_call(
        functools.partial(_attention_kernel, n_kv=n_kv),
        out_shape=(jax.ShapeDtypeStruct((n, ATTN_WIDTH), BF16), *cast_shapes),
        grid=(batch, ATTN_KV_HEADS, n_q),
        in_specs=[
            pl.BlockSpec((ATT_TQ, group_w), lambda b, g, i: (b * n_q + i, q_col0 + g)),
            pl.BlockSpec((seq, HEAD_DIM), lambda b, g, i: (b, k_col0 + g)),
            pl.BlockSpec((seq, HEAD_DIM), lambda b, g, i: (b, v_col0 + g)),
            *cast_specs,
        ],
        out_specs=(pl.BlockSpec((ATT_TQ, group_w), lambda b, g, i: (b * n_q + i, g)),
                   *cast_specs),
        scratch_shapes=[
            pltpu.VMEM((n_kv, HEAD_DIM, ATT_TK), BF16),
            pltpu.VMEM((8, HEAD_DIM), F32),
            pltpu.VMEM((GQA_GROUP, ATT_TQ, 2 * HEAD_DIM), BF16),
            pltpu.VMEM((GQA_GROUP, HEAD_DIM, ATT_TQ), F32),
            pltpu.VMEM((GQA_GROUP, 8, ATT_TQ), F32),
        ],
        compiler_params=pltpu.CompilerParams(
            dimension_semantics=("arbitrary", "arbitrary", "arbitrary"),
            vmem_limit_bytes=_vmem_limit(vmem)),
        name="attention",
    )(proj, proj, proj, w_up, w_down)


def _mem_kv_kernel(m_ref, nw_ref, wk_ref, wv_ref, o_ref):
    m = m_ref[...]
    h = (m * _rms_scale(m) * nw_ref[...]).astype(BF16)
    o_ref[:, :CROSS_WIDTH] = jnp.dot(h, wk_ref[...].astype(BF16),
                                     preferred_element_type=F32).astype(BF16)
    o_ref[:, CROSS_WIDTH:] = jnp.dot(h, wv_ref[...].astype(BF16),
                                     preferred_element_type=F32).astype(BF16)


def _mem_kv(mem2d, norm_w, w_k, w_v):
    rows = mem2d.shape[0]
    vmem = 2 * (rows * D_MODEL * 4 + 2 * D_MODEL * CROSS_WIDTH * 4 + rows * 2 * CROSS_WIDTH * 2)
    return pl.pallas_call(
        _mem_kv_kernel,
        out_shape=jax.ShapeDtypeStruct((rows, 2 * CROSS_WIDTH), BF16),
        grid=(1,),
        in_specs=[
            pl.BlockSpec((rows, D_MODEL), lambda i: (0, 0)),
            pl.BlockSpec((1, D_MODEL), lambda i: (0, 0)),
            pl.BlockSpec((D_MODEL, CROSS_WIDTH), lambda i: (0, 0)),
            pl.BlockSpec((D_MODEL, CROSS_WIDTH), lambda i: (0, 0)),
        ],
        out_specs=pl.BlockSpec((rows, 2 * CROSS_WIDTH), lambda i: (0, 0)),
        compiler_params=pltpu.CompilerParams(
            dimension_semantics=("arbitrary",),
            vmem_limit_bytes=_vmem_limit(vmem)),
        name="mem_kv",
    )(mem2d, norm_w, w_k, w_v)


def _out_cross_kernel(x_ref, yr_ref, ya_ref, wo_ref, nw_ref, wq_ref, kv_ref, wco_ref, o_ref):
    contract_cols = (((1,), (1,)), ((), ()))
    scale = CROSS_HEAD_DIM ** -0.5

    for r in range(OC_TM // OC_RC):
        rows = slice(r * OC_RC, (r + 1) * OC_RC)
        x1 = (x_ref[rows, :]
              + jnp.dot(yr_ref[rows, :], wo_ref[0:RET_WIDTH, :], preferred_element_type=F32)
              + jnp.dot(ya_ref[rows, :], wo_ref[RET_WIDTH:, :], preferred_element_type=F32))
        h = (x1 * _rms_scale(x1) * nw_ref[...]).astype(BF16)
        q = (jnp.dot(h, wq_ref[...], preferred_element_type=F32) * scale).astype(BF16)
        heads = []
        for hd in range(CROSS_HEADS):
            cols = slice(hd * CROSS_HEAD_DIM, (hd + 1) * CROSS_HEAD_DIM)
            k = kv_ref[:, cols]
            v = kv_ref[:, CROSS_WIDTH + hd * CROSS_HEAD_DIM:CROSS_WIDTH + (hd + 1) * CROSS_HEAD_DIM]
            s = lax.dot_general(q[:, cols], k, contract_cols, preferred_element_type=F32)
            e = jnp.exp(s - jnp.max(s, axis=-1, keepdims=True))
            inv = 1.0 / jnp.sum(e, axis=-1, keepdims=True)
            pv = jnp.dot(e.astype(BF16), v, preferred_element_type=F32)
            heads.append((pv * inv).astype(BF16))
        o = jnp.concatenate(heads, axis=1)
        o_ref[rows, :] = x1 + jnp.dot(o, wco_ref[...], preferred_element_type=F32)


def _out_cross(x2d, y_ret, y_attn, w_out, norm_w, w_q, mem_kv, w_co, seq):
    n = x2d.shape[0]
    assert n % OC_TM == 0 and seq % OC_TM == 0
    tiles_per_seq = seq // OC_TM
    vmem = (2 * 2 * OC_TM * D_MODEL * 4 + 2 * 2 * OC_TM * RET_WIDTH * 2 + 2 * D_MODEL * D_MODEL * 2
            + 2 * 2 * D_MODEL * CROSS_WIDTH * 2 + 2 * MEM_TOKENS * 2 * CROSS_WIDTH * 2)
    return pl.pallas_call(
        _out_cross_kernel,
        out_shape=jax.ShapeDtypeStruct((n, D_MODEL), F32),
        grid=(n // OC_TM,),
        in_specs=[
            pl.BlockSpec((OC_TM, D_MODEL), lambda i: (i, 0)),
            pl.BlockSpec((OC_TM, RET_WIDTH), lambda i: (i, 0)),
            pl.BlockSpec((OC_TM, ATTN_WIDTH), lambda i: (i, 0)),
            pl.BlockSpec((D_MODEL, D_MODEL), lambda i: (0, 0)),
            pl.BlockSpec((1, D_MODEL), lambda i: (0, 0)),
            pl.BlockSpec((D_MODEL, CROSS_WIDTH), lambda i: (0, 0)),
            pl.BlockSpec((MEM_TOKENS, 2 * CROSS_WIDTH), lambda i: (i // tiles_per_seq, 0)),
            pl.BlockSpec((CROSS_WIDTH, D_MODEL), lambda i: (0, 0)),
        ],
        out_specs=pl.BlockSpec((OC_TM, D_MODEL), lambda i: (i, 0)),
        compiler_params=pltpu.CompilerParams(
            dimension_semantics=("arbitrary",),
            vmem_limit_bytes=_vmem_limit(vmem)),
        name="out_cross",
    )(x2d, y_ret, y_attn, w_out, norm_w, w_q, mem_kv, w_co)


def _mlp_kernel(x_ref, nw_ref, wu_ref, wd_ref, fw_ref, o_ref, h_ref):
    j = pl.program_id(1)
    last = pl.num_programs(1) - 1

    def step(rows, is_first, is_last):
        if is_first:
            xr = x_ref[rows, :]
            h = (xr * _rms_scale(xr) * nw_ref[...]).astype(BF16)
            h_ref[rows, :] = h
            base = xr
        else:
            h = h_ref[rows, :]
            base = o_ref[rows, :]
        u = jnp.maximum(jnp.dot(h, wu_ref[...], preferred_element_type=F32), 0.0)
        y = base + jnp.dot((u * u).astype(BF16), wd_ref[...], preferred_element_type=F32)
        if is_last:
            y = y * _rms_scale(y) * fw_ref[...]
        o_ref[rows, :] = y

    edge_chunks = [slice(r * MLP_EDGE_RC, (r + 1) * MLP_EDGE_RC)
                   for r in range(MLP_TM // MLP_EDGE_RC)]

    @pl.when(j == 0)
    def _():
        for rows in edge_chunks:
            step(rows, True, False)

    @pl.when((j > 0) & (j < last))
    def _():
        step(slice(None), False, False)

    @pl.when(j == last)
    def _():
        for rows in edge_chunks:
            step(rows, False, True)


def _mlp(x2d, norm_w, w_up, w_down, final_w):
    n = x2d.shape[0]
    assert n % MLP_TM == 0 and D_FF % MLP_TF == 0
    vmem = (2 * 2 * MLP_TM * D_MODEL * 4 + MLP_TM * D_MODEL * 2 + 2 * 2 * D_MODEL * MLP_TF * 2)
    return pl.pallas_call(
        _mlp_kernel,
        out_shape=jax.ShapeDtypeStruct((n, D_MODEL), F32),
        grid=(n // MLP_TM, D_FF // MLP_TF),
        in_specs=[
            pl.BlockSpec((MLP_TM, D_MODEL), lambda i, j: (i, 0)),
            pl.BlockSpec((1, D_MODEL), lambda i, j: (0, 0)),
            pl.BlockSpec((D_MODEL, MLP_TF), lambda i, j: (0, j)),
            pl.BlockSpec((MLP_TF, D_MODEL), lambda i, j: (j, 0)),
            pl.BlockSpec((1, D_MODEL), lambda i, j: (0, 0)),
        ],
        out_specs=pl.BlockSpec((MLP_TM, D_MODEL), lambda i, j: (i, 0)),
        scratch_shapes=[pltpu.VMEM((MLP_TM, D_MODEL), BF16)],
        compiler_params=pltpu.CompilerParams(
            dimension_semantics=("arbitrary", "arbitrary"),
            vmem_limit_bytes=_vmem_limit(vmem)),
        name="mlp",
    )(x2d, norm_w, w_up, w_down, final_w)


def _rope_tables(seq):
    t = np.arange(seq)
    row = (t // GRID_W).astype(np.float64)
    col = (t % GRID_W).astype(np.float64)
    inv_freq = 1.0 / (ROPE_THETA ** (np.arange(0, AXIS_DIM, 2, dtype=np.float64) / AXIS_DIM))
    ang_r = row[:, None] * inv_freq[None, :]
    ang_c = col[:, None] * inv_freq[None, :]
    cos_t = np.concatenate([np.cos(ang_r), np.cos(ang_c), np.cos(ang_r), np.cos(ang_c)], axis=-1)
    sin_t = np.concatenate([-np.sin(ang_r), -np.sin(ang_c), np.sin(ang_r), np.sin(ang_c)], axis=-1)
    return jnp.asarray(cos_t, F32), jnp.asarray(sin_t, F32)


def _pair_heads(w):
    lead = w.shape[:-1]
    heads = w.shape[-1] // HEAD_DIM
    quarter = HEAD_DIM // 4
    w = w.reshape(lead + (heads, 2, 2, quarter))
    return jnp.swapaxes(w, -2, -3).reshape(lead + (heads * HEAD_DIM,))


def _w_in_prep_kernel(w_ref, o_ref):
    lane = lax.broadcasted_iota(jnp.int32, (WPREP_ROWS, HEAD_DIM), 1)
    quarter = HEAD_DIM // 4
    takes_c1 = (lane >= quarter) & (lane < 2 * quarter)
    takes_r2 = (lane >= 2 * quarter) & (lane < 3 * quarter)
    rope_slabs = (set(range(0, 2 * RET_HEADS))
                  | set(range(4 * RET_HEADS, 4 * RET_HEADS + ATTN_HEADS + ATTN_KV_HEADS)))
    for s in range(IN_WIDTH // HEAD_DIM):
        cols = slice(s * HEAD_DIM, (s + 1) * HEAD_DIM)
        y = w_ref[:, cols]
        if s in rope_slabs:
            y = jnp.where(takes_c1, pltpu.roll(y, HEAD_DIM - quarter, 1),
                          jnp.where(takes_r2, pltpu.roll(y, quarter, 1), y))
        o_ref[:, cols] = y.astype(BF16)


def _in_proj_weights(w_in):
    rows = w_in.shape[0]
    assert rows % WPREP_ROWS == 0
    vmem = 2 * WPREP_ROWS * IN_WIDTH * (4 + 2)
    return pl.pallas_call(
        _w_in_prep_kernel,
        out_shape=jax.ShapeDtypeStruct((rows, IN_WIDTH), BF16),
        grid=(rows // WPREP_ROWS,),
        in_specs=[pl.BlockSpec((WPREP_ROWS, IN_WIDTH), lambda i: (i, 0))],
        out_specs=pl.BlockSpec((WPREP_ROWS, IN_WIDTH), lambda i: (i, 0)),
        compiler_params=pltpu.CompilerParams(
            dimension_semantics=("arbitrary",),
            vmem_limit_bytes=_vmem_limit(vmem)),
        name="w_in_prep",
    )(w_in)


def kernel(x, mem, norm_mix_w, w_in, ret_decay_fwd, ret_decay_bwd, ret_gn_w, ret_gn_b, attn_q_norm_w, attn_k_norm_w, w_out, norm_cross_w, norm_mem_w, w_cross_q, w_cross_k, w_cross_v, w_cross_o, norm_mlp_w, w_mlp_up, w_mlp_down, norm_final_w):
    batch, seq, _ = x.shape
    assert w_in.shape[0] == 1, "single-layer block: per-layer parameters have a leading axis of 1"
    cos_t, sin_t = _rope_tables(seq)
    xs = x.reshape(batch * seq, D_MODEL)
    mem2d = mem.reshape(batch * MEM_TOKENS, D_MODEL)
    dec = jnp.stack([ret_decay_fwd[0], ret_decay_bwd[0]], axis=1)
    dec = jnp.broadcast_to(dec[:, :, None], (RET_HEADS, 2, RET_C)).astype(F32)
    proj = _in_proj(xs, norm_mix_w, _in_proj_weights(w_in[0]), cos_t, sin_t,
                    _pair_heads(attn_q_norm_w), _pair_heads(attn_k_norm_w), seq)
    y_ret, (w_out_bf16, w_cq_bf16, w_co_bf16) = _retention(
        proj, dec, ret_gn_w, ret_gn_b, (w_out[0], w_cross_q[0], w_cross_o[0]), batch, seq)
    y_attn, w_up_bf16, w_down_bf16 = _attention(proj, w_mlp_up[0], w_mlp_down[0], batch, seq)
    mem_kv = _mem_kv(mem2d, norm_mem_w, w_cross_k[0], w_cross_v[0])
    xs = _out_cross(xs, y_ret, y_attn, w_out_bf16, norm_cross_w, w_cq_bf16, mem_kv, w_co_bf16, seq)
    xs = _mlp(xs, norm_mlp_w, w_up_bf16, w_down_bf16, norm_final_w[None, :])
    return xs.reshape(batch, seq, D_MODEL)
```

```python
import functools
import math

import numpy as np
import jax
import jax.numpy as jnp
from jax import lax
from jax.experimental import pallas as pl
from jax.experimental.pallas import tpu as pltpu

D_MODEL = 2048
HEAD_DIM = 128
RET_WIDTH = 1024
ATTN_WIDTH = 1024
RET_HEADS = 8
ATTN_HEADS = 8
ATTN_KV_HEADS = 2
GQA_GROUP = ATTN_HEADS // ATTN_KV_HEADS
KV_WIDTH = ATTN_KV_HEADS * HEAD_DIM
IN_WIDTH = 4 * RET_WIDTH + ATTN_WIDTH + 2 * KV_WIDTH
GRID_W = 64
AXIS_DIM = HEAD_DIM // 2
ROPE_THETA = 10000.0
MEM_TOKENS = 256
CROSS_HEADS = 4
CROSS_HEAD_DIM = 128
CROSS_WIDTH = CROSS_HEADS * CROSS_HEAD_DIM
D_FF = 4 * D_MODEL
NORM_EPS = 1e-6

V7X_LANES = 128
V7X_BF16_SUBLANES = 16
V7X_VMEM_BYTES = 64 * 1024 * 1024

F32 = jnp.float32
BF16 = jnp.bfloat16
LOG2E = math.log2(math.e)
NEG_BIG = -1e30
TINY = 1e-30

IN_TM = 512
IN_TN = 512
IN_NORM_RC = 256
WPREP_ROWS = 256
RET_C = 256
RET_TS = 2048
RET_HG = 4
ATT_TQ = 512
ATT_TK = 4096
ATT_DENOM_FLOOR = 2.0 ** -40
OC_TM = 512
OC_RC = 512
MLP_TM = 1024
MLP_TF = 1024
MLP_EDGE_RC = 512


def _vmem_limit(nbytes):
    return int(min(nbytes + 16 * 1024 * 1024, V7X_VMEM_BYTES - 8 * 1024 * 1024))


def _rms_scale(y):
    return lax.rsqrt(jnp.mean(y * y, axis=-1, keepdims=True) + NORM_EPS)


def _slab_cast_specs(weights, steps, flat_step):
    specs, shapes, nbytes = [], [], 0
    for w in weights:
        assert w.shape[0] % steps == 0 and (w.shape[0] // steps) % V7X_BF16_SUBLANES == 0
        block = (w.shape[0] // steps, w.shape[1])
        specs.append(pl.BlockSpec(block, lambda *ids: (flat_step(*ids), 0)))
        shapes.append(jax.ShapeDtypeStruct(w.shape, BF16))
        nbytes += 2 * block[0] * block[1] * (4 + 2)
    return specs, shapes, nbytes


def _cast_slabs(src_refs, dst_refs):
    for src, dst in zip(src_refs, dst_refs):
        dst[...] = src[...].astype(BF16)


def _rope(y, cos, sin_signed):
    return y * cos + pltpu.roll(y, HEAD_DIM // 2, 1) * sin_signed


def _in_proj_kernel(x_ref, nw_ref, w_ref, cos_ref, sin_ref, qg_ref, kg_ref, o_ref, h_ref):
    for r in range(IN_TM // IN_NORM_RC):
        rows = slice(r * IN_NORM_RC, (r + 1) * IN_NORM_RC)
        xr = x_ref[rows, :]
        h_ref[rows, :] = (xr * _rms_scale(xr) * nw_ref[...]).astype(BF16)

    ret_k_scale = HEAD_DIM ** -0.5
    attn_q_scale = (HEAD_DIM ** -0.5) * LOG2E

    n_slabs = IN_TN // HEAD_DIM

    def run(j, epilogue):
        tile = slice(j * IN_TN, (j + 1) * IN_TN)
        acc = jnp.dot(h_ref[...], w_ref[:, tile], preferred_element_type=F32)
        slabs = epilogue([acc[:, s * HEAD_DIM:(s + 1) * HEAD_DIM] for s in range(n_slabs)],
                         cos_ref[...], sin_ref[...])
        for s, y in enumerate(slabs):
            c0 = j * IN_TN + s * HEAD_DIM
            o_ref[:, c0:c0 + HEAD_DIM] = y.astype(BF16)

    def ret_q(ys, cos, sin):
        return [_rope(y, cos, sin) for y in ys]

    def ret_k(ys, cos, sin):
        return [_rope(y, cos, sin) * ret_k_scale for y in ys]

    def plain(ys, cos, sin):
        return ys

    def head_inv_rms(y_a, y_b):
        width = 2 * HEAD_DIM
        r = lax.broadcasted_iota(jnp.int32, (width, width), 0) // HEAD_DIM
        c = lax.broadcasted_iota(jnp.int32, (width, width), 1) // HEAD_DIM
        ones_bd = jnp.where(r == c, 1.0, 0.0).astype(BF16)
        sq = jnp.concatenate([(y_a * y_a).astype(BF16), (y_b * y_b).astype(BF16)], axis=1)
        ms = jnp.dot(sq, ones_bd, preferred_element_type=F32) * (1.0 / HEAD_DIM)
        inv = lax.rsqrt(ms + NORM_EPS)
        return inv[:, :HEAD_DIM], inv[:, HEAD_DIM:]

    def normed_rope(y_a, y_b, gain, scale, cos, sin):
        inv_a, inv_b = head_inv_rms(y_a, y_b)
        return [_rope(y_a * gain, cos, sin) * (inv_a * scale),
                _rope(y_b * gain, cos, sin) * (inv_b * scale)]

    def attn_q(ys, cos, sin):
        gain = qg_ref[...]
        return (normed_rope(ys[0], ys[1], gain, attn_q_scale, cos, sin)
                + normed_rope(ys[2], ys[3], gain, attn_q_scale, cos, sin))

    def attn_kv(ys, cos, sin):
        return normed_rope(ys[0], ys[1], kg_ref[...], 1.0, cos, sin) + ys[ATTN_KV_HEADS:]

    nb = RET_WIDTH // IN_TN
    for j in range(IN_WIDTH // IN_TN):
        if j < nb:
            run(j, ret_q)
        elif j < 2 * nb:
            run(j, ret_k)
        elif j < 4 * nb:
            run(j, plain)
        elif j < 5 * nb:
            run(j, attn_q)
        else:
            run(j, attn_kv)


def _in_proj(x2d, norm_w, w_in, cos_t, sin_t, q_gain, k_gain, seq):
    n = x2d.shape[0]
    assert IN_WIDTH % IN_TN == 0 and 2 * KV_WIDTH == IN_TN and n % IN_TM == 0 and seq % IN_TM == 0
    tiles_per_seq = seq // IN_TM
    vmem = (2 * IN_TM * D_MODEL * 4 + IN_TM * D_MODEL * 2 + D_MODEL * IN_WIDTH * 2
            + 2 * IN_TM * IN_WIDTH * 2 + 4 * IN_TM * HEAD_DIM * 4)
    resident = pl.Buffered(1)
    return pl.pallas_call(
        _in_proj_kernel,
        out_shape=jax.ShapeDtypeStruct((n, IN_WIDTH), BF16),
        grid=(n // IN_TM,),
        in_specs=[
            pl.BlockSpec((IN_TM, D_MODEL), lambda i: (i, 0)),
            pl.BlockSpec((1, D_MODEL), lambda i: (0, 0)),
            pl.BlockSpec((D_MODEL, IN_WIDTH), lambda i: (0, 0), pipeline_mode=resident),
            pl.BlockSpec((IN_TM, HEAD_DIM), lambda i: (i % tiles_per_seq, 0)),
            pl.BlockSpec((IN_TM, HEAD_DIM), lambda i: (i % tiles_per_seq, 0)),
            pl.BlockSpec((1, HEAD_DIM), lambda i: (0, 0)),
            pl.BlockSpec((1, HEAD_DIM), lambda i: (0, 0)),
        ],
        out_specs=pl.BlockSpec((IN_TM, IN_WIDTH), lambda i: (i, 0)),
        scratch_shapes=[pltpu.VMEM((IN_TM, D_MODEL), BF16)],
        compiler_params=pltpu.CompilerParams(
            dimension_semantics=("arbitrary",),
            vmem_limit_bytes=_vmem_limit(vmem)),
        name="in_proj",
    )(x2d, norm_w, w_in, cos_t, sin_t, q_gain, k_gain)


def _retention_kernel(dec_ref, q_ref, k_ref, v_ref, g_ref, gw_ref, gb_ref,
                      wout_ref, wq_ref, wco_ref, o_ref, wout_o_ref, wq_o_ref, wco_o_ref,
                      d_ref, tab_ref, sb_ref, st_ref, *, n_seq):
    _cast_slabs((wout_ref, wq_ref, wco_ref), (wout_o_ref, wq_o_ref, wco_o_ref))
    s = pl.program_id(2)
    c_len = RET_C
    cps = RET_TS // RET_C
    contract_rows = (((0,), (0,)), ((), ()))
    contract_cols = (((1,), (1,)), ((), ()))
    head_cols = [slice(h * HEAD_DIM, (h + 1) * HEAD_DIM) for h in range(RET_HG)]
    chunk_rows = [slice(c * c_len, (c + 1) * c_len) for c in range(cps)]

    def log_gammas(h):
        return -jnp.exp(dec_ref[h, 0:1, :]), -jnp.exp(dec_ref[h, 1:2, :])

    @pl.when(s == 0)
    def _():
        row = lax.broadcasted_iota(jnp.int32, (c_len, c_len), 0)
        col = lax.broadcasted_iota(jnp.int32, (c_len, c_len), 1)
        rel = (row - col).astype(F32)
        idx = lax.broadcasted_iota(jnp.int32, (c_len, HEAD_DIM), 0).astype(F32)
        for h in range(RET_HG):
            lg_f, lg_b = log_gammas(h)
            d_ref[h] = jnp.exp(jnp.where(rel >= 0, lg_f * rel, -lg_b * rel))
            lf = lg_f[:, :HEAD_DIM]
            lb = lg_b[:, :HEAD_DIM]
            tab_ref[h, 0] = jnp.exp(lf * (idx + 1.0))
            tab_ref[h, 1] = jnp.exp(lb * (c_len - idx))
            tab_ref[h, 2] = jnp.exp(lf * (c_len - 1.0 - idx))
            tab_ref[h, 3] = jnp.exp(lb * idx)
        st_ref[...] = jnp.zeros(st_ref.shape, F32)

    @pl.when(s < n_seq)
    def _():
        first_chunk = (n_seq - 1 - s) * cps
        for h in range(RET_HG):
            chunk_b = jnp.exp(log_gammas(h)[1][:, :HEAD_DIM] * c_len)
            state = st_ref[h]
            for c in reversed(range(cps)):
                sb_ref[first_chunk + c, h] = state.astype(BF16)
                k = k_ref[chunk_rows[c], head_cols[h]]
                kd = (k.astype(F32) * tab_ref[h, 3]).astype(BF16)
                upd = lax.dot_general(kd, v_ref[chunk_rows[c], head_cols[h]], contract_rows,
                                      preferred_element_type=F32)
                state = state * chunk_b + upd
            st_ref[h] = state

        @pl.when(s == n_seq - 1)
        def _():
            st_ref[...] = jnp.zeros(st_ref.shape, F32)

    @pl.when(s >= n_seq)
    def _():
        first_chunk = (s - n_seq) * cps
        for h in range(RET_HG):
            chunk_f = jnp.exp(log_gammas(h)[0][:, :HEAD_DIM] * c_len)
            gw = gw_ref[:, head_cols[h]]
            gb = gb_ref[:, head_cols[h]]
            state = st_ref[h]
            for c in range(cps):
                q = q_ref[chunk_rows[c], head_cols[h]]
                k = k_ref[chunk_rows[c], head_cols[h]]
                v = v_ref[chunk_rows[c], head_cols[h]]
                scores = lax.dot_general(q, k, contract_cols, preferred_element_type=F32)
                masked = (scores * d_ref[h]).astype(BF16)
                qf = q.astype(F32)
                lhs = jnp.concatenate(
                    [masked, (qf * tab_ref[h, 0]).astype(BF16), (qf * tab_ref[h, 1]).astype(BF16)],
                    axis=1)
                rhs = jnp.concatenate([v, state.astype(BF16), sb_ref[first_chunk + c, h]], axis=0)
                out = jnp.dot(lhs, rhs, preferred_element_type=F32)
                mu = jnp.mean(out, axis=-1, keepdims=True)
                cen = out - mu
                var = jnp.mean(cen * cen, axis=-1, keepdims=True)
                y = cen * lax.rsqrt(var + NORM_EPS) * gw + gb
                g = g_ref[chunk_rows[c], head_cols[h]].astype(F32)
                y = y * (g / (1.0 + jnp.exp(-g)))
                o_ref[chunk_rows[c], head_cols[h]] = y.astype(BF16)
                kd = (k.astype(F32) * tab_ref[h, 2]).astype(BF16)
                upd = lax.dot_general(kd, v, contract_rows, preferred_element_type=F32)
                state = state * chunk_f + upd
            st_ref[h] = state


def _retention(proj, dec, gn_w, gn_b, cross_weights, batch, seq):
    n = proj.shape[0]
    assert seq % RET_TS == 0 and RET_TS % RET_C == 0 and RET_HEADS % RET_HG == 0
    n_seq = seq // RET_TS
    n_chunks = seq // RET_C
    group_w = RET_HG * HEAD_DIM
    groups = RET_HEADS // RET_HG

    def kv_block(section):
        def index(b, hg, s):
            blk = jnp.where(s < n_seq, n_seq - 1 - s, s - n_seq)
            return (b * n_seq + blk, section * groups + hg)
        return index

    def fwd_block(section):
        return lambda b, hg, s: (b * n_seq + jnp.maximum(s - n_seq, 0), section * groups + hg)

    cast_specs, cast_shapes, cast_bytes = _slab_cast_specs(
        cross_weights, batch * groups * 2 * n_seq,
        lambda b, hg, s: (b * groups + hg) * 2 * n_seq + s)
    vmem = (2 * 5 * RET_TS * group_w * 2 + RET_HG * RET_C * RET_C * 4
            + RET_HG * 4 * RET_C * HEAD_DIM * 4 + n_chunks * RET_HG * HEAD_DIM * HEAD_DIM * 2
            + RET_HG * HEAD_DIM * HEAD_DIM * 4 + cast_bytes)
    y_ret, *cast = pl.pallas_call(
        functools.partial(_retention_kernel, n_seq=n_seq),
        out_shape=(jax.ShapeDtypeStruct((n, RET_WIDTH), BF16), *cast_shapes),
        grid=(batch, groups, 2 * n_seq),
        in_specs=[
            pl.BlockSpec((RET_HG, 2, RET_C), lambda b, hg, s: (hg, 0, 0)),
            pl.BlockSpec((RET_TS, group_w), fwd_block(0)),
            pl.BlockSpec((RET_TS, group_w), kv_block(1)),
            pl.BlockSpec((RET_TS, group_w), kv_block(2)),
            pl.BlockSpec((RET_TS, group_w), fwd_block(3)),
            pl.BlockSpec((1, group_w), lambda b, hg, s: (0, hg)),
            pl.BlockSpec((1, group_w), lambda b, hg, s: (0, hg)),
            *cast_specs,
        ],
        out_specs=(pl.BlockSpec((RET_TS, group_w), fwd_block(0)), *cast_specs),
        scratch_shapes=[
            pltpu.VMEM((RET_HG, RET_C, RET_C), F32),
            pltpu.VMEM((RET_HG, 4, RET_C, HEAD_DIM), F32),
            pltpu.VMEM((n_chunks, RET_HG, HEAD_DIM, HEAD_DIM), BF16),
            pltpu.VMEM((RET_HG, HEAD_DIM, HEAD_DIM), F32),
        ],
        compiler_params=pltpu.CompilerParams(
            dimension_semantics=("arbitrary", "arbitrary", "arbitrary"),
            vmem_limit_bytes=_vmem_limit(vmem)),
        name="retention",
    )(dec, proj, proj, proj, proj, gn_w, gn_b, *cross_weights)
    return y_ret, cast


def _attention_kernel(q_ref, k_ref, v_ref, wu_ref, wd_ref, o_ref, wu_o_ref, wd_o_ref,
                      vt_ref, k2max_ref, qa_ref, acc_ref, den_ref, *, n_kv):
    _cast_slabs((wu_ref, wd_ref), (wu_o_ref, wd_o_ref))

    i = pl.program_id(2)
    contract_cols = (((1,), (1,)), ((), ()))

    def kv_rows(c):
        return pl.ds(pl.multiple_of(c * ATT_TK, ATT_TK), ATT_TK)

    @pl.when(i == 0)
    def _():
        def body(c, k2max):
            rows = kv_rows(c)
            vt_ref[c] = v_ref[rows, :].astype(F32).T.astype(BF16)
            kc = k_ref[rows, :].astype(F32)
            k2 = jnp.max(jnp.sum(kc * kc, axis=1, keepdims=True), axis=0, keepdims=True)
            return jnp.maximum(k2max, k2)
        k2max = lax.fori_loop(0, n_kv, body, jnp.zeros((1, 1), F32))
        k2max_ref[...] = jnp.broadcast_to(k2max, k2max_ref.shape)

    q_lane = lax.broadcasted_iota(jnp.int32, (ATT_TQ, HEAD_DIM), 1)
    k_lane = lax.broadcasted_iota(jnp.int32, (ATT_TK, HEAD_DIM), 1)
    k_one = jnp.where(k_lane == 0, 1.0, 0.0).astype(BF16)
    head_cols = [slice(h * HEAD_DIM, (h + 1) * HEAD_DIM) for h in range(GQA_GROUP)]

    def set_shift(h, shift):
        qa_ref[h, :, HEAD_DIM:] = jnp.where(q_lane == 0, -shift, 0.0).astype(BF16)

    def sweep():
        acc_ref[...] = jnp.zeros(acc_ref.shape, F32)
        den_ref[...] = jnp.zeros(den_ref.shape, F32)

        def kv_step(c, carry):
            k_aug = jnp.concatenate([k_ref[kv_rows(c), :], k_one], axis=1)
            vt = vt_ref[c]
            for h in range(GQA_GROUP):
                s_t = lax.dot_general(k_aug, qa_ref[h], contract_cols,
                                      preferred_element_type=F32)
                p_t = jnp.exp2(s_t)
                den_ref[h] += jnp.sum(p_t.reshape(ATT_TK // 8, 8, ATT_TQ), axis=0)
                acc_ref[h] += jnp.dot(vt, p_t.astype(BF16), preferred_element_type=F32)
            return carry
        lax.fori_loop(0, n_kv, kv_step, 0)

    ones = jnp.ones((HEAD_DIM, HEAD_DIM), BF16)
    for h in range(GQA_GROUP):
        qh = q_ref[:, head_cols[h]]
        qa_ref[h, :, 0:HEAD_DIM] = qh
        qf = qh.astype(F32)
        q2 = jnp.dot((qf * qf).astype(BF16), ones, preferred_element_type=F32)
        bound2 = q2 * k2max_ref[0:1, :]
        set_shift(h, bound2 * lax.rsqrt(bound2 + TINY))
    sweep()

    def denominator(h):
        return jnp.sum(den_ref[h], axis=0, keepdims=True)

    denom_min = functools.reduce(jnp.minimum,
                                 [jnp.min(denominator(h)) for h in range(GQA_GROUP)])

    @pl.when(denom_min < ATT_DENOM_FLOOR)
    def _():
        for h in range(GQA_GROUP):
            def body(c, m):
                s = lax.dot_general(q_ref[:, head_cols[h]], k_ref[kv_rows(c), :], contract_cols,
                                    preferred_element_type=F32)
                return jnp.maximum(m, jnp.max(s, axis=1, keepdims=True))
            set_shift(h, lax.fori_loop(0, n_kv, body, jnp.full((ATT_TQ, 1), NEG_BIG, F32)))
        sweep()

    for h in range(GQA_GROUP):
        o_t = acc_ref[h] * (1.0 / denominator(h))
        o_ref[:, head_cols[h]] = o_t.T.astype(BF16)


def _attention(proj, w_up, w_down, batch, seq):
    n = proj.shape[0]
    assert seq % ATT_TQ == 0 and seq % ATT_TK == 0
    n_q = seq // ATT_TQ
    n_kv = seq // ATT_TK
    group_w = GQA_GROUP * HEAD_DIM
    q_col0 = 4 * RET_WIDTH // group_w
    k_col0 = (4 * RET_WIDTH + ATTN_WIDTH) // HEAD_DIM
    v_col0 = k_col0 + ATTN_KV_HEADS
    cast_specs, cast_shapes, cast_bytes = _slab_cast_specs(
        (w_up, w_down), batch * ATTN_KV_HEADS * n_q,
        lambda b, g, i: (b * ATTN_KV_HEADS + g) * n_q + i)
    vmem = (2 * ATT_TQ * group_w * 2 * 2 + 2 * 2 * seq * HEAD_DIM * 2
            + HEAD_DIM * seq * 2 + GQA_GROUP * (HEAD_DIM + 8) * ATT_TQ * 4
            + GQA_GROUP * ATT_TQ * 2 * HEAD_DIM * 2 + 2 * GQA_GROUP * ATT_TK * ATT_TQ * 4
            + cast_bytes)
    return pl.pallas_call(
        functools.partial(_attention_kernel, n_kv=n_kv),
        out_shape=(jax.ShapeDtypeStruct((n, ATTN_WIDTH), BF16), *cast_shapes),
        grid=(batch, ATTN_KV_HEADS, n_q),
        in_specs=[
            pl.BlockSpec((ATT_TQ, group_w), lambda b, g, i: (b * n_q + i, q_col0 + g)),
            pl.BlockSpec((seq, HEAD_DIM), lambda b, g, i: (b, k_col0 + g)),
            pl.BlockSpec((seq, HEAD_DIM), lambda b, g, i: (b, v_col0 + g)),
            *cast_specs,
        ],
        out_specs=(pl.BlockSpec((ATT_TQ, group_w), lambda b, g, i: (b * n_q + i, g)),
                   *cast_specs),
        scratch_shapes=[
            pltpu.VMEM((n_kv, HEAD_DIM, ATT_TK), BF16),
            pltpu.VMEM((8, HEAD_DIM), F32),
            pltpu.VMEM((GQA_GROUP, ATT_TQ, 2 * HEAD_DIM), BF16),
            pltpu.VMEM((GQA_GROUP, HEAD_DIM, ATT_TQ), F32),
            pltpu.VMEM((GQA_GROUP, 8, ATT_TQ), F32),
        ],
        compiler_params=pltpu.CompilerParams(
            dimension_semantics=("arbitrary", "arbitrary", "arbitrary"),
            vmem_limit_bytes=_vmem_limit(vmem)),
        name="attention",
    )(proj, proj, proj, w_up, w_down)


def _mem_kv_kernel(m_ref, nw_ref, wk_ref, wv_ref, o_ref):
    m = m_ref[...]
    h = (m * _rms_scale(m) * nw_ref[...]).astype(BF16)
    o_ref[:, :CROSS_WIDTH] = jnp.dot(h, wk_ref[...].astype(BF16),
                                     preferred_element_type=F32).astype(BF16)
    o_ref[:, CROSS_WIDTH:] = jnp.dot(h, wv_ref[...].astype(BF16),
                                     preferred_element_type=F32).astype(BF16)


def _mem_kv(mem2d, norm_w, w_k, w_v):
    rows = mem2d.shape[0]
    vmem = 2 * (rows * D_MODEL * 4 + 2 * D_MODEL * CROSS_WIDTH * 4 + rows * 2 * CROSS_WIDTH * 2)
    return pl.pallas_call(
        _mem_kv_kernel,
        out_shape=jax.ShapeDtypeStruct((rows, 2 * CROSS_WIDTH), BF16),
        grid=(1,),
        in_specs=[
            pl.BlockSpec((rows, D_MODEL), lambda i: (0, 0)),
            pl.BlockSpec((1, D_MODEL), lambda i: (0, 0)),
            pl.BlockSpec((D_MODEL, CROSS_WIDTH), lambda i: (0, 0)),
            pl.BlockSpec((D_MODEL, CROSS_WIDTH), lambda i: (0, 0)),
        ],
        out_specs=pl.BlockSpec((rows, 2 * CROSS_WIDTH), lambda i: (0, 0)),
        compiler_params=pltpu.CompilerParams(
            dimension_semantics=("arbitrary",),
            vmem_limit_bytes=_vmem_limit(vmem)),
        name="mem_kv",
    )(mem2d, norm_w, w_k, w_v)


def _out_cross_kernel(x_ref, yr_ref, ya_ref, wo_ref, nw_ref, wq_ref, kv_ref, wco_ref, o_ref):
    contract_cols = (((1,), (1,)), ((), ()))
    scale = CROSS_HEAD_DIM ** -0.5

    for r in range(OC_TM // OC_RC):
        rows = slice(r * OC_RC, (r + 1) * OC_RC)
        x1 = (x_ref[rows, :]
              + jnp.dot(yr_ref[rows, :], wo_ref[0:RET_WIDTH, :], preferred_element_type=F32)
              + jnp.dot(ya_ref[rows, :], wo_ref[RET_WIDTH:, :], preferred_element_type=F32))
        h = (x1 * _rms_scale(x1) * nw_ref[...]).astype(BF16)
        q = (jnp.dot(h, wq_ref[...], preferred_element_type=F32) * scale).astype(BF16)
        heads = []
        for hd in range(CROSS_HEADS):
            cols = slice(hd * CROSS_HEAD_DIM, (hd + 1) * CROSS_HEAD_DIM)
            k = kv_ref[:, cols]
            v = kv_ref[:, CROSS_WIDTH + hd * CROSS_HEAD_DIM:CROSS_WIDTH + (hd + 1) * CROSS_HEAD_DIM]
            s = lax.dot_general(q[:, cols], k, contract_cols, preferred_element_type=F32)
            e = jnp.exp(s - jnp.max(s, axis=-1, keepdims=True))
            inv = 1.0 / jnp.sum(e, axis=-1, keepdims=True)
            pv = jnp.dot(e.astype(BF16), v, preferred_element_type=F32)
            heads.append((pv * inv).astype(BF16))
        o = jnp.concatenate(heads, axis=1)
        o_ref[rows, :] = x1 + jnp.dot(o, wco_ref[...], preferred_element_type=F32)


def _out_cross(x2d, y_ret, y_attn, w_out, norm_w, w_q, mem_kv, w_co, seq):
    n = x2d.shape[0]
    assert n % OC_TM == 0 and seq % OC_TM == 0
    tiles_per_seq = seq // OC_TM
    vmem = (2 * 2 * OC_TM * D_MODEL * 4 + 2 * 2 * OC_TM * RET_WIDTH * 2 + 2 * D_MODEL * D_MODEL * 2
            + 2 * 2 * D_MODEL * CROSS_WIDTH * 2 + 2 * MEM_TOKENS * 2 * CROSS_WIDTH * 2)
    return pl.pallas_call(
        _out_cross_kernel,
        out_shape=jax.ShapeDtypeStruct((n, D_MODEL), F32),
        grid=(n // OC_TM,),
        in_specs=[
            pl.BlockSpec((OC_TM, D_MODEL), lambda i: (i, 0)),
            pl.BlockSpec((OC_TM, RET_WIDTH), lambda i: (i, 0)),
            pl.BlockSpec((OC_TM, ATTN_WIDTH), lambda i: (i, 0)),
            pl.BlockSpec((D_MODEL, D_MODEL), lambda i: (0, 0)),
            pl.BlockSpec((1, D_MODEL), lambda i: (0, 0)),
            pl.BlockSpec((D_MODEL, CROSS_WIDTH), lambda i: (0, 0)),
            pl.BlockSpec((MEM_TOKENS, 2 * CROSS_WIDTH), lambda i: (i // tiles_per_seq, 0)),
            pl.BlockSpec((CROSS_WIDTH, D_MODEL), lambda i: (0, 0)),
        ],
        out_specs=pl.BlockSpec((OC_TM, D_MODEL), lambda i: (i, 0)),
        compiler_params=pltpu.CompilerParams(
            dimension_semantics=("arbitrary",),
            vmem_limit_bytes=_vmem_limit(vmem)),
        name="out_cross",
    )(x2d, y_ret, y_attn, w_out, norm_w, w_q, mem_kv, w_co)


def _mlp_kernel(x_hbm, nw_ref, wu_ref, wd_ref, fw_ref, o_ref, h_ref, x_ref, x_sem):
    i = pl.program_id(0)
    j = pl.program_id(1)
    last = pl.num_programs(1) - 1

    def x_copy(tile):
        rows = pl.ds(pl.multiple_of(tile * MLP_TM, MLP_TM), MLP_TM)
        return pltpu.make_async_copy(x_hbm.at[rows, :], x_ref, x_sem)

    @pl.when((i == 0) & (j == 0))
    def _():
        x_copy(0).start()

    @pl.when(j == 0)
    def _():
        x_copy(i).wait()

    @pl.when((j == 1) & (i + 1 < pl.num_programs(0)))
    def _():
        x_copy(i + 1).start()

    def step(rows, is_first, is_last):
        if is_first:
            xr = x_ref[rows, :]
            h = (xr * _rms_scale(xr) * nw_ref[...]).astype(BF16)
            h_ref[rows, :] = h
            base = xr
        else:
            h = h_ref[rows, :]
            base = o_ref[rows, :]
        u = jnp.maximum(jnp.dot(h, wu_ref[...], preferred_element_type=F32), 0.0)
        y = base + jnp.dot((u * u).astype(BF16), wd_ref[...], preferred_element_type=F32)
        if is_last:
            y = y * _rms_scale(y) * fw_ref[...]
        o_ref[rows, :] = y

    edge_chunks = [slice(r * MLP_EDGE_RC, (r + 1) * MLP_EDGE_RC)
                   for r in range(MLP_TM // MLP_EDGE_RC)]

    @pl.when(j == 0)
    def _():
        for rows in edge_chunks:
            step(rows, True, False)

    @pl.when((j > 0) & (j < last))
    def _():
        step(slice(None), False, False)

    @pl.when(j == last)
    def _():
        for rows in edge_chunks:
            step(rows, False, True)


def _mlp(x2d, norm_w, w_up, w_down, final_w):
    n = x2d.shape[0]
    assert n % MLP_TM == 0 and D_FF % MLP_TF == 0 and D_FF // MLP_TF >= 3
    vmem = (3 * MLP_TM * D_MODEL * 4 + MLP_TM * D_MODEL * 2 + 2 * 2 * D_MODEL * MLP_TF * 2)
    return pl.pallas_call(
        _mlp_kernel,
        out_shape=jax.ShapeDtypeStruct((n, D_MODEL), F32),
        grid=(n // MLP_TM, D_FF // MLP_TF),
        in_specs=[
            pl.BlockSpec(memory_space=pl.ANY),
            pl.BlockSpec((1, D_MODEL), lambda i, j: (0, 0)),
            pl.BlockSpec((D_MODEL, MLP_TF), lambda i, j: (0, j)),
            pl.BlockSpec((MLP_TF, D_MODEL), lambda i, j: (j, 0)),
            pl.BlockSpec((1, D_MODEL), lambda i, j: (0, 0)),
        ],
        out_specs=pl.BlockSpec((MLP_TM, D_MODEL), lambda i, j: (i, 0)),
        scratch_shapes=[
            pltpu.VMEM((MLP_TM, D_MODEL), BF16),
            pltpu.VMEM((MLP_TM, D_MODEL), F32),
            pltpu.SemaphoreType.DMA(()),
        ],
        compiler_params=pltpu.CompilerParams(
            dimension_semantics=("arbitrary", "arbitrary"),
            vmem_limit_bytes=_vmem_limit(vmem)),
        name="mlp",
    )(x2d, norm_w, w_up, w_down, final_w)


def _rope_tables(seq):
    t = np.arange(seq)
    row = (t // GRID_W).astype(np.float64)
    col = (t % GRID_W).astype(np.float64)
    inv_freq = 1.0 / (ROPE_THETA ** (np.arange(0, AXIS_DIM, 2, dtype=np.float64) / AXIS_DIM))
    ang_r = row[:, None] * inv_freq[None, :]
    ang_c = col[:, None] * inv_freq[None, :]
    cos_t = np.concatenate([np.cos(ang_r), np.cos(ang_c), np.cos(ang_r), np.cos(ang_c)], axis=-1)
    sin_t = np.concatenate([-np.sin(ang_r), -np.sin(ang_c), np.sin(ang_r), np.sin(ang_c)], axis=-1)
    return jnp.asarray(cos_t, F32), jnp.asarray(sin_t, F32)


def _pair_heads(w):
    lead = w.shape[:-1]
    heads = w.shape[-1] // HEAD_DIM
    quarter = HEAD_DIM // 4
    w = w.reshape(lead + (heads, 2, 2, quarter))
    return jnp.swapaxes(w, -2, -3).reshape(lead + (heads * HEAD_DIM,))


def _w_in_prep_kernel(w_ref, o_ref):
    lane = lax.broadcasted_iota(jnp.int32, (WPREP_ROWS, HEAD_DIM), 1)
    quarter = HEAD_DIM // 4
    takes_c1 = (lane >= quarter) & (lane < 2 * quarter)
    takes_r2 = (lane >= 2 * quarter) & (lane < 3 * quarter)
    rope_slabs = (set(range(0, 2 * RET_HEADS))
                  | set(range(4 * RET_HEADS, 4 * RET_HEADS + ATTN_HEADS + ATTN_KV_HEADS)))
    for s in range(IN_WIDTH // HEAD_DIM):
        cols = slice(s * HEAD_DIM, (s + 1) * HEAD_DIM)
        y = w_ref[:, cols]
        if s in rope_slabs:
            y = jnp.where(takes_c1, pltpu.roll(y, HEAD_DIM - quarter, 1),
                          jnp.where(takes_r2, pltpu.roll(y, quarter, 1), y))
        o_ref[:, cols] = y.astype(BF16)


def _in_proj_weights(w_in):
    rows = w_in.shape[0]
    assert rows % WPREP_ROWS == 0
    vmem = 2 * WPREP_ROWS * IN_WIDTH * (4 + 2)
    return pl.pallas_call(
        _w_in_prep_kernel,
        out_shape=jax.ShapeDtypeStruct((rows, IN_WIDTH), BF16),
        grid=(rows // WPREP_ROWS,),
        in_specs=[pl.BlockSpec((WPREP_ROWS, IN_WIDTH), lambda i: (i, 0))],
        out_specs=pl.BlockSpec((WPREP_ROWS, IN_WIDTH), lambda i: (i, 0)),
        compiler_params=pltpu.CompilerParams(
            dimension_semantics=("arbitrary",),
            vmem_limit_bytes=_vmem_limit(vmem)),
        name="w_in_prep",
    )(w_in)


def kernel(x, mem, norm_mix_w, w_in, ret_decay_fwd, ret_decay_bwd, ret_gn_w, ret_gn_b, attn_q_norm_w, attn_k_norm_w, w_out, norm_cross_w, norm_mem_w, w_cross_q, w_cross_k, w_cross_v, w_cross_o, norm_mlp_w, w_mlp_up, w_mlp_down, norm_final_w):
    batch, seq, _ = x.shape
    assert w_in.shape[0] == 1, "single-layer block: per-layer parameters have a leading axis of 1"
    cos_t, sin_t = _rope_tables(seq)
    xs = x.reshape(batch * seq, D_MODEL)
    mem2d = mem.reshape(batch * MEM_TOKENS, D_MODEL)
    dec = jnp.stack([ret_decay_fwd[0], ret_decay_bwd[0]], axis=1)
    dec = jnp.broadcast_to(dec[:, :, None], (RET_HEADS, 2, RET_C)).astype(F32)
    proj = _in_proj(xs, norm_mix_w, _in_proj_weights(w_in[0]), cos_t, sin_t,
                    _pair_heads(attn_q_norm_w), _pair_heads(attn_k_norm_w), seq)
    y_ret, (w_out_bf16, w_cq_bf16, w_co_bf16) = _retention(
        proj, dec, ret_gn_w, ret_gn_b, (w_out[0], w_cross_q[0], w_cross_o[0]), batch, seq)
    y_attn, w_up_bf16, w_down_bf16 = _attention(proj, w_mlp_up[0], w_mlp_down[0], batch, seq)
    mem_kv = _mem_kv(mem2d, norm_mem_w, w_cross_k[0], w_cross_v[0])
    xs = _out_cross(xs, y_ret, y_attn, w_out_bf16, norm_cross_w, w_cq_bf16, mem_kv, w_co_bf16, seq)
    xs = _mlp(xs, norm_mlp_w, w_up_bf16, w_down_bf16, norm_final_w[None, :])
    return xs.reshape(batch, seq, D_MODEL)
```

```python
import functools
import math

import numpy as np
import jax
import jax.numpy as jnp
from jax import lax
from jax.experimental import pallas as pl
from jax.experimental.pallas import tpu as pltpu

D_MODEL = 2048
HEAD_DIM = 128
RET_WIDTH = 1024
ATTN_WIDTH = 1024
RET_HEADS = 8
ATTN_HEADS = 8
ATTN_KV_HEADS = 2
GQA_GROUP = ATTN_HEADS // ATTN_KV_HEADS
KV_WIDTH = ATTN_KV_HEADS * HEAD_DIM
IN_WIDTH = 4 * RET_WIDTH + ATTN_WIDTH + 2 * KV_WIDTH
GRID_W = 64
AXIS_DIM = HEAD_DIM // 2
ROPE_THETA = 10000.0
MEM_TOKENS = 256
CROSS_HEADS = 4
CROSS_HEAD_DIM = 128
CROSS_WIDTH = CROSS_HEADS * CROSS_HEAD_DIM
D_FF = 4 * D_MODEL
NORM_EPS = 1e-6

V7X_LANES = 128
V7X_BF16_SUBLANES = 16
V7X_VMEM_BYTES = 64 * 1024 * 1024

F32 = jnp.float32
BF16 = jnp.bfloat16
LOG2E = math.log2(math.e)
NEG_BIG = -1e30

IN_TM = 512
IN_TN = 512
IN_NORM_RC = 256
WPREP_ROWS = 256
RET_C = 256
RET_TS = 2048
RET_HG = 4
ATT_TQ = 512
ATT_TK = 4096
ATT_DENOM_FLOOR = 2.0 ** -40
ATT_BOUND_MARGIN = 1.0 + 2.0 ** -6
OC_TM = 512
OC_RC = 512
MLP_TM = 1024
MLP_TF = 1024
MLP_EDGE_RC = 512


def _vmem_limit(nbytes):
    return int(min(nbytes + 16 * 1024 * 1024, V7X_VMEM_BYTES - 8 * 1024 * 1024))


def _rms_scale(y):
    return lax.rsqrt(jnp.mean(y * y, axis=-1, keepdims=True) + NORM_EPS)


def _slab_cast_specs(weights, steps, flat_step):
    specs, shapes, nbytes = [], [], 0
    for w in weights:
        assert w.shape[0] % steps == 0 and (w.shape[0] // steps) % V7X_BF16_SUBLANES == 0
        block = (w.shape[0] // steps, w.shape[1])
        specs.append(pl.BlockSpec(block, lambda *ids: (flat_step(*ids), 0)))
        shapes.append(jax.ShapeDtypeStruct(w.shape, BF16))
        nbytes += 2 * block[0] * block[1] * (4 + 2)
    return specs, shapes, nbytes


def _cast_slabs(src_refs, dst_refs):
    for src, dst in zip(src_refs, dst_refs):
        dst[...] = src[...].astype(BF16)


def _rope(y, cos, sin_signed):
    return y * cos + pltpu.roll(y, HEAD_DIM // 2, 1) * sin_signed


def _in_proj_kernel(x_ref, nw_ref, w_ref, cos_ref, sin_ref, qg_ref, kg_ref, o_ref, h_ref):
    for r in range(IN_TM // IN_NORM_RC):
        rows = slice(r * IN_NORM_RC, (r + 1) * IN_NORM_RC)
        xr = x_ref[rows, :]
        h_ref[rows, :] = (xr * _rms_scale(xr) * nw_ref[...]).astype(BF16)

    ret_k_scale = HEAD_DIM ** -0.5
    attn_q_scale = (HEAD_DIM ** -0.5) * LOG2E

    n_slabs = IN_TN // HEAD_DIM

    def run(j, epilogue):
        tile = slice(j * IN_TN, (j + 1) * IN_TN)
        acc = jnp.dot(h_ref[...], w_ref[:, tile], preferred_element_type=F32)
        slabs = epilogue([acc[:, s * HEAD_DIM:(s + 1) * HEAD_DIM] for s in range(n_slabs)],
                         cos_ref[...], sin_ref[...])
        for s, y in enumerate(slabs):
            c0 = j * IN_TN + s * HEAD_DIM
            o_ref[:, c0:c0 + HEAD_DIM] = y.astype(BF16)

    def ret_q(ys, cos, sin):
        return [_rope(y, cos, sin) for y in ys]

    def ret_k(ys, cos, sin):
        return [_rope(y, cos, sin) * ret_k_scale for y in ys]

    def plain(ys, cos, sin):
        return ys

    def head_inv_rms(y_a, y_b):
        width = 2 * HEAD_DIM
        r = lax.broadcasted_iota(jnp.int32, (width, width), 0) // HEAD_DIM
        c = lax.broadcasted_iota(jnp.int32, (width, width), 1) // HEAD_DIM
        ones_bd = jnp.where(r == c, 1.0, 0.0).astype(BF16)
        sq = jnp.concatenate([(y_a * y_a).astype(BF16), (y_b * y_b).astype(BF16)], axis=1)
        ms = jnp.dot(sq, ones_bd, preferred_element_type=F32) * (1.0 / HEAD_DIM)
        inv = lax.rsqrt(ms + NORM_EPS)
        return inv[:, :HEAD_DIM], inv[:, HEAD_DIM:]

    def normed_rope(y_a, y_b, gain, scale, cos, sin):
        inv_a, inv_b = head_inv_rms(y_a, y_b)
        return [_rope(y_a * gain, cos, sin) * (inv_a * scale),
                _rope(y_b * gain, cos, sin) * (inv_b * scale)]

    def attn_q(ys, cos, sin):
        gain = qg_ref[...]
        return (normed_rope(ys[0], ys[1], gain, attn_q_scale, cos, sin)
                + normed_rope(ys[2], ys[3], gain, attn_q_scale, cos, sin))

    def attn_kv(ys, cos, sin):
        return normed_rope(ys[0], ys[1], kg_ref[...], 1.0, cos, sin) + ys[ATTN_KV_HEADS:]

    nb = RET_WIDTH // IN_TN
    for j in range(IN_WIDTH // IN_TN):
        if j < nb:
            run(j, ret_q)
        elif j < 2 * nb:
            run(j, ret_k)
        elif j < 4 * nb:
            run(j, plain)
        elif j < 5 * nb:
            run(j, attn_q)
        else:
            run(j, attn_kv)


def _in_proj(x2d, norm_w, w_in, cos_t, sin_t, q_gain, k_gain, seq):
    n = x2d.shape[0]
    assert IN_WIDTH % IN_TN == 0 and 2 * KV_WIDTH == IN_TN and n % IN_TM == 0 and seq % IN_TM == 0
    tiles_per_seq = seq // IN_TM
    vmem = (2 * IN_TM * D_MODEL * 4 + IN_TM * D_MODEL * 2 + D_MODEL * IN_WIDTH * 2
            + 2 * IN_TM * IN_WIDTH * 2 + 4 * IN_TM * HEAD_DIM * 4)
    resident = pl.Buffered(1)
    return pl.pallas_call(
        _in_proj_kernel,
        out_shape=jax.ShapeDtypeStruct((n, IN_WIDTH), BF16),
        grid=(n // IN_TM,),
        in_specs=[
            pl.BlockSpec((IN_TM, D_MODEL), lambda i: (i, 0)),
            pl.BlockSpec((1, D_MODEL), lambda i: (0, 0)),
            pl.BlockSpec((D_MODEL, IN_WIDTH), lambda i: (0, 0), pipeline_mode=resident),
            pl.BlockSpec((IN_TM, HEAD_DIM), lambda i: (i % tiles_per_seq, 0)),
            pl.BlockSpec((IN_TM, HEAD_DIM), lambda i: (i % tiles_per_seq, 0)),
            pl.BlockSpec((1, HEAD_DIM), lambda i: (0, 0)),
            pl.BlockSpec((1, HEAD_DIM), lambda i: (0, 0)),
        ],
        out_specs=pl.BlockSpec((IN_TM, IN_WIDTH), lambda i: (i, 0)),
        scratch_shapes=[pltpu.VMEM((IN_TM, D_MODEL), BF16)],
        compiler_params=pltpu.CompilerParams(
            dimension_semantics=("arbitrary",),
            vmem_limit_bytes=_vmem_limit(vmem)),
        name="in_proj",
    )(x2d, norm_w, w_in, cos_t, sin_t, q_gain, k_gain)


def _retention_kernel(dec_ref, q_ref, k_ref, v_ref, g_ref, gw_ref, gb_ref,
                      wout_ref, wq_ref, wco_ref, o_ref, wout_o_ref, wq_o_ref, wco_o_ref,
                      d_ref, tab_ref, sb_ref, st_ref, *, n_seq):
    _cast_slabs((wout_ref, wq_ref, wco_ref), (wout_o_ref, wq_o_ref, wco_o_ref))
    s = pl.program_id(2)
    c_len = RET_C
    cps = RET_TS // RET_C
    contract_rows = (((0,), (0,)), ((), ()))
    contract_cols = (((1,), (1,)), ((), ()))
    head_cols = [slice(h * HEAD_DIM, (h + 1) * HEAD_DIM) for h in range(RET_HG)]
    chunk_rows = [slice(c * c_len, (c + 1) * c_len) for c in range(cps)]

    def log_gammas(h):
        return -jnp.exp(dec_ref[h, 0:1, :]), -jnp.exp(dec_ref[h, 1:2, :])

    @pl.when(s == 0)
    def _():
        row = lax.broadcasted_iota(jnp.int32, (c_len, c_len), 0)
        col = lax.broadcasted_iota(jnp.int32, (c_len, c_len), 1)
        rel = (row - col).astype(F32)
        idx = lax.broadcasted_iota(jnp.int32, (c_len, HEAD_DIM), 0).astype(F32)
        for h in range(RET_HG):
            lg_f, lg_b = log_gammas(h)
            d_ref[h] = jnp.exp(jnp.where(rel >= 0, lg_f * rel, -lg_b * rel))
            lf = lg_f[:, :HEAD_DIM]
            lb = lg_b[:, :HEAD_DIM]
            tab_ref[h, 0] = jnp.exp(lf * (idx + 1.0)).astype(BF16)
            tab_ref[h, 1] = jnp.exp(lb * (c_len - idx)).astype(BF16)
            tab_ref[h, 2] = jnp.exp(lf * (c_len - 1.0 - idx)).astype(BF16)
            tab_ref[h, 3] = jnp.exp(lb * idx).astype(BF16)
        st_ref[...] = jnp.zeros(st_ref.shape, F32)

    @pl.when(s < n_seq)
    def _():
        first_chunk = (n_seq - 1 - s) * cps
        for h in range(RET_HG):
            chunk_b = jnp.exp(log_gammas(h)[1][:, :HEAD_DIM] * c_len)
            state = st_ref[h]
            for c in reversed(range(cps)):
                sb_ref[first_chunk + c, h] = state.astype(BF16)
                k = k_ref[chunk_rows[c], head_cols[h]]
                kd = k * tab_ref[h, 3]
                upd = lax.dot_general(kd, v_ref[chunk_rows[c], head_cols[h]], contract_rows,
                                      preferred_element_type=F32)
                state = state * chunk_b + upd
            st_ref[h] = state

        @pl.when(s == n_seq - 1)
        def _():
            st_ref[...] = jnp.zeros(st_ref.shape, F32)

    @pl.when(s >= n_seq)
    def _():
        first_chunk = (s - n_seq) * cps
        for h in range(RET_HG):
            chunk_f = jnp.exp(log_gammas(h)[0][:, :HEAD_DIM] * c_len)
            gw = gw_ref[:, head_cols[h]]
            gb = gb_ref[:, head_cols[h]]
            state = st_ref[h]
            for c in range(cps):
                q = q_ref[chunk_rows[c], head_cols[h]]
                k = k_ref[chunk_rows[c], head_cols[h]]
                v = v_ref[chunk_rows[c], head_cols[h]]
                scores = lax.dot_general(q, k, contract_cols, preferred_element_type=F32)
                masked = (scores * d_ref[h]).astype(BF16)
                lhs = jnp.concatenate([masked, q * tab_ref[h, 0], q * tab_ref[h, 1]], axis=1)
                rhs = jnp.concatenate([v, state.astype(BF16), sb_ref[first_chunk + c, h]], axis=0)
                out = jnp.dot(lhs, rhs, preferred_element_type=F32)
                mu = jnp.mean(out, axis=-1, keepdims=True)
                cen = out - mu
                var = jnp.mean(cen * cen, axis=-1, keepdims=True)
                y = cen * lax.rsqrt(var + NORM_EPS) * gw + gb
                g = g_ref[chunk_rows[c], head_cols[h]].astype(F32)
                y = y * (g / (1.0 + jnp.exp(-g)))
                o_ref[chunk_rows[c], head_cols[h]] = y.astype(BF16)
                kd = k * tab_ref[h, 2]
                upd = lax.dot_general(kd, v, contract_rows, preferred_element_type=F32)
                state = state * chunk_f + upd
            st_ref[h] = state


def _retention(proj, dec, gn_w, gn_b, cross_weights, batch, seq):
    n = proj.shape[0]
    assert seq % RET_TS == 0 and RET_TS % RET_C == 0 and RET_HEADS % RET_HG == 0
    n_seq = seq // RET_TS
    n_chunks = seq // RET_C
    group_w = RET_HG * HEAD_DIM
    groups = RET_HEADS // RET_HG

    def kv_block(section):
        def index(b, hg, s):
            blk = jnp.where(s < n_seq, n_seq - 1 - s, s - n_seq)
            return (b * n_seq + blk, section * groups + hg)
        return index

    def fwd_block(section):
        return lambda b, hg, s: (b * n_seq + jnp.maximum(s - n_seq, 0), section * groups + hg)

    cast_specs, cast_shapes, cast_bytes = _slab_cast_specs(
        cross_weights, batch * groups * 2 * n_seq,
        lambda b, hg, s: (b * groups + hg) * 2 * n_seq + s)
    vmem = (2 * 5 * RET_TS * group_w * 2 + RET_HG * RET_C * RET_C * 4
            + RET_HG * 4 * RET_C * HEAD_DIM * 4 + n_chunks * RET_HG * HEAD_DIM * HEAD_DIM * 2
            + RET_HG * HEAD_DIM * HEAD_DIM * 4 + cast_bytes)
    y_ret, *cast = pl.pallas_call(
        functools.partial(_retention_kernel, n_seq=n_seq),
        out_shape=(jax.ShapeDtypeStruct((n, RET_WIDTH), BF16), *cast_shapes),
        grid=(batch, groups, 2 * n_seq),
        in_specs=[
            pl.BlockSpec((RET_HG, 2, RET_C), lambda b, hg, s: (hg, 0, 0)),
            pl.BlockSpec((RET_TS, group_w), fwd_block(0)),
            pl.BlockSpec((RET_TS, group_w), kv_block(1)),
            pl.BlockSpec((RET_TS, group_w), kv_block(2)),
            pl.BlockSpec((RET_TS, group_w), fwd_block(3)),
            pl.BlockSpec((1, group_w), lambda b, hg, s: (0, hg)),
            pl.BlockSpec((1, group_w), lambda b, hg, s: (0, hg)),
            *cast_specs,
        ],
        out_specs=(pl.BlockSpec((RET_TS, group_w), fwd_block(0)), *cast_specs),
        scratch_shapes=[
            pltpu.VMEM((RET_HG, RET_C, RET_C), F32),
            pltpu.VMEM((RET_HG, 4, RET_C, HEAD_DIM), BF16),
            pltpu.VMEM((n_chunks, RET_HG, HEAD_DIM, HEAD_DIM), BF16),
            pltpu.VMEM((RET_HG, HEAD_DIM, HEAD_DIM), F32),
        ],
        compiler_params=pltpu.CompilerParams(
            dimension_semantics=("arbitrary", "arbitrary", "arbitrary"),
            vmem_limit_bytes=_vmem_limit(vmem)),
        name="retention",
    )(dec, proj, proj, proj, proj, gn_w, gn_b, *cross_weights)
    return y_ret, cast


def _attention_kernel(q_ref, k_ref, v_ref, bound_ref, wu_ref, wd_ref, o_ref, wu_o_ref, wd_o_ref,
                      vt_ref, qa_ref, acc_ref, den_ref, *, n_kv):
    _cast_slabs((wu_ref, wd_ref), (wu_o_ref, wd_o_ref))

    i = pl.program_id(2)
    contract_cols = (((1,), (1,)), ((), ()))

    def kv_rows(c):
        return pl.ds(pl.multiple_of(c * ATT_TK, ATT_TK), ATT_TK)

    @pl.when(i == 0)
    def _():
        def body(c, carry):
            vt_ref[c] = v_ref[kv_rows(c), :].astype(F32).T.astype(BF16)
            return carry
        lax.fori_loop(0, n_kv, body, 0)

    q_lane = lax.broadcasted_iota(jnp.int32, (ATT_TQ, HEAD_DIM), 1)
    k_lane = lax.broadcasted_iota(jnp.int32, (ATT_TK, HEAD_DIM), 1)
    k_one = jnp.where(k_lane == 0, 1.0, 0.0).astype(BF16)
    head_cols = [slice(h * HEAD_DIM, (h + 1) * HEAD_DIM) for h in range(GQA_GROUP)]

    def shift_column(shift):
        return jnp.where(q_lane == 0, -shift, 0.0).astype(BF16)

    def sweep(q_aug):
        acc_ref[...] = jnp.zeros(acc_ref.shape, F32)
        den_ref[...] = jnp.zeros(den_ref.shape, F32)

        def kv_step(c, carry):
            k_aug = jnp.concatenate([k_ref[kv_rows(c), :], k_one], axis=1)
            vt = vt_ref[c]
            for h in range(GQA_GROUP):
                s_t = lax.dot_general(k_aug, q_aug(h), contract_cols,
                                      preferred_element_type=F32)
                p_t = jnp.exp2(s_t)
                den_ref[h] += jnp.sum(p_t.reshape(ATT_TK // 8, 8, ATT_TQ), axis=0)
                acc_ref[h] += jnp.dot(vt, p_t.astype(BF16), preferred_element_type=F32)
            return carry
        lax.fori_loop(0, n_kv, kv_step, 0)

    bound_column = shift_column(bound_ref[...])
    sweep(lambda h: jnp.concatenate([q_ref[:, head_cols[h]], bound_column], axis=1))

    def denominator(h):
        return jnp.sum(den_ref[h], axis=0, keepdims=True)

    denom_min = functools.reduce(jnp.minimum,
                                 [jnp.min(denominator(h)) for h in range(GQA_GROUP)])

    @pl.when(denom_min < ATT_DENOM_FLOOR)
    def _():
        for h in range(GQA_GROUP):
            def body(c, m):
                s = lax.dot_general(q_ref[:, head_cols[h]], k_ref[kv_rows(c), :], contract_cols,
                                    preferred_element_type=F32)
                return jnp.maximum(m, jnp.max(s, axis=1, keepdims=True))
            row_max = lax.fori_loop(0, n_kv, body, jnp.full((ATT_TQ, 1), NEG_BIG, F32))
            qa_ref[h, :, 0:HEAD_DIM] = q_ref[:, head_cols[h]]
            qa_ref[h, :, HEAD_DIM:] = shift_column(row_max)
        sweep(lambda h: qa_ref[h])

    for h in range(GQA_GROUP):
        o_t = acc_ref[h] * (1.0 / denominator(h))
        o_ref[:, head_cols[h]] = o_t.T.astype(BF16)


def _attention(proj, score_bound, w_up, w_down, batch, seq):
    n = proj.shape[0]
    assert seq % ATT_TQ == 0 and seq % ATT_TK == 0
    n_q = seq // ATT_TQ
    n_kv = seq // ATT_TK
    group_w = GQA_GROUP * HEAD_DIM
    q_col0 = 4 * RET_WIDTH // group_w
    k_col0 = (4 * RET_WIDTH + ATTN_WIDTH) // HEAD_DIM
    v_col0 = k_col0 + ATTN_KV_HEADS
    cast_specs, cast_shapes, cast_bytes = _slab_cast_specs(
        (w_up, w_down), batch * ATTN_KV_HEADS * n_q,
        lambda b, g, i: (b * ATTN_KV_HEADS + g) * n_q + i)
    vmem = (2 * ATT_TQ * group_w * 2 * 2 + 2 * 2 * seq * HEAD_DIM * 2
            + HEAD_DIM * seq * 2 + GQA_GROUP * (HEAD_DIM + 8) * ATT_TQ * 4
            + GQA_GROUP * ATT_TQ * 2 * HEAD_DIM * 2 + 2 * GQA_GROUP * ATT_TK * ATT_TQ * 4
            + cast_bytes)
    return pl.pallas_call(
        functools.partial(_attention_kernel, n_kv=n_kv),
        out_shape=(jax.ShapeDtypeStruct((n, ATTN_WIDTH), BF16), *cast_shapes),
        grid=(batch, ATTN_KV_HEADS, n_q),
        in_specs=[
            pl.BlockSpec((ATT_TQ, group_w), lambda b, g, i: (b * n_q + i, q_col0 + g)),
            pl.BlockSpec((seq, HEAD_DIM), lambda b, g, i: (b, k_col0 + g)),
            pl.BlockSpec((seq, HEAD_DIM), lambda b, g, i: (b, v_col0 + g)),
            pl.BlockSpec((1, HEAD_DIM), lambda b, g, i: (0, 0)),
            *cast_specs,
        ],
        out_specs=(pl.BlockSpec((ATT_TQ, group_w), lambda b, g, i: (b * n_q + i, g)),
                   *cast_specs),
        scratch_shapes=[
            pltpu.VMEM((n_kv, HEAD_DIM, ATT_TK), BF16),
            pltpu.VMEM((GQA_GROUP, ATT_TQ, 2 * HEAD_DIM), BF16),
            pltpu.VMEM((GQA_GROUP, HEAD_DIM, ATT_TQ), F32),
            pltpu.VMEM((GQA_GROUP, 8, ATT_TQ), F32),
        ],
        compiler_params=pltpu.CompilerParams(
            dimension_semantics=("arbitrary", "arbitrary", "arbitrary"),
            vmem_limit_bytes=_vmem_limit(vmem)),
        name="attention",
    )(proj, proj, proj, score_bound, w_up, w_down)


def _mem_kv_kernel(m_ref, nw_ref, wk_ref, wv_ref, o_ref):
    m = m_ref[...]
    h = (m * _rms_scale(m) * nw_ref[...]).astype(BF16)
    o_ref[:, :CROSS_WIDTH] = jnp.dot(h, wk_ref[...].astype(BF16),
                                     preferred_element_type=F32).astype(BF16)
    o_ref[:, CROSS_WIDTH:] = jnp.dot(h, wv_ref[...].astype(BF16),
                                     preferred_element_type=F32).astype(BF16)


def _mem_kv(mem2d, norm_w, w_k, w_v):
    rows = mem2d.shape[0]
    vmem = 2 * (rows * D_MODEL * 4 + 2 * D_MODEL * CROSS_WIDTH * 4 + rows * 2 * CROSS_WIDTH * 2)
    return pl.pallas_call(
        _mem_kv_kernel,
        out_shape=jax.ShapeDtypeStruct((rows, 2 * CROSS_WIDTH), BF16),
        grid=(1,),
        in_specs=[
            pl.BlockSpec((rows, D_MODEL), lambda i: (0, 0)),
            pl.BlockSpec((1, D_MODEL), lambda i: (0, 0)),
            pl.BlockSpec((D_MODEL, CROSS_WIDTH), lambda i: (0, 0)),
            pl.BlockSpec((D_MODEL, CROSS_WIDTH), lambda i: (0, 0)),
        ],
        out_specs=pl.BlockSpec((rows, 2 * CROSS_WIDTH), lambda i: (0, 0)),
        compiler_params=pltpu.CompilerParams(
            dimension_semantics=("arbitrary",),
            vmem_limit_bytes=_vmem_limit(vmem)),
        name="mem_kv",
    )(mem2d, norm_w, w_k, w_v)


def _out_cross_kernel(x_ref, yr_ref, ya_ref, wo_ref, nw_ref, wq_ref, kv_ref, wco_ref, o_ref):
    contract_cols = (((1,), (1,)), ((), ()))
    scale = CROSS_HEAD_DIM ** -0.5

    for r in range(OC_TM // OC_RC):
        rows = slice(r * OC_RC, (r + 1) * OC_RC)
        x1 = (x_ref[rows, :]
              + jnp.dot(yr_ref[rows, :], wo_ref[0:RET_WIDTH, :], preferred_element_type=F32)
              + jnp.dot(ya_ref[rows, :], wo_ref[RET_WIDTH:, :], preferred_element_type=F32))
        h = (x1 * _rms_scale(x1) * nw_ref[...]).astype(BF16)
        q = (jnp.dot(h, wq_ref[...], preferred_element_type=F32) * scale).astype(BF16)
        heads = []
        for hd in range(CROSS_HEADS):
            cols = slice(hd * CROSS_HEAD_DIM, (hd + 1) * CROSS_HEAD_DIM)
            k = kv_ref[:, cols]
            v = kv_ref[:, CROSS_WIDTH + hd * CROSS_HEAD_DIM:CROSS_WIDTH + (hd + 1) * CROSS_HEAD_DIM]
            s = lax.dot_general(q[:, cols], k, contract_cols, preferred_element_type=F32)
            e = jnp.exp(s - jnp.max(s, axis=-1, keepdims=True))
            inv = 1.0 / jnp.sum(e, axis=-1, keepdims=True)
            pv = jnp.dot(e.astype(BF16), v, preferred_element_type=F32)
            heads.append((pv * inv).astype(BF16))
        o = jnp.concatenate(heads, axis=1)
        o_ref[rows, :] = x1 + jnp.dot(o, wco_ref[...], preferred_element_type=F32)


def _out_cross(x2d, y_ret, y_attn, w_out, norm_w, w_q, mem_kv, w_co, seq):
    n = x2d.shape[0]
    assert n % OC_TM == 0 and seq % OC_TM == 0
    tiles_per_seq = seq // OC_TM
    vmem = (2 * 2 * OC_TM * D_MODEL * 4 + 2 * 2 * OC_TM * RET_WIDTH * 2 + 2 * D_MODEL * D_MODEL * 2
            + 2 * 2 * D_MODEL * CROSS_WIDTH * 2 + 2 * MEM_TOKENS * 2 * CROSS_WIDTH * 2)
    return pl.pallas_call(
        _out_cross_kernel,
        out_shape=jax.ShapeDtypeStruct((n, D_MODEL), F32),
        grid=(n // OC_TM,),
        in_specs=[
            pl.BlockSpec((OC_TM, D_MODEL), lambda i: (i, 0)),
            pl.BlockSpec((OC_TM, RET_WIDTH), lambda i: (i, 0)),
            pl.BlockSpec((OC_TM, ATTN_WIDTH), lambda i: (i, 0)),
            pl.BlockSpec((D_MODEL, D_MODEL), lambda i: (0, 0)),
            pl.BlockSpec((1, D_MODEL), lambda i: (0, 0)),
            pl.BlockSpec((D_MODEL, CROSS_WIDTH), lambda i: (0, 0)),
            pl.BlockSpec((MEM_TOKENS, 2 * CROSS_WIDTH), lambda i: (i // tiles_per_seq, 0)),
            pl.BlockSpec((CROSS_WIDTH, D_MODEL), lambda i: (0, 0)),
        ],
        out_specs=pl.BlockSpec((OC_TM, D_MODEL), lambda i: (i, 0)),
        compiler_params=pltpu.CompilerParams(
            dimension_semantics=("arbitrary",),
            vmem_limit_bytes=_vmem_limit(vmem)),
        name="out_cross",
    )(x2d, y_ret, y_attn, w_out, norm_w, w_q, mem_kv, w_co)


def _mlp_kernel(x_hbm, nw_ref, wu_ref, wd_ref, fw_ref, o_ref, h_ref, x_ref, x_sem):
    i = pl.program_id(0)
    j = pl.program_id(1)
    last = pl.num_programs(1) - 1

    def x_copy(tile):
        rows = pl.ds(pl.multiple_of(tile * MLP_TM, MLP_TM), MLP_TM)
        return pltpu.make_async_copy(x_hbm.at[rows, :], x_ref, x_sem)

    @pl.when((i == 0) & (j == 0))
    def _():
        x_copy(0).start()

    @pl.when(j == 0)
    def _():
        x_copy(i).wait()

    @pl.when((j == 1) & (i + 1 < pl.num_programs(0)))
    def _():
        x_copy(i + 1).start()

    def step(rows, is_first, is_last):
        if is_first:
            xr = x_ref[rows, :]
            h = (xr * _rms_scale(xr) * nw_ref[...]).astype(BF16)
            h_ref[rows, :] = h
            base = xr
        else:
            h = h_ref[rows, :]
            base = o_ref[rows, :]
        u = jnp.maximum(jnp.dot(h, wu_ref[...], preferred_element_type=F32), 0.0)
        y = base + jnp.dot((u * u).astype(BF16), wd_ref[...], preferred_element_type=F32)
        if is_last:
            y = y * _rms_scale(y) * fw_ref[...]
        o_ref[rows, :] = y

    edge_chunks = [slice(r * MLP_EDGE_RC, (r + 1) * MLP_EDGE_RC)
                   for r in range(MLP_TM // MLP_EDGE_RC)]

    @pl.when(j == 0)
    def _():
        for rows in edge_chunks:
            step(rows, True, False)

    @pl.when((j > 0) & (j < last))
    def _():
        step(slice(None), False, False)

    @pl.when(j == last)
    def _():
        for rows in edge_chunks:
            step(rows, False, True)


def _mlp(x2d, norm_w, w_up, w_down, final_w):
    n = x2d.shape[0]
    assert n % MLP_TM == 0 and D_FF % MLP_TF == 0 and D_FF // MLP_TF >= 3
    vmem = (3 * MLP_TM * D_MODEL * 4 + MLP_TM * D_MODEL * 2 + 2 * 2 * D_MODEL * MLP_TF * 2)
    return pl.pallas_call(
        _mlp_kernel,
        out_shape=jax.ShapeDtypeStruct((n, D_MODEL), F32),
        grid=(n // MLP_TM, D_FF // MLP_TF),
        in_specs=[
            pl.BlockSpec(memory_space=pl.ANY),
            pl.BlockSpec((1, D_MODEL), lambda i, j: (0, 0)),
            pl.BlockSpec((D_MODEL, MLP_TF), lambda i, j: (0, j)),
            pl.BlockSpec((MLP_TF, D_MODEL), lambda i, j: (j, 0)),
            pl.BlockSpec((1, D_MODEL), lambda i, j: (0, 0)),
        ],
        out_specs=pl.BlockSpec((MLP_TM, D_MODEL), lambda i, j: (i, 0)),
        scratch_shapes=[
            pltpu.VMEM((MLP_TM, D_MODEL), BF16),
            pltpu.VMEM((MLP_TM, D_MODEL), F32),
            pltpu.SemaphoreType.DMA(()),
        ],
        compiler_params=pltpu.CompilerParams(
            dimension_semantics=("arbitrary", "arbitrary"),
            vmem_limit_bytes=_vmem_limit(vmem)),
        name="mlp",
    )(x2d, norm_w, w_up, w_down, final_w)


def _rope_tables(seq):
    t = np.arange(seq)
    row = (t // GRID_W).astype(np.float64)
    col = (t % GRID_W).astype(np.float64)
    inv_freq = 1.0 / (ROPE_THETA ** (np.arange(0, AXIS_DIM, 2, dtype=np.float64) / AXIS_DIM))
    ang_r = row[:, None] * inv_freq[None, :]
    ang_c = col[:, None] * inv_freq[None, :]
    cos_t = np.concatenate([np.cos(ang_r), np.cos(ang_c), np.cos(ang_r), np.cos(ang_c)], axis=-1)
    sin_t = np.concatenate([-np.sin(ang_r), -np.sin(ang_c), np.sin(ang_r), np.sin(ang_c)], axis=-1)
    return jnp.asarray(cos_t, F32), jnp.asarray(sin_t, F32)


def _pair_heads(w):
    lead = w.shape[:-1]
    heads = w.shape[-1] // HEAD_DIM
    quarter = HEAD_DIM // 4
    w = w.reshape(lead + (heads, 2, 2, quarter))
    return jnp.swapaxes(w, -2, -3).reshape(lead + (heads * HEAD_DIM,))


def _w_in_prep_kernel(w_ref, o_ref):
    lane = lax.broadcasted_iota(jnp.int32, (WPREP_ROWS, HEAD_DIM), 1)
    quarter = HEAD_DIM // 4
    takes_c1 = (lane >= quarter) & (lane < 2 * quarter)
    takes_r2 = (lane >= 2 * quarter) & (lane < 3 * quarter)
    rope_slabs = (set(range(0, 2 * RET_HEADS))
                  | set(range(4 * RET_HEADS, 4 * RET_HEADS + ATTN_HEADS + ATTN_KV_HEADS)))
    for s in range(IN_WIDTH // HEAD_DIM):
        cols = slice(s * HEAD_DIM, (s + 1) * HEAD_DIM)
        y = w_ref[:, cols]
        if s in rope_slabs:
            y = jnp.where(takes_c1, pltpu.roll(y, HEAD_DIM - quarter, 1),
                          jnp.where(takes_r2, pltpu.roll(y, quarter, 1), y))
        o_ref[:, cols] = y.astype(BF16)


def _in_proj_weights(w_in):
    rows = w_in.shape[0]
    assert rows % WPREP_ROWS == 0
    vmem = 2 * WPREP_ROWS * IN_WIDTH * (4 + 2)
    return pl.pallas_call(
        _w_in_prep_kernel,
        out_shape=jax.ShapeDtypeStruct((rows, IN_WIDTH), BF16),
        grid=(rows // WPREP_ROWS,),
        in_specs=[pl.BlockSpec((WPREP_ROWS, IN_WIDTH), lambda i: (i, 0))],
        out_specs=pl.BlockSpec((WPREP_ROWS, IN_WIDTH), lambda i: (i, 0)),
        compiler_params=pltpu.CompilerParams(
            dimension_semantics=("arbitrary",),
            vmem_limit_bytes=_vmem_limit(vmem)),
        name="w_in_prep",
    )(w_in)


def kernel(x, mem, norm_mix_w, w_in, ret_decay_fwd, ret_decay_bwd, ret_gn_w, ret_gn_b, attn_q_norm_w, attn_k_norm_w, w_out, norm_cross_w, norm_mem_w, w_cross_q, w_cross_k, w_cross_v, w_cross_o, norm_mlp_w, w_mlp_up, w_mlp_down, norm_final_w):
    batch, seq, _ = x.shape
    assert w_in.shape[0] == 1, "single-layer block: per-layer parameters have a leading axis of 1"
    cos_t, sin_t = _rope_tables(seq)
    xs = x.reshape(batch * seq, D_MODEL)
    mem2d = mem.reshape(batch * MEM_TOKENS, D_MODEL)
    dec = jnp.stack([ret_decay_fwd[0], ret_decay_bwd[0]], axis=1)
    dec = jnp.broadcast_to(dec[:, :, None], (RET_HEADS, 2, RET_C)).astype(F32)
    proj = _in_proj(xs, norm_mix_w, _in_proj_weights(w_in[0]), cos_t, sin_t,
                    _pair_heads(attn_q_norm_w), _pair_heads(attn_k_norm_w), seq)
    y_ret, (w_out_bf16, w_cq_bf16, w_co_bf16) = _retention(
        proj, dec, ret_gn_w, ret_gn_b, (w_out[0], w_cross_q[0], w_cross_o[0]), batch, seq)
    score_bound = (LOG2E * math.sqrt(HEAD_DIM) * ATT_BOUND_MARGIN
                   * jnp.max(jnp.abs(attn_q_norm_w)) * jnp.max(jnp.abs(attn_k_norm_w)))
    score_bound = jnp.full((1, HEAD_DIM), score_bound, F32)
    y_attn, w_up_bf16, w_down_bf16 = _attention(proj, score_bound, w_mlp_up[0], w_mlp_down[0],
                                                batch, seq)
    mem_kv = _mem_kv(mem2d, norm_mem_w, w_cross_k[0], w_cross_v[0])
    xs = _out_cross(xs, y_ret, y_attn, w_out_bf16, norm_cross_w, w_cq_bf16, mem_kv, w_co_bf16, seq)
    xs = _mlp(xs, norm_mlp_w, w_up_bf16, w_down_bf16, norm_final_w[None, :])
    return xs.reshape(batch, seq, D_MODEL)
```

```python
import functools
import math

import numpy as np
import jax
import jax.numpy as jnp
from jax import lax
from jax.experimental import pallas as pl
from jax.experimental.pallas import tpu as pltpu

D_MODEL = 2048
HEAD_DIM = 128
RET_WIDTH = 1024
ATTN_WIDTH = 1024
RET_HEADS = 8
ATTN_HEADS = 8
ATTN_KV_HEADS = 2
GQA_GROUP = ATTN_HEADS // ATTN_KV_HEADS
KV_WIDTH = ATTN_KV_HEADS * HEAD_DIM
IN_WIDTH = 4 * RET_WIDTH + ATTN_WIDTH + 2 * KV_WIDTH
GRID_W = 64
AXIS_DIM = HEAD_DIM // 2
ROPE_THETA = 10000.0
MEM_TOKENS = 256
CROSS_HEADS = 4
CROSS_HEAD_DIM = 128
CROSS_WIDTH = CROSS_HEADS * CROSS_HEAD_DIM
D_FF = 4 * D_MODEL
NORM_EPS = 1e-6

V7X_LANES = 128
V7X_BF16_SUBLANES = 16
V7X_VMEM_BYTES = 64 * 1024 * 1024

F32 = jnp.float32
BF16 = jnp.bfloat16
LOG2E = math.log2(math.e)
NEG_BIG = -1e30

IN_TM = 512
IN_TN = 512
IN_NORM_RC = 256
WPREP_ROWS = 256
RET_C = 256
RET_TS = 2048
RET_HG = 4
ATT_TQ = 512
ATT_TK = 4096
ATT_DENOM_FLOOR = 2.0 ** -40
ATT_BOUND_MARGIN = 1.0 + 2.0 ** -6
OC_TM = 512
OC_RC = 512
MLP_TM = 1024
MLP_TF = 1024
MLP_EDGE_RC = 512


def _vmem_limit(nbytes):
    return int(min(nbytes + 16 * 1024 * 1024, V7X_VMEM_BYTES - 8 * 1024 * 1024))


def _rms_scale(y):
    return lax.rsqrt(jnp.mean(y * y, axis=-1, keepdims=True) + NORM_EPS)


def _slab_cast_specs(weights, steps, flat_step):
    specs, shapes, nbytes = [], [], 0
    for w in weights:
        assert w.shape[0] % steps == 0 and (w.shape[0] // steps) % V7X_BF16_SUBLANES == 0
        block = (w.shape[0] // steps, w.shape[1])
        specs.append(pl.BlockSpec(block, lambda *ids: (flat_step(*ids), 0)))
        shapes.append(jax.ShapeDtypeStruct(w.shape, BF16))
        nbytes += 2 * block[0] * block[1] * (4 + 2)
    return specs, shapes, nbytes


def _cast_slabs(src_refs, dst_refs):
    for src, dst in zip(src_refs, dst_refs):
        dst[...] = src[...].astype(BF16)


def _rope(y, cos, sin_signed):
    return y * cos + pltpu.roll(y, HEAD_DIM // 2, 1) * sin_signed


def _in_proj_kernel(x_ref, nw_ref, w_ref, cos_ref, sin_ref, qg_ref, kg_ref, o_ref, h_ref):
    for r in range(IN_TM // IN_NORM_RC):
        rows = slice(r * IN_NORM_RC, (r + 1) * IN_NORM_RC)
        xr = x_ref[rows, :]
        h_ref[rows, :] = (xr * _rms_scale(xr) * nw_ref[...]).astype(BF16)

    ret_k_scale = HEAD_DIM ** -0.5
    attn_q_scale = (HEAD_DIM ** -0.5) * LOG2E

    n_slabs = IN_TN // HEAD_DIM

    def run(j, epilogue):
        tile = slice(j * IN_TN, (j + 1) * IN_TN)
        acc = jnp.dot(h_ref[...], w_ref[:, tile], preferred_element_type=F32)
        slabs = epilogue([acc[:, s * HEAD_DIM:(s + 1) * HEAD_DIM] for s in range(n_slabs)],
                         cos_ref[...], sin_ref[...])
        for s, y in enumerate(slabs):
            c0 = j * IN_TN + s * HEAD_DIM
            o_ref[:, c0:c0 + HEAD_DIM] = y.astype(BF16)

    def ret_q(ys, cos, sin):
        return [_rope(y, cos, sin) for y in ys]

    def ret_k(ys, cos, sin):
        return [_rope(y, cos, sin) * ret_k_scale for y in ys]

    def plain(ys, cos, sin):
        return ys

    def head_inv_rms(y_a, y_b):
        width = 2 * HEAD_DIM
        r = lax.broadcasted_iota(jnp.int32, (width, width), 0) // HEAD_DIM
        c = lax.broadcasted_iota(jnp.int32, (width, width), 1) // HEAD_DIM
        ones_bd = jnp.where(r == c, 1.0, 0.0).astype(BF16)
        sq = jnp.concatenate([(y_a * y_a).astype(BF16), (y_b * y_b).astype(BF16)], axis=1)
        ms = jnp.dot(sq, ones_bd, preferred_element_type=F32) * (1.0 / HEAD_DIM)
        inv = lax.rsqrt(ms + NORM_EPS)
        return inv[:, :HEAD_DIM], inv[:, HEAD_DIM:]

    def normed_rope(y_a, y_b, gain, scale, cos, sin):
        inv_a, inv_b = head_inv_rms(y_a, y_b)
        return [_rope(y_a * gain, cos, sin) * (inv_a * scale),
                _rope(y_b * gain, cos, sin) * (inv_b * scale)]

    def attn_q(ys, cos, sin):
        gain = qg_ref[...]
        return (normed_rope(ys[0], ys[1], gain, attn_q_scale, cos, sin)
                + normed_rope(ys[2], ys[3], gain, attn_q_scale, cos, sin))

    def attn_kv(ys, cos, sin):
        return normed_rope(ys[0], ys[1], kg_ref[...], 1.0, cos, sin) + ys[ATTN_KV_HEADS:]

    nb = RET_WIDTH // IN_TN
    for j in range(IN_WIDTH // IN_TN):
        if j < nb:
            run(j, ret_q)
        elif j < 2 * nb:
            run(j, ret_k)
        elif j < 4 * nb:
            run(j, plain)
        elif j < 5 * nb:
            run(j, attn_q)
        else:
            run(j, attn_kv)


def _in_proj(x2d, norm_w, w_in, cos_t, sin_t, q_gain, k_gain, seq):
    n = x2d.shape[0]
    assert IN_WIDTH % IN_TN == 0 and 2 * KV_WIDTH == IN_TN and n % IN_TM == 0 and seq % IN_TM == 0
    tiles_per_seq = seq // IN_TM
    vmem = (2 * IN_TM * D_MODEL * 4 + IN_TM * D_MODEL * 2 + D_MODEL * IN_WIDTH * 2
            + 2 * IN_TM * IN_WIDTH * 2 + 4 * IN_TM * HEAD_DIM * 4)
    resident = pl.Buffered(1)
    return pl.pallas_call(
        _in_proj_kernel,
        out_shape=jax.ShapeDtypeStruct((n, IN_WIDTH), BF16),
        grid=(n // IN_TM,),
        in_specs=[
            pl.BlockSpec((IN_TM, D_MODEL), lambda i: (i, 0)),
            pl.BlockSpec((1, D_MODEL), lambda i: (0, 0)),
            pl.BlockSpec((D_MODEL, IN_WIDTH), lambda i: (0, 0), pipeline_mode=resident),
            pl.BlockSpec((IN_TM, HEAD_DIM), lambda i: (i % tiles_per_seq, 0)),
            pl.BlockSpec((IN_TM, HEAD_DIM), lambda i: (i % tiles_per_seq, 0)),
            pl.BlockSpec((1, HEAD_DIM), lambda i: (0, 0)),
            pl.BlockSpec((1, HEAD_DIM), lambda i: (0, 0)),
        ],
        out_specs=pl.BlockSpec((IN_TM, IN_WIDTH), lambda i: (i, 0)),
        scratch_shapes=[pltpu.VMEM((IN_TM, D_MODEL), BF16)],
        compiler_params=pltpu.CompilerParams(
            dimension_semantics=("arbitrary",),
            vmem_limit_bytes=_vmem_limit(vmem)),
        name="in_proj",
    )(x2d, norm_w, w_in, cos_t, sin_t, q_gain, k_gain)


def _retention_kernel(dec_ref, q_ref, k_ref, v_ref, g_ref, gw_ref, gb_ref,
                      wout_ref, wq_ref, wco_ref, o_ref, wout_o_ref, wq_o_ref, wco_o_ref,
                      d_ref, tab_ref, sb_ref, st_ref, *, n_seq):
    _cast_slabs((wout_ref, wq_ref, wco_ref), (wout_o_ref, wq_o_ref, wco_o_ref))
    s = pl.program_id(2)
    c_len = RET_C
    cps = RET_TS // RET_C
    contract_rows = (((0,), (0,)), ((), ()))
    contract_cols = (((1,), (1,)), ((), ()))
    head_cols = [slice(h * HEAD_DIM, (h + 1) * HEAD_DIM) for h in range(RET_HG)]
    chunk_rows = [slice(c * c_len, (c + 1) * c_len) for c in range(cps)]

    def log_gammas(h):
        return -jnp.exp(dec_ref[h, 0:1, :]), -jnp.exp(dec_ref[h, 1:2, :])

    @pl.when(s == 0)
    def _():
        row = lax.broadcasted_iota(jnp.int32, (c_len, c_len), 0)
        col = lax.broadcasted_iota(jnp.int32, (c_len, c_len), 1)
        rel = (row - col).astype(F32)
        idx = lax.broadcasted_iota(jnp.int32, (c_len, HEAD_DIM), 0).astype(F32)
        for h in range(RET_HG):
            lg_f, lg_b = log_gammas(h)
            d_ref[h] = jnp.exp(jnp.where(rel >= 0, lg_f * rel, -lg_b * rel))
            lf = lg_f[:, :HEAD_DIM]
            lb = lg_b[:, :HEAD_DIM]
            tab_ref[h, 0] = jnp.exp(lf * (idx + 1.0)).astype(BF16)
            tab_ref[h, 1] = jnp.exp(lb * (c_len - idx)).astype(BF16)
            tab_ref[h, 2] = jnp.exp(lf * (c_len - 1.0 - idx)).astype(BF16)
            tab_ref[h, 3] = jnp.exp(lb * idx).astype(BF16)
        st_ref[...] = jnp.zeros(st_ref.shape, F32)

    @pl.when(s < n_seq)
    def _():
        first_chunk = (n_seq - 1 - s) * cps
        for h in range(RET_HG):
            chunk_b = jnp.exp(log_gammas(h)[1][:, :HEAD_DIM] * c_len)
            state = st_ref[h]
            for c in reversed(range(cps)):
                sb_ref[first_chunk + c, h] = state.astype(BF16)
                k = k_ref[chunk_rows[c], head_cols[h]]
                kd = k * tab_ref[h, 3]
                upd = lax.dot_general(kd, v_ref[chunk_rows[c], head_cols[h]], contract_rows,
                                      preferred_element_type=F32)
                state = state * chunk_b + upd
            st_ref[h] = state

        @pl.when(s == n_seq - 1)
        def _():
            st_ref[...] = jnp.zeros(st_ref.shape, F32)

    @pl.when(s >= n_seq)
    def _():
        first_chunk = (s - n_seq) * cps
        heads = range(RET_HG)
        chunk_f = [jnp.exp(log_gammas(h)[0][:, :HEAD_DIM] * c_len) for h in heads]
        state = [st_ref[h] for h in heads]
        for c in range(cps):
            blk = [(chunk_rows[c], head_cols[h]) for h in heads]
            scores = [lax.dot_general(q_ref[blk[h]], k_ref[blk[h]], contract_cols,
                                      preferred_element_type=F32) for h in heads]
            out = []
            for h in heads:
                q = q_ref[blk[h]]
                masked = (scores[h] * d_ref[h]).astype(BF16)
                lhs = jnp.concatenate([masked, q * tab_ref[h, 0], q * tab_ref[h, 1]], axis=1)
                rhs = jnp.concatenate(
                    [v_ref[blk[h]], state[h].astype(BF16), sb_ref[first_chunk + c, h]], axis=0)
                out.append(jnp.dot(lhs, rhs, preferred_element_type=F32))
            upd = [lax.dot_general(k_ref[blk[h]] * tab_ref[h, 2], v_ref[blk[h]], contract_rows,
                                   preferred_element_type=F32) for h in heads]
            state = [state[h] * chunk_f[h] + upd[h] for h in heads]
            cen = [out[h] - jnp.mean(out[h], axis=-1, keepdims=True) for h in heads]
            var = [jnp.mean(cen[h] * cen[h], axis=-1, keepdims=True) for h in heads]
            for h in heads:
                y = (cen[h] * lax.rsqrt(var[h] + NORM_EPS) * gw_ref[:, head_cols[h]]
                     + gb_ref[:, head_cols[h]])
                g = g_ref[blk[h]].astype(F32)
                y = y * (g / (1.0 + jnp.exp(-g)))
                o_ref[blk[h]] = y.astype(BF16)
        for h in heads:
            st_ref[h] = state[h]


def _retention(proj, dec, gn_w, gn_b, cross_weights, batch, seq):
    n = proj.shape[0]
    assert seq % RET_TS == 0 and RET_TS % RET_C == 0 and RET_HEADS % RET_HG == 0
    n_seq = seq // RET_TS
    n_chunks = seq // RET_C
    group_w = RET_HG * HEAD_DIM
    groups = RET_HEADS // RET_HG

    def kv_block(section):
        def index(b, hg, s):
            blk = jnp.where(s < n_seq, n_seq - 1 - s, s - n_seq)
            return (b * n_seq + blk, section * groups + hg)
        return index

    def fwd_block(section):
        return lambda b, hg, s: (b * n_seq + jnp.maximum(s - n_seq, 0), section * groups + hg)

    cast_specs, cast_shapes, cast_bytes = _slab_cast_specs(
        cross_weights, batch * groups * 2 * n_seq,
        lambda b, hg, s: (b * groups + hg) * 2 * n_seq + s)
    vmem = (2 * 5 * RET_TS * group_w * 2 + RET_HG * RET_C * RET_C * 4
            + RET_HG * 4 * RET_C * HEAD_DIM * 4 + n_chunks * RET_HG * HEAD_DIM * HEAD_DIM * 2
            + RET_HG * HEAD_DIM * HEAD_DIM * 4 + cast_bytes)
    y_ret, *cast = pl.pallas_call(
        functools.partial(_retention_kernel, n_seq=n_seq),
        out_shape=(jax.ShapeDtypeStruct((n, RET_WIDTH), BF16), *cast_shapes),
        grid=(batch, groups, 2 * n_seq),
        in_specs=[
            pl.BlockSpec((RET_HG, 2, RET_C), lambda b, hg, s: (hg, 0, 0)),
            pl.BlockSpec((RET_TS, group_w), fwd_block(0)),
            pl.BlockSpec((RET_TS, group_w), kv_block(1)),
            pl.BlockSpec((RET_TS, group_w), kv_block(2)),
            pl.BlockSpec((RET_TS, group_w), fwd_block(3)),
            pl.BlockSpec((1, group_w), lambda b, hg, s: (0, hg)),
            pl.BlockSpec((1, group_w), lambda b, hg, s: (0, hg)),
            *cast_specs,
        ],
        out_specs=(pl.BlockSpec((RET_TS, group_w), fwd_block(0)), *cast_specs),
        scratch_shapes=[
            pltpu.VMEM((RET_HG, RET_C, RET_C), F32),
            pltpu.VMEM((RET_HG, 4, RET_C, HEAD_DIM), BF16),
            pltpu.VMEM((n_chunks, RET_HG, HEAD_DIM, HEAD_DIM), BF16),
            pltpu.VMEM((RET_HG, HEAD_DIM, HEAD_DIM), F32),
        ],
        compiler_params=pltpu.CompilerParams(
            dimension_semantics=("arbitrary", "arbitrary", "arbitrary"),
            vmem_limit_bytes=_vmem_limit(vmem)),
        name="retention",
    )(dec, proj, proj, proj, proj, gn_w, gn_b, *cross_weights)
    return y_ret, cast


def _attention_kernel(q_ref, k_ref, v_ref, bound_ref, wu_ref, wd_ref, o_ref, wu_o_ref, wd_o_ref,
                      vt_ref, qa_ref, acc_ref, den_ref, *, n_kv):
    _cast_slabs((wu_ref, wd_ref), (wu_o_ref, wd_o_ref))

    i = pl.program_id(2)
    contract_cols = (((1,), (1,)), ((), ()))

    def kv_rows(c):
        return pl.ds(pl.multiple_of(c * ATT_TK, ATT_TK), ATT_TK)

    @pl.when(i == 0)
    def _():
        def body(c, carry):
            vt_ref[c] = v_ref[kv_rows(c), :].astype(F32).T.astype(BF16)
            return carry
        lax.fori_loop(0, n_kv, body, 0)

    q_lane = lax.broadcasted_iota(jnp.int32, (ATT_TQ, HEAD_DIM), 1)
    k_lane = lax.broadcasted_iota(jnp.int32, (ATT_TK, HEAD_DIM), 1)
    k_one = jnp.where(k_lane == 0, 1.0, 0.0).astype(BF16)
    head_cols = [slice(h * HEAD_DIM, (h + 1) * HEAD_DIM) for h in range(GQA_GROUP)]

    def shift_column(shift):
        return jnp.where(q_lane == 0, -shift, 0.0).astype(BF16)

    def sweep(q_aug):
        acc_ref[...] = jnp.zeros(acc_ref.shape, F32)
        den_ref[...] = jnp.zeros(den_ref.shape, F32)

        def kv_step(c, carry):
            k_aug = jnp.concatenate([k_ref[kv_rows(c), :], k_one], axis=1)
            vt = vt_ref[c]
            for h in range(GQA_GROUP):
                s_t = lax.dot_general(k_aug, q_aug(h), contract_cols,
                                      preferred_element_type=F32)
                p_t = jnp.exp2(s_t)
                den_ref[h] += jnp.sum(p_t.reshape(ATT_TK // 8, 8, ATT_TQ), axis=0)
                acc_ref[h] += jnp.dot(vt, p_t.astype(BF16), preferred_element_type=F32)
            return carry
        lax.fori_loop(0, n_kv, kv_step, 0)

    bound_column = shift_column(bound_ref[...])
    sweep(lambda h: jnp.concatenate([q_ref[:, head_cols[h]], bound_column], axis=1))

    def denominator(h):
        return jnp.sum(den_ref[h], axis=0, keepdims=True)

    denom_min = functools.reduce(jnp.minimum,
                                 [jnp.min(denominator(h)) for h in range(GQA_GROUP)])

    @pl.when(denom_min < ATT_DENOM_FLOOR)
    def _():
        for h in range(GQA_GROUP):
            def body(c, m):
                s = lax.dot_general(q_ref[:, head_cols[h]], k_ref[kv_rows(c), :], contract_cols,
                                    preferred_element_type=F32)
                return jnp.maximum(m, jnp.max(s, axis=1, keepdims=True))
            row_max = lax.fori_loop(0, n_kv, body, jnp.full((ATT_TQ, 1), NEG_BIG, F32))
            qa_ref[h, :, 0:HEAD_DIM] = q_ref[:, head_cols[h]]
            qa_ref[h, :, HEAD_DIM:] = shift_column(row_max)
        sweep(lambda h: qa_ref[h])

    for h in range(GQA_GROUP):
        o_t = acc_ref[h] * (1.0 / denominator(h))
        o_ref[:, head_cols[h]] = o_t.T.astype(BF16)


def _attention(proj, score_bound, w_up, w_down, batch, seq):
    n = proj.shape[0]
    assert seq % ATT_TQ == 0 and seq % ATT_TK == 0
    n_q = seq // ATT_TQ
    n_kv = seq // ATT_TK
    group_w = GQA_GROUP * HEAD_DIM
    q_col0 = 4 * RET_WIDTH // group_w
    k_col0 = (4 * RET_WIDTH + ATTN_WIDTH) // HEAD_DIM
    v_col0 = k_col0 + ATTN_KV_HEADS
    cast_specs, cast_shapes, cast_bytes = _slab_cast_specs(
        (w_up, w_down), batch * ATTN_KV_HEADS * n_q,
        lambda b, g, i: (b * ATTN_KV_HEADS + g) * n_q + i)
    vmem = (2 * ATT_TQ * group_w * 2 * 2 + 2 * 2 * seq * HEAD_DIM * 2
            + HEAD_DIM * seq * 2 + GQA_GROUP * (HEAD_DIM + 8) * ATT_TQ * 4
            + GQA_GROUP * ATT_TQ * 2 * HEAD_DIM * 2 + 2 * GQA_GROUP * ATT_TK * ATT_TQ * 4
            + cast_bytes)
    return pl.pallas_call(
        functools.partial(_attention_kernel, n_kv=n_kv),
        out_shape=(jax.ShapeDtypeStruct((n, ATTN_WIDTH), BF16), *cast_shapes),
        grid=(batch, ATTN_KV_HEADS, n_q),
        in_specs=[
            pl.BlockSpec((ATT_TQ, group_w), lambda b, g, i: (b * n_q + i, q_col0 + g)),
            pl.BlockSpec((seq, HEAD_DIM), lambda b, g, i: (b, k_col0 + g)),
            pl.BlockSpec((seq, HEAD_DIM), lambda b, g, i: (b, v_col0 + g)),
            pl.BlockSpec((1, HEAD_DIM), lambda b, g, i: (0, 0)),
            *cast_specs,
        ],
        out_specs=(pl.BlockSpec((ATT_TQ, group_w), lambda b, g, i: (b * n_q + i, g)),
                   *cast_specs),
        scratch_shapes=[
            pltpu.VMEM((n_kv, HEAD_DIM, ATT_TK), BF16),
            pltpu.VMEM((GQA_GROUP, ATT_TQ, 2 * HEAD_DIM), BF16),
            pltpu.VMEM((GQA_GROUP, HEAD_DIM, ATT_TQ), F32),
            pltpu.VMEM((GQA_GROUP, 8, ATT_TQ), F32),
        ],
        compiler_params=pltpu.CompilerParams(
            dimension_semantics=("arbitrary", "arbitrary", "arbitrary"),
            vmem_limit_bytes=_vmem_limit(vmem)),
        name="attention",
    )(proj, proj, proj, score_bound, w_up, w_down)


def _mem_kv_kernel(m_ref, nw_ref, wk_ref, wv_ref, o_ref):
    m = m_ref[...]
    h = (m * _rms_scale(m) * nw_ref[...]).astype(BF16)
    o_ref[:, :CROSS_WIDTH] = jnp.dot(h, wk_ref[...].astype(BF16),
                                     preferred_element_type=F32).astype(BF16)
    o_ref[:, CROSS_WIDTH:] = jnp.dot(h, wv_ref[...].astype(BF16),
                                     preferred_element_type=F32).astype(BF16)


def _mem_kv(mem2d, norm_w, w_k, w_v):
    rows = mem2d.shape[0]
    vmem = 2 * (rows * D_MODEL * 4 + 2 * D_MODEL * CROSS_WIDTH * 4 + rows * 2 * CROSS_WIDTH * 2)
    return pl.pallas_call(
        _mem_kv_kernel,
        out_shape=jax.ShapeDtypeStruct((rows, 2 * CROSS_WIDTH), BF16),
        grid=(1,),
        in_specs=[
            pl.BlockSpec((rows, D_MODEL), lambda i: (0, 0)),
            pl.BlockSpec((1, D_MODEL), lambda i: (0, 0)),
            pl.BlockSpec((D_MODEL, CROSS_WIDTH), lambda i: (0, 0)),
            pl.BlockSpec((D_MODEL, CROSS_WIDTH), lambda i: (0, 0)),
        ],
        out_specs=pl.BlockSpec((rows, 2 * CROSS_WIDTH), lambda i: (0, 0)),
        compiler_params=pltpu.CompilerParams(
            dimension_semantics=("arbitrary",),
            vmem_limit_bytes=_vmem_limit(vmem)),
        name="mem_kv",
    )(mem2d, norm_w, w_k, w_v)


def _out_cross_kernel(x_ref, yr_ref, ya_ref, wo_ref, nw_ref, wq_ref, kv_ref, wco_ref, o_ref):
    contract_cols = (((1,), (1,)), ((), ()))
    scale = CROSS_HEAD_DIM ** -0.5

    for r in range(OC_TM // OC_RC):
        rows = slice(r * OC_RC, (r + 1) * OC_RC)
        x1 = (x_ref[rows, :]
              + jnp.dot(yr_ref[rows, :], wo_ref[0:RET_WIDTH, :], preferred_element_type=F32)
              + jnp.dot(ya_ref[rows, :], wo_ref[RET_WIDTH:, :], preferred_element_type=F32))
        h = (x1 * _rms_scale(x1) * nw_ref[...]).astype(BF16)
        q = (jnp.dot(h, wq_ref[...], preferred_element_type=F32) * scale).astype(BF16)
        heads = range(CROSS_HEADS)
        cols = [slice(hd * CROSS_HEAD_DIM, (hd + 1) * CROSS_HEAD_DIM) for hd in heads]
        s = [lax.dot_general(q[:, cols[hd]], kv_ref[:, cols[hd]], contract_cols,
                             preferred_element_type=F32) for hd in heads]
        e = [jnp.exp(s[hd] - jnp.max(s[hd], axis=-1, keepdims=True)) for hd in heads]
        inv = [1.0 / jnp.sum(e[hd], axis=-1, keepdims=True) for hd in heads]
        pv = [jnp.dot(e[hd].astype(BF16),
                      kv_ref[:, CROSS_WIDTH + hd * CROSS_HEAD_DIM:CROSS_WIDTH + (hd + 1) * CROSS_HEAD_DIM],
                      preferred_element_type=F32) for hd in heads]
        o = jnp.concatenate([(pv[hd] * inv[hd]).astype(BF16) for hd in heads], axis=1)
        o_ref[rows, :] = x1 + jnp.dot(o, wco_ref[...], preferred_element_type=F32)


def _out_cross(x2d, y_ret, y_attn, w_out, norm_w, w_q, mem_kv, w_co, seq):
    n = x2d.shape[0]
    assert n % OC_TM == 0 and seq % OC_TM == 0
    tiles_per_seq = seq // OC_TM
    vmem = (2 * 2 * OC_TM * D_MODEL * 4 + 2 * 2 * OC_TM * RET_WIDTH * 2 + 2 * D_MODEL * D_MODEL * 2
            + 2 * 2 * D_MODEL * CROSS_WIDTH * 2 + 2 * MEM_TOKENS * 2 * CROSS_WIDTH * 2)
    return pl.pallas_call(
        _out_cross_kernel,
        out_shape=jax.ShapeDtypeStruct((n, D_MODEL), F32),
        grid=(n // OC_TM,),
        in_specs=[
            pl.BlockSpec((OC_TM, D_MODEL), lambda i: (i, 0)),
            pl.BlockSpec((OC_TM, RET_WIDTH), lambda i: (i, 0)),
            pl.BlockSpec((OC_TM, ATTN_WIDTH), lambda i: (i, 0)),
            pl.BlockSpec((D_MODEL, D_MODEL), lambda i: (0, 0)),
            pl.BlockSpec((1, D_MODEL), lambda i: (0, 0)),
            pl.BlockSpec((D_MODEL, CROSS_WIDTH), lambda i: (0, 0)),
            pl.BlockSpec((MEM_TOKENS, 2 * CROSS_WIDTH), lambda i: (i // tiles_per_seq, 0)),
            pl.BlockSpec((CROSS_WIDTH, D_MODEL), lambda i: (0, 0)),
        ],
        out_specs=pl.BlockSpec((OC_TM, D_MODEL), lambda i: (i, 0)),
        compiler_params=pltpu.CompilerParams(
            dimension_semantics=("arbitrary",),
            vmem_limit_bytes=_vmem_limit(vmem)),
        name="out_cross",
    )(x2d, y_ret, y_attn, w_out, norm_w, w_q, mem_kv, w_co)


def _mlp_kernel(x_hbm, nw_ref, wu_ref, wd_ref, fw_ref, o_ref, h_ref, x_ref, x_sem):
    i = pl.program_id(0)
    j = pl.program_id(1)
    last = pl.num_programs(1) - 1

    def x_copy(tile):
        rows = pl.ds(pl.multiple_of(tile * MLP_TM, MLP_TM), MLP_TM)
        return pltpu.make_async_copy(x_hbm.at[rows, :], x_ref, x_sem)

    @pl.when((i == 0) & (j == 0))
    def _():
        x_copy(0).start()

    @pl.when(j == 0)
    def _():
        x_copy(i).wait()

    @pl.when((j == 1) & (i + 1 < pl.num_programs(0)))
    def _():
        x_copy(i + 1).start()

    def step(rows, is_first, is_last):
        if is_first:
            xr = x_ref[rows, :]
            h = (xr * _rms_scale(xr) * nw_ref[...]).astype(BF16)
            h_ref[rows, :] = h
            base = xr
        else:
            h = h_ref[rows, :]
            base = o_ref[rows, :]
        u = jnp.maximum(jnp.dot(h, wu_ref[...], preferred_element_type=F32), 0.0)
        y = base + jnp.dot((u * u).astype(BF16), wd_ref[...], preferred_element_type=F32)
        if is_last:
            y = y * _rms_scale(y) * fw_ref[...]
        o_ref[rows, :] = y

    edge_chunks = [slice(r * MLP_EDGE_RC, (r + 1) * MLP_EDGE_RC)
                   for r in range(MLP_TM // MLP_EDGE_RC)]

    @pl.when(j == 0)
    def _():
        for rows in edge_chunks:
            step(rows, True, False)

    @pl.when((j > 0) & (j < last))
    def _():
        step(slice(None), False, False)

    @pl.when(j == last)
    def _():
        for rows in edge_chunks:
            step(rows, False, True)


def _mlp(x2d, norm_w, w_up, w_down, final_w):
    n = x2d.shape[0]
    assert n % MLP_TM == 0 and D_FF % MLP_TF == 0 and D_FF // MLP_TF >= 3
    vmem = (3 * MLP_TM * D_MODEL * 4 + MLP_TM * D_MODEL * 2 + 2 * 2 * D_MODEL * MLP_TF * 2)
    return pl.pallas_call(
        _mlp_kernel,
        out_shape=jax.ShapeDtypeStruct((n, D_MODEL), F32),
        grid=(n // MLP_TM, D_FF // MLP_TF),
        in_specs=[
            pl.BlockSpec(memory_space=pl.ANY),
            pl.BlockSpec((1, D_MODEL), lambda i, j: (0, 0)),
            pl.BlockSpec((D_MODEL, MLP_TF), lambda i, j: (0, j)),
            pl.BlockSpec((MLP_TF, D_MODEL), lambda i, j: (j, 0)),
            pl.BlockSpec((1, D_MODEL), lambda i, j: (0, 0)),
        ],
        out_specs=pl.BlockSpec((MLP_TM, D_MODEL), lambda i, j: (i, 0)),
        scratch_shapes=[
            pltpu.VMEM((MLP_TM, D_MODEL), BF16),
            pltpu.VMEM((MLP_TM, D_MODEL), F32),
            pltpu.SemaphoreType.DMA(()),
        ],
        compiler_params=pltpu.CompilerParams(
            dimension_semantics=("arbitrary", "arbitrary"),
            vmem_limit_bytes=_vmem_limit(vmem)),
        name="mlp",
    )(x2d, norm_w, w_up, w_down, final_w)


def _rope_tables(seq):
    t = np.arange(seq)
    row = (t // GRID_W).astype(np.float64)
    col = (t % GRID_W).astype(np.float64)
    inv_freq = 1.0 / (ROPE_THETA ** (np.arange(0, AXIS_DIM, 2, dtype=np.float64) / AXIS_DIM))
    ang_r = row[:, None] * inv_freq[None, :]
    ang_c = col[:, None] * inv_freq[None, :]
    cos_t = np.concatenate([np.cos(ang_r), np.cos(ang_c), np.cos(ang_r), np.cos(ang_c)], axis=-1)
    sin_t = np.concatenate([-np.sin(ang_r), -np.sin(ang_c), np.sin(ang_r), np.sin(ang_c)], axis=-1)
    return jnp.asarray(cos_t, F32), jnp.asarray(sin_t, F32)


def _pair_heads(w):
    lead = w.shape[:-1]
    heads = w.shape[-1] // HEAD_DIM
    quarter = HEAD_DIM // 4
    w = w.reshape(lead + (heads, 2, 2, quarter))
    return jnp.swapaxes(w, -2, -3).reshape(lead + (heads * HEAD_DIM,))


def _w_in_prep_kernel(w_ref, o_ref):
    lane = lax.broadcasted_iota(jnp.int32, (WPREP_ROWS, HEAD_DIM), 1)
    quarter = HEAD_DIM // 4
    takes_c1 = (lane >= quarter) & (lane < 2 * quarter)
    takes_r2 = (lane >= 2 * quarter) & (lane < 3 * quarter)
    rope_slabs = (set(range(0, 2 * RET_HEADS))
                  | set(range(4 * RET_HEADS, 4 * RET_HEADS + ATTN_HEADS + ATTN_KV_HEADS)))
    for s in range(IN_WIDTH // HEAD_DIM):
        cols = slice(s * HEAD_DIM, (s + 1) * HEAD_DIM)
        y = w_ref[:, cols]
        if s in rope_slabs:
            y = jnp.where(takes_c1, pltpu.roll(y, HEAD_DIM - quarter, 1),
                          jnp.where(takes_r2, pltpu.roll(y, quarter, 1), y))
        o_ref[:, cols] = y.astype(BF16)


def _in_proj_weights(w_in):
    rows = w_in.shape[0]
    assert rows % WPREP_ROWS == 0
    vmem = 2 * WPREP_ROWS * IN_WIDTH * (4 + 2)
    return pl.pallas_call(
        _w_in_prep_kernel,
        out_shape=jax.ShapeDtypeStruct((rows, IN_WIDTH), BF16),
        grid=(rows // WPREP_ROWS,),
        in_specs=[pl.BlockSpec((WPREP_ROWS, IN_WIDTH), lambda i: (i, 0))],
        out_specs=pl.BlockSpec((WPREP_ROWS, IN_WIDTH), lambda i: (i, 0)),
        compiler_params=pltpu.CompilerParams(
            dimension_semantics=("arbitrary",),
            vmem_limit_bytes=_vmem_limit(vmem)),
        name="w_in_prep",
    )(w_in)


def kernel(x, mem, norm_mix_w, w_in, ret_decay_fwd, ret_decay_bwd, ret_gn_w, ret_gn_b, attn_q_norm_w, attn_k_norm_w, w_out, norm_cross_w, norm_mem_w, w_cross_q, w_cross_k, w_cross_v, w_cross_o, norm_mlp_w, w_mlp_up, w_mlp_down, norm_final_w):
    batch, seq, _ = x.shape
    assert w_in.shape[0] == 1, "single-layer block: per-layer parameters have a leading axis of 1"
    cos_t, sin_t = _rope_tables(seq)
    xs = x.reshape(batch * seq, D_MODEL)
    mem2d = mem.reshape(batch * MEM_TOKENS, D_MODEL)
    dec = jnp.stack([ret_decay_fwd[0], ret_decay_bwd[0]], axis=1)
    dec = jnp.broadcast_to(dec[:, :, None], (RET_HEADS, 2, RET_C)).astype(F32)
    proj = _in_proj(xs, norm_mix_w, _in_proj_weights(w_in[0]), cos_t, sin_t,
                    _pair_heads(attn_q_norm_w), _pair_heads(attn_k_norm_w), seq)
    y_ret, (w_out_bf16, w_cq_bf16, w_co_bf16) = _retention(
        proj, dec, ret_gn_w, ret_gn_b, (w_out[0], w_cross_q[0], w_cross_o[0]), batch, seq)
    score_bound = (LOG2E * math.sqrt(HEAD_DIM) * ATT_BOUND_MARGIN
                   * jnp.max(jnp.abs(attn_q_norm_w)) * jnp.max(jnp.abs(attn_k_norm_w)))
    score_bound = jnp.full((1, HEAD_DIM), score_bound, F32)
    y_attn, w_up_bf16, w_down_bf16 = _attention(proj, score_bound, w_mlp_up[0], w_mlp_down[0],
                                                batch, seq)
    mem_kv = _mem_kv(mem2d, norm_mem_w, w_cross_k[0], w_cross_v[0])
    xs = _out_cross(xs, y_ret, y_attn, w_out_bf16, norm_cross_w, w_cq_bf16, mem_kv, w_co_bf16, seq)
    xs = _mlp(xs, norm_mlp_w, w_up_bf16, w_down_bf16, norm_final_w[None, :])
    return xs.reshape(batch, seq, D_MODEL)
```

```python
import functools
import math

import numpy as np
import jax
import jax.numpy as jnp
from jax import lax
from jax.experimental import pallas as pl
from jax.experimental.pallas import tpu as pltpu

D_MODEL = 2048
HEAD_DIM = 128
RET_WIDTH = 1024
ATTN_WIDTH = 1024
RET_HEADS = 8
ATTN_HEADS = 8
ATTN_KV_HEADS = 2
GQA_GROUP = ATTN_HEADS // ATTN_KV_HEADS
KV_WIDTH = ATTN_KV_HEADS * HEAD_DIM
IN_WIDTH = 4 * RET_WIDTH + ATTN_WIDTH + 2 * KV_WIDTH
GRID_W = 64
AXIS_DIM = HEAD_DIM // 2
ROPE_THETA = 10000.0
MEM_TOKENS = 256
CROSS_HEADS = 4
CROSS_HEAD_DIM = 128
CROSS_WIDTH = CROSS_HEADS * CROSS_HEAD_DIM
D_FF = 4 * D_MODEL
NORM_EPS = 1e-6

V7X_LANES = 128
V7X_BF16_SUBLANES = 16
V7X_VMEM_BYTES = 64 * 1024 * 1024

F32 = jnp.float32
BF16 = jnp.bfloat16
LOG2E = math.log2(math.e)
NEG_BIG = -1e30

IN_TM = 512
IN_TN = 512
IN_NORM_RC = 256
WPREP_ROWS = 256
RET_C = 256
RET_TS = 2048
RET_HG = 4
ATT_TQ = 512
ATT_TK = 4096
ATT_DENOM_FLOOR = 2.0 ** -40
ATT_BOUND_MARGIN = 1.0 + 2.0 ** -6
OC_TM = 512
OC_RC = 512
MLP_TM = 1024
MLP_TF = 1024
MLP_EDGE_RC = 512


def _vmem_limit(nbytes):
    return int(min(nbytes + 16 * 1024 * 1024, V7X_VMEM_BYTES - 8 * 1024 * 1024))


def _rms_scale(y):
    return lax.rsqrt(jnp.mean(y * y, axis=-1, keepdims=True) + NORM_EPS)


def _slab_cast_specs(weights, steps, flat_step):
    specs, shapes, nbytes = [], [], 0
    for w in weights:
        assert w.shape[0] % steps == 0 and (w.shape[0] // steps) % V7X_BF16_SUBLANES == 0
        block = (w.shape[0] // steps, w.shape[1])
        specs.append(pl.BlockSpec(block, lambda *ids: (flat_step(*ids), 0)))
        shapes.append(jax.ShapeDtypeStruct(w.shape, BF16))
        nbytes += 2 * block[0] * block[1] * (4 + 2)
    return specs, shapes, nbytes


def _cast_slabs(src_refs, dst_refs):
    for src, dst in zip(src_refs, dst_refs):
        dst[...] = src[...].astype(BF16)


def _rope(y, cos, sin_signed):
    return y * cos + pltpu.roll(y, HEAD_DIM // 2, 1) * sin_signed


def _in_proj_kernel(x_ref, nw_ref, w_ref, cos_ref, sin_ref, qg_ref, kg_ref, o_ref, h_ref):
    for r in range(IN_TM // IN_NORM_RC):
        rows = slice(r * IN_NORM_RC, (r + 1) * IN_NORM_RC)
        xr = x_ref[rows, :]
        h_ref[rows, :] = (xr * _rms_scale(xr) * nw_ref[...]).astype(BF16)

    ret_k_scale = HEAD_DIM ** -0.5
    attn_q_scale = (HEAD_DIM ** -0.5) * LOG2E

    n_slabs = IN_TN // HEAD_DIM

    def run(j, epilogue):
        tile = slice(j * IN_TN, (j + 1) * IN_TN)
        acc = jnp.dot(h_ref[...], w_ref[:, tile], preferred_element_type=F32)
        slabs = epilogue([acc[:, s * HEAD_DIM:(s + 1) * HEAD_DIM] for s in range(n_slabs)],
                         cos_ref[...], sin_ref[...])
        for s, y in enumerate(slabs):
            c0 = j * IN_TN + s * HEAD_DIM
            o_ref[:, c0:c0 + HEAD_DIM] = y.astype(BF16)

    def ret_q(ys, cos, sin):
        return [_rope(y, cos, sin) for y in ys]

    def ret_k(ys, cos, sin):
        return [_rope(y, cos, sin) * ret_k_scale for y in ys]

    def plain(ys, cos, sin):
        return ys

    def head_inv_rms(y_a, y_b):
        width = 2 * HEAD_DIM
        r = lax.broadcasted_iota(jnp.int32, (width, width), 0) // HEAD_DIM
        c = lax.broadcasted_iota(jnp.int32, (width, width), 1) // HEAD_DIM
        ones_bd = jnp.where(r == c, 1.0, 0.0).astype(BF16)
        sq = jnp.concatenate([(y_a * y_a).astype(BF16), (y_b * y_b).astype(BF16)], axis=1)
        ms = jnp.dot(sq, ones_bd, preferred_element_type=F32) * (1.0 / HEAD_DIM)
        inv = lax.rsqrt(ms + NORM_EPS)
        return inv[:, :HEAD_DIM], inv[:, HEAD_DIM:]

    def normed_rope(y_a, y_b, gain, scale, cos, sin):
        inv_a, inv_b = head_inv_rms(y_a, y_b)
        return [_rope(y_a * gain, cos, sin) * (inv_a * scale),
                _rope(y_b * gain, cos, sin) * (inv_b * scale)]

    def attn_q(ys, cos, sin):
        gain = qg_ref[...]
        return (normed_rope(ys[0], ys[1], gain, attn_q_scale, cos, sin)
                + normed_rope(ys[2], ys[3], gain, attn_q_scale, cos, sin))

    def attn_kv(ys, cos, sin):
        return normed_rope(ys[0], ys[1], kg_ref[...], 1.0, cos, sin) + ys[ATTN_KV_HEADS:]

    nb = RET_WIDTH // IN_TN
    for j in range(IN_WIDTH // IN_TN):
        if j < nb:
            run(j, ret_q)
        elif j < 2 * nb:
            run(j, ret_k)
        elif j < 4 * nb:
            run(j, plain)
        elif j < 5 * nb:
            run(j, attn_q)
        else:
            run(j, attn_kv)


def _in_proj(x2d, norm_w, w_in, cos_t, sin_t, q_gain, k_gain, seq):
    n = x2d.shape[0]
    assert IN_WIDTH % IN_TN == 0 and 2 * KV_WIDTH == IN_TN and n % IN_TM == 0 and seq % IN_TM == 0
    tiles_per_seq = seq // IN_TM
    vmem = (2 * IN_TM * D_MODEL * 4 + IN_TM * D_MODEL * 2 + D_MODEL * IN_WIDTH * 2
            + 2 * IN_TM * IN_WIDTH * 2 + 4 * IN_TM * HEAD_DIM * 4)
    resident = pl.Buffered(1)
    return pl.pallas_call(
        _in_proj_kernel,
        out_shape=jax.ShapeDtypeStruct((n, IN_WIDTH), BF16),
        grid=(n // IN_TM,),
        in_specs=[
            pl.BlockSpec((IN_TM, D_MODEL), lambda i: (i, 0)),
            pl.BlockSpec((1, D_MODEL), lambda i: (0, 0)),
            pl.BlockSpec((D_MODEL, IN_WIDTH), lambda i: (0, 0), pipeline_mode=resident),
            pl.BlockSpec((IN_TM, HEAD_DIM), lambda i: (i % tiles_per_seq, 0)),
            pl.BlockSpec((IN_TM, HEAD_DIM), lambda i: (i % tiles_per_seq, 0)),
            pl.BlockSpec((1, HEAD_DIM), lambda i: (0, 0)),
            pl.BlockSpec((1, HEAD_DIM), lambda i: (0, 0)),
        ],
        out_specs=pl.BlockSpec((IN_TM, IN_WIDTH), lambda i: (i, 0)),
        scratch_shapes=[pltpu.VMEM((IN_TM, D_MODEL), BF16)],
        compiler_params=pltpu.CompilerParams(
            dimension_semantics=("arbitrary",),
            vmem_limit_bytes=_vmem_limit(vmem)),
        name="in_proj",
    )(x2d, norm_w, w_in, cos_t, sin_t, q_gain, k_gain)


def _retention_kernel(dec_ref, q_ref, k_ref, v_ref, g_ref, gw_ref, gb_ref,
                      wout_ref, wq_ref, wco_ref, o_ref, wout_o_ref, wq_o_ref, wco_o_ref,
                      d_ref, tab_ref, sb_ref, st_ref, *, n_seq):
    _cast_slabs((wout_ref, wq_ref, wco_ref), (wout_o_ref, wq_o_ref, wco_o_ref))
    s = pl.program_id(2)
    c_len = RET_C
    cps = RET_TS // RET_C
    contract_rows = (((0,), (0,)), ((), ()))
    contract_cols = (((1,), (1,)), ((), ()))
    head_cols = [slice(h * HEAD_DIM, (h + 1) * HEAD_DIM) for h in range(RET_HG)]
    chunk_rows = [slice(c * c_len, (c + 1) * c_len) for c in range(cps)]

    def log_gammas(h):
        return -jnp.exp(dec_ref[h, 0:1, :]), -jnp.exp(dec_ref[h, 1:2, :])

    @pl.when(s == 0)
    def _():
        row = lax.broadcasted_iota(jnp.int32, (c_len, c_len), 0)
        col = lax.broadcasted_iota(jnp.int32, (c_len, c_len), 1)
        rel = (row - col).astype(F32)
        idx = lax.broadcasted_iota(jnp.int32, (c_len, HEAD_DIM), 0).astype(F32)
        for h in range(RET_HG):
            lg_f, lg_b = log_gammas(h)
            d_ref[h] = jnp.exp(jnp.where(rel >= 0, lg_f * rel, -lg_b * rel))
            lf = lg_f[:, :HEAD_DIM]
            lb = lg_b[:, :HEAD_DIM]
            tab_ref[h, 0] = jnp.exp(lf * (idx + 1.0)).astype(BF16)
            tab_ref[h, 1] = jnp.exp(lb * (c_len - idx)).astype(BF16)
            tab_ref[h, 2] = jnp.exp(lf * (c_len - 1.0 - idx)).astype(BF16)
            tab_ref[h, 3] = jnp.exp(lb * idx).astype(BF16)
        st_ref[...] = jnp.zeros(st_ref.shape, F32)

    @pl.when(s < n_seq)
    def _():
        first_chunk = (n_seq - 1 - s) * cps
        for h in range(RET_HG):
            chunk_b = jnp.exp(log_gammas(h)[1][:, :HEAD_DIM] * c_len)
            state = st_ref[h]
            for c in reversed(range(cps)):
                sb_ref[first_chunk + c, h] = state.astype(BF16)
                k = k_ref[chunk_rows[c], head_cols[h]]
                kd = k * tab_ref[h, 3]
                upd = lax.dot_general(kd, v_ref[chunk_rows[c], head_cols[h]], contract_rows,
                                      preferred_element_type=F32)
                state = state * chunk_b + upd
            st_ref[h] = state

        @pl.when(s == n_seq - 1)
        def _():
            st_ref[...] = jnp.zeros(st_ref.shape, F32)

    @pl.when(s >= n_seq)
    def _():
        first_chunk = (s - n_seq) * cps
        heads = range(RET_HG)
        chunk_f = [jnp.exp(log_gammas(h)[0][:, :HEAD_DIM] * c_len) for h in heads]
        state = [st_ref[h] for h in heads]
        for c in range(cps):
            blk = [(chunk_rows[c], head_cols[h]) for h in heads]
            scores = [lax.dot_general(q_ref[blk[h]], k_ref[blk[h]], contract_cols,
                                      preferred_element_type=F32) for h in heads]
            out = []
            for h in heads:
                q = q_ref[blk[h]]
                masked = (scores[h] * d_ref[h]).astype(BF16)
                lhs = jnp.concatenate([masked, q * tab_ref[h, 0], q * tab_ref[h, 1]], axis=1)
                rhs = jnp.concatenate(
                    [v_ref[blk[h]], state[h].astype(BF16), sb_ref[first_chunk + c, h]], axis=0)
                out.append(jnp.dot(lhs, rhs, preferred_element_type=F32))
            upd = [lax.dot_general(k_ref[blk[h]] * tab_ref[h, 2], v_ref[blk[h]], contract_rows,
                                   preferred_element_type=F32) for h in heads]
            state = [state[h] * chunk_f[h] + upd[h] for h in heads]
            cen = [out[h] - jnp.mean(out[h], axis=-1, keepdims=True) for h in heads]
            var = [jnp.mean(cen[h] * cen[h], axis=-1, keepdims=True) for h in heads]
            for h in heads:
                y = (cen[h] * lax.rsqrt(var[h] + NORM_EPS) * gw_ref[:, head_cols[h]]
                     + gb_ref[:, head_cols[h]])
                g = g_ref[blk[h]].astype(F32)
                y = y * (g / (1.0 + jnp.exp(-g)))
                o_ref[blk[h]] = y.astype(BF16)
        for h in heads:
            st_ref[h] = state[h]


def _retention(proj, dec, gn_w, gn_b, cross_weights, batch, seq):
    n = proj.shape[0]
    assert seq % RET_TS == 0 and RET_TS % RET_C == 0 and RET_HEADS % RET_HG == 0
    n_seq = seq // RET_TS
    n_chunks = seq // RET_C
    group_w = RET_HG * HEAD_DIM
    groups = RET_HEADS // RET_HG

    def kv_block(section):
        def index(b, hg, s):
            blk = jnp.where(s < n_seq, n_seq - 1 - s, s - n_seq)
            return (b * n_seq + blk, section * groups + hg)
        return index

    def fwd_block(section):
        return lambda b, hg, s: (b * n_seq + jnp.maximum(s - n_seq, 0), section * groups + hg)

    cast_specs, cast_shapes, cast_bytes = _slab_cast_specs(
        cross_weights, batch * groups * 2 * n_seq,
        lambda b, hg, s: (b * groups + hg) * 2 * n_seq + s)
    vmem = (2 * 5 * RET_TS * group_w * 2 + RET_HG * RET_C * RET_C * 4
            + RET_HG * 4 * RET_C * HEAD_DIM * 4 + n_chunks * RET_HG * HEAD_DIM * HEAD_DIM * 2
            + RET_HG * HEAD_DIM * HEAD_DIM * 4 + cast_bytes)
    y_ret, *cast = pl.pallas_call(
        functools.partial(_retention_kernel, n_seq=n_seq),
        out_shape=(jax.ShapeDtypeStruct((n, RET_WIDTH), BF16), *cast_shapes),
        grid=(batch, groups, 2 * n_seq),
        in_specs=[
            pl.BlockSpec((RET_HG, 2, RET_C), lambda b, hg, s: (hg, 0, 0)),
            pl.BlockSpec((RET_TS, group_w), fwd_block(0)),
            pl.BlockSpec((RET_TS, group_w), kv_block(1)),
            pl.BlockSpec((RET_TS, group_w), kv_block(2)),
            pl.BlockSpec((RET_TS, group_w), fwd_block(3)),
            pl.BlockSpec((1, group_w), lambda b, hg, s: (0, hg)),
            pl.BlockSpec((1, group_w), lambda b, hg, s: (0, hg)),
            *cast_specs,
        ],
        out_specs=(pl.BlockSpec((RET_TS, group_w), fwd_block(0)), *cast_specs),
        scratch_shapes=[
            pltpu.VMEM((RET_HG, RET_C, RET_C), F32),
            pltpu.VMEM((RET_HG, 4, RET_C, HEAD_DIM), BF16),
            pltpu.VMEM((n_chunks, RET_HG, HEAD_DIM, HEAD_DIM), BF16),
            pltpu.VMEM((RET_HG, HEAD_DIM, HEAD_DIM), F32),
        ],
        compiler_params=pltpu.CompilerParams(
            dimension_semantics=("arbitrary", "arbitrary", "arbitrary"),
            vmem_limit_bytes=_vmem_limit(vmem)),
        name="retention",
    )(dec, proj, proj, proj, proj, gn_w, gn_b, *cross_weights)
    return y_ret, cast


def _attention_kernel(q_ref, k_ref, v_ref, bound_ref, wu_ref, wd_ref, o_ref, wu_o_ref, wd_o_ref,
                      vt_ref, qa_ref, acc_ref, den_ref, *, n_kv):
    _cast_slabs((wu_ref, wd_ref), (wu_o_ref, wd_o_ref))

    i = pl.program_id(2)
    contract_cols = (((1,), (1,)), ((), ()))

    def kv_rows(c):
        return pl.ds(pl.multiple_of(c * ATT_TK, ATT_TK), ATT_TK)

    @pl.when(i == 0)
    def _():
        def body(c, carry):
            vt_ref[c] = v_ref[kv_rows(c), :].astype(F32).T.astype(BF16)
            return carry
        lax.fori_loop(0, n_kv, body, 0)

    q_lane = lax.broadcasted_iota(jnp.int32, (ATT_TQ, HEAD_DIM), 1)
    k_lane = lax.broadcasted_iota(jnp.int32, (ATT_TK, HEAD_DIM), 1)
    k_one = jnp.where(k_lane == 0, 1.0, 0.0).astype(BF16)
    head_cols = [slice(h * HEAD_DIM, (h + 1) * HEAD_DIM) for h in range(GQA_GROUP)]

    def shift_column(shift):
        return jnp.where(q_lane == 0, -shift, 0.0).astype(BF16)

    def sweep(q_aug):
        acc_ref[...] = jnp.zeros(acc_ref.shape, F32)
        den_ref[...] = jnp.zeros(den_ref.shape, F32)

        def kv_step(c, carry):
            k_aug = jnp.concatenate([k_ref[kv_rows(c), :], k_one], axis=1)
            vt = vt_ref[c]
            def scores(h):
                return lax.dot_general(k_aug, q_aug(h), contract_cols,
                                       preferred_element_type=F32)

            s_next = scores(0)
            for h in range(GQA_GROUP):
                s_t = s_next
                if h + 1 < GQA_GROUP:
                    s_next = scores(h + 1)
                p_t = jnp.exp2(s_t)
                den_ref[h] += jnp.sum(p_t.reshape(ATT_TK // 8, 8, ATT_TQ), axis=0)
                acc_ref[h] += jnp.dot(vt, p_t.astype(BF16), preferred_element_type=F32)
            return carry
        lax.fori_loop(0, n_kv, kv_step, 0)

    bound_column = shift_column(bound_ref[...])
    sweep(lambda h: jnp.concatenate([q_ref[:, head_cols[h]], bound_column], axis=1))

    def denominator(h):
        return jnp.sum(den_ref[h], axis=0, keepdims=True)

    denom_min = functools.reduce(jnp.minimum,
                                 [jnp.min(denominator(h)) for h in range(GQA_GROUP)])

    @pl.when(denom_min < ATT_DENOM_FLOOR)
    def _():
        for h in range(GQA_GROUP):
            def body(c, m):
                s = lax.dot_general(q_ref[:, head_cols[h]], k_ref[kv_rows(c), :], contract_cols,
                                    preferred_element_type=F32)
                return jnp.maximum(m, jnp.max(s, axis=1, keepdims=True))
            row_max = lax.fori_loop(0, n_kv, body, jnp.full((ATT_TQ, 1), NEG_BIG, F32))
            qa_ref[h, :, 0:HEAD_DIM] = q_ref[:, head_cols[h]]
            qa_ref[h, :, HEAD_DIM:] = shift_column(row_max)
        sweep(lambda h: qa_ref[h])

    inv_den = [1.0 / denominator(h) for h in range(GQA_GROUP)]
    o_t = [acc_ref[h] * inv_den[h] for h in range(GQA_GROUP)]
    for h in range(GQA_GROUP):
        o_ref[:, head_cols[h]] = o_t[h].T.astype(BF16)


def _attention(proj, score_bound, w_up, w_down, batch, seq):
    n = proj.shape[0]
    assert seq % ATT_TQ == 0 and seq % ATT_TK == 0
    n_q = seq // ATT_TQ
    n_kv = seq // ATT_TK
    group_w = GQA_GROUP * HEAD_DIM
    q_col0 = 4 * RET_WIDTH // group_w
    k_col0 = (4 * RET_WIDTH + ATTN_WIDTH) // HEAD_DIM
    v_col0 = k_col0 + ATTN_KV_HEADS
    cast_specs, cast_shapes, cast_bytes = _slab_cast_specs(
        (w_up, w_down), batch * ATTN_KV_HEADS * n_q,
        lambda b, g, i: (b * ATTN_KV_HEADS + g) * n_q + i)
    vmem = (2 * ATT_TQ * group_w * 2 * 2 + 2 * 2 * seq * HEAD_DIM * 2
            + HEAD_DIM * seq * 2 + GQA_GROUP * (HEAD_DIM + 8) * ATT_TQ * 4
            + GQA_GROUP * ATT_TQ * 2 * HEAD_DIM * 2 + 2 * GQA_GROUP * ATT_TK * ATT_TQ * 4
            + cast_bytes)
    return pl.pallas_call(
        functools.partial(_attention_kernel, n_kv=n_kv),
        out_shape=(jax.ShapeDtypeStruct((n, ATTN_WIDTH), BF16), *cast_shapes),
        grid=(batch, ATTN_KV_HEADS, n_q),
        in_specs=[
            pl.BlockSpec((ATT_TQ, group_w), lambda b, g, i: (b * n_q + i, q_col0 + g)),
            pl.BlockSpec((seq, HEAD_DIM), lambda b, g, i: (b, k_col0 + g)),
            pl.BlockSpec((seq, HEAD_DIM), lambda b, g, i: (b, v_col0 + g)),
            pl.BlockSpec((1, HEAD_DIM), lambda b, g, i: (0, 0)),
            *cast_specs,
        ],
        out_specs=(pl.BlockSpec((ATT_TQ, group_w), lambda b, g, i: (b * n_q + i, g)),
                   *cast_specs),
        scratch_shapes=[
            pltpu.VMEM((n_kv, HEAD_DIM, ATT_TK), BF16),
            pltpu.VMEM((GQA_GROUP, ATT_TQ, 2 * HEAD_DIM), BF16),
            pltpu.VMEM((GQA_GROUP, HEAD_DIM, ATT_TQ), F32),
            pltpu.VMEM((GQA_GROUP, 8, ATT_TQ), F32),
        ],
        compiler_params=pltpu.CompilerParams(
            dimension_semantics=("arbitrary", "arbitrary", "arbitrary"),
            vmem_limit_bytes=_vmem_limit(vmem)),
        name="attention",
    )(proj, proj, proj, score_bound, w_up, w_down)


def _mem_kv_kernel(m_ref, nw_ref, wk_ref, wv_ref, o_ref):
    m = m_ref[...]
    h = (m * _rms_scale(m) * nw_ref[...]).astype(BF16)
    o_ref[:, :CROSS_WIDTH] = jnp.dot(h, wk_ref[...].astype(BF16),
                                     preferred_element_type=F32).astype(BF16)
    o_ref[:, CROSS_WIDTH:] = jnp.dot(h, wv_ref[...].astype(BF16),
                                     preferred_element_type=F32).astype(BF16)


def _mem_kv(mem2d, norm_w, w_k, w_v):
    rows = mem2d.shape[0]
    vmem = 2 * (rows * D_MODEL * 4 + 2 * D_MODEL * CROSS_WIDTH * 4 + rows * 2 * CROSS_WIDTH * 2)
    return pl.pallas_call(
        _mem_kv_kernel,
        out_shape=jax.ShapeDtypeStruct((rows, 2 * CROSS_WIDTH), BF16),
        grid=(1,),
        in_specs=[
            pl.BlockSpec((rows, D_MODEL), lambda i: (0, 0)),
            pl.BlockSpec((1, D_MODEL), lambda i: (0, 0)),
            pl.BlockSpec((D_MODEL, CROSS_WIDTH), lambda i: (0, 0)),
            pl.BlockSpec((D_MODEL, CROSS_WIDTH), lambda i: (0, 0)),
        ],
        out_specs=pl.BlockSpec((rows, 2 * CROSS_WIDTH), lambda i: (0, 0)),
        compiler_params=pltpu.CompilerParams(
            dimension_semantics=("arbitrary",),
            vmem_limit_bytes=_vmem_limit(vmem)),
        name="mem_kv",
    )(mem2d, norm_w, w_k, w_v)


def _out_cross_kernel(x_ref, yr_ref, ya_ref, wo_ref, nw_ref, wq_ref, kv_ref, wco_ref, o_ref):
    contract_cols = (((1,), (1,)), ((), ()))
    scale = CROSS_HEAD_DIM ** -0.5

    for r in range(OC_TM // OC_RC):
        rows = slice(r * OC_RC, (r + 1) * OC_RC)
        x1 = (x_ref[rows, :]
              + jnp.dot(yr_ref[rows, :], wo_ref[0:RET_WIDTH, :], preferred_element_type=F32)
              + jnp.dot(ya_ref[rows, :], wo_ref[RET_WIDTH:, :], preferred_element_type=F32))
        h = (x1 * _rms_scale(x1) * nw_ref[...]).astype(BF16)
        q = (jnp.dot(h, wq_ref[...], preferred_element_type=F32) * scale).astype(BF16)
        heads = range(CROSS_HEADS)
        cols = [slice(hd * CROSS_HEAD_DIM, (hd + 1) * CROSS_HEAD_DIM) for hd in heads]
        s = [lax.dot_general(q[:, cols[hd]], kv_ref[:, cols[hd]], contract_cols,
                             preferred_element_type=F32) for hd in heads]
        e = [jnp.exp(s[hd] - jnp.max(s[hd], axis=-1, keepdims=True)) for hd in heads]
        inv = [1.0 / jnp.sum(e[hd], axis=-1, keepdims=True) for hd in heads]
        pv = [jnp.dot(e[hd].astype(BF16),
                      kv_ref[:, CROSS_WIDTH + hd * CROSS_HEAD_DIM:CROSS_WIDTH + (hd + 1) * CROSS_HEAD_DIM],
                      preferred_element_type=F32) for hd in heads]
        o = jnp.concatenate([(pv[hd] * inv[hd]).astype(BF16) for hd in heads], axis=1)
        o_ref[rows, :] = x1 + jnp.dot(o, wco_ref[...], preferred_element_type=F32)


def _out_cross(x2d, y_ret, y_attn, w_out, norm_w, w_q, mem_kv, w_co, seq):
    n = x2d.shape[0]
    assert n % OC_TM == 0 and seq % OC_TM == 0
    tiles_per_seq = seq // OC_TM
    vmem = (2 * 2 * OC_TM * D_MODEL * 4 + 2 * 2 * OC_TM * RET_WIDTH * 2 + 2 * D_MODEL * D_MODEL * 2
            + 2 * 2 * D_MODEL * CROSS_WIDTH * 2 + 2 * MEM_TOKENS * 2 * CROSS_WIDTH * 2)
    return pl.pallas_call(
        _out_cross_kernel,
        out_shape=jax.ShapeDtypeStruct((n, D_MODEL), F32),
        grid=(n // OC_TM,),
        in_specs=[
            pl.BlockSpec((OC_TM, D_MODEL), lambda i: (i, 0)),
            pl.BlockSpec((OC_TM, RET_WIDTH), lambda i: (i, 0)),
            pl.BlockSpec((OC_TM, ATTN_WIDTH), lambda i: (i, 0)),
            pl.BlockSpec((D_MODEL, D_MODEL), lambda i: (0, 0)),
            pl.BlockSpec((1, D_MODEL), lambda i: (0, 0)),
            pl.BlockSpec((D_MODEL, CROSS_WIDTH), lambda i: (0, 0)),
            pl.BlockSpec((MEM_TOKENS, 2 * CROSS_WIDTH), lambda i: (i // tiles_per_seq, 0)),
            pl.BlockSpec((CROSS_WIDTH, D_MODEL), lambda i: (0, 0)),
        ],
        out_specs=pl.BlockSpec((OC_TM, D_MODEL), lambda i: (i, 0)),
        compiler_params=pltpu.CompilerParams(
            dimension_semantics=("arbitrary",),
            vmem_limit_bytes=_vmem_limit(vmem)),
        name="out_cross",
    )(x2d, y_ret, y_attn, w_out, norm_w, w_q, mem_kv, w_co)


def _mlp_kernel(x_hbm, nw_ref, wu_ref, wd_ref, fw_ref, o_ref, h_ref, x_ref, x_sem):
    i = pl.program_id(0)
    j = pl.program_id(1)
    last = pl.num_programs(1) - 1

    def x_copy(tile):
        rows = pl.ds(pl.multiple_of(tile * MLP_TM, MLP_TM), MLP_TM)
        return pltpu.make_async_copy(x_hbm.at[rows, :], x_ref, x_sem)

    @pl.when((i == 0) & (j == 0))
    def _():
        x_copy(0).start()

    @pl.when(j == 0)
    def _():
        x_copy(i).wait()

    @pl.when((j == 1) & (i + 1 < pl.num_programs(0)))
    def _():
        x_copy(i + 1).start()

    def step(rows, is_first, is_last):
        if is_first:
            xr = x_ref[rows, :]
            h = (xr * _rms_scale(xr) * nw_ref[...]).astype(BF16)
            h_ref[rows, :] = h
            base = xr
        else:
            h = h_ref[rows, :]
            base = o_ref[rows, :]
        u = jnp.maximum(jnp.dot(h, wu_ref[...], preferred_element_type=F32), 0.0)
        y = base + jnp.dot((u * u).astype(BF16), wd_ref[...], preferred_element_type=F32)
        if is_last:
            y = y * _rms_scale(y) * fw_ref[...]
        o_ref[rows, :] = y

    edge_chunks = [slice(r * MLP_EDGE_RC, (r + 1) * MLP_EDGE_RC)
                   for r in range(MLP_TM // MLP_EDGE_RC)]

    @pl.when(j == 0)
    def _():
        for rows in edge_chunks:
            step(rows, True, False)

    @pl.when((j > 0) & (j < last))
    def _():
        step(slice(None), False, False)

    @pl.when(j == last)
    def _():
        for rows in edge_chunks:
            step(rows, False, True)


def _mlp(x2d, norm_w, w_up, w_down, final_w):
    n = x2d.shape[0]
    assert n % MLP_TM == 0 and D_FF % MLP_TF == 0 and D_FF // MLP_TF >= 3
    vmem = (3 * MLP_TM * D_MODEL * 4 + MLP_TM * D_MODEL * 2 + 2 * 2 * D_MODEL * MLP_TF * 2)
    return pl.pallas_call(
        _mlp_kernel,
        out_shape=jax.ShapeDtypeStruct((n, D_MODEL), F32),
        grid=(n // MLP_TM, D_FF // MLP_TF),
        in_specs=[
            pl.BlockSpec(memory_space=pl.ANY),
            pl.BlockSpec((1, D_MODEL), lambda i, j: (0, 0)),
            pl.BlockSpec((D_MODEL, MLP_TF), lambda i, j: (0, j)),
            pl.BlockSpec((MLP_TF, D_MODEL), lambda i, j: (j, 0)),
            pl.BlockSpec((1, D_MODEL), lambda i, j: (0, 0)),
        ],
        out_specs=pl.BlockSpec((MLP_TM, D_MODEL), lambda i, j: (i, 0)),
        scratch_shapes=[
            pltpu.VMEM((MLP_TM, D_MODEL), BF16),
            pltpu.VMEM((MLP_TM, D_MODEL), F32),
            pltpu.SemaphoreType.DMA(()),
        ],
        compiler_params=pltpu.CompilerParams(
            dimension_semantics=("arbitrary", "arbitrary"),
            vmem_limit_bytes=_vmem_limit(vmem)),
        name="mlp",
    )(x2d, norm_w, w_up, w_down, final_w)


def _rope_tables(seq):
    t = np.arange(seq)
    row = (t // GRID_W).astype(np.float64)
    col = (t % GRID_W).astype(np.float64)
    inv_freq = 1.0 / (ROPE_THETA ** (np.arange(0, AXIS_DIM, 2, dtype=np.float64) / AXIS_DIM))
    ang_r = row[:, None] * inv_freq[None, :]
    ang_c = col[:, None] * inv_freq[None, :]
    cos_t = np.concatenate([np.cos(ang_r), np.cos(ang_c), np.cos(ang_r), np.cos(ang_c)], axis=-1)
    sin_t = np.concatenate([-np.sin(ang_r), -np.sin(ang_c), np.sin(ang_r), np.sin(ang_c)], axis=-1)
    return jnp.asarray(cos_t, F32), jnp.asarray(sin_t, F32)


def _pair_heads(w):
    lead = w.shape[:-1]
    heads = w.shape[-1] // HEAD_DIM
    quarter = HEAD_DIM // 4
    w = w.reshape(lead + (heads, 2, 2, quarter))
    return jnp.swapaxes(w, -2, -3).reshape(lead + (heads * HEAD_DIM,))


def _w_in_prep_kernel(w_ref, o_ref):
    lane = lax.broadcasted_iota(jnp.int32, (WPREP_ROWS, HEAD_DIM), 1)
    quarter = HEAD_DIM // 4
    takes_c1 = (lane >= quarter) & (lane < 2 * quarter)
    takes_r2 = (lane >= 2 * quarter) & (lane < 3 * quarter)
    rope_slabs = (set(range(0, 2 * RET_HEADS))
                  | set(range(4 * RET_HEADS, 4 * RET_HEADS + ATTN_HEADS + ATTN_KV_HEADS)))
    for s in range(IN_WIDTH // HEAD_DIM):
        cols = slice(s * HEAD_DIM, (s + 1) * HEAD_DIM)
        y = w_ref[:, cols]
        if s in rope_slabs:
            y = jnp.where(takes_c1, pltpu.roll(y, HEAD_DIM - quarter, 1),
                          jnp.where(takes_r2, pltpu.roll(y, quarter, 1), y))
        o_ref[:, cols] = y.astype(BF16)


def _in_proj_weights(w_in):
    rows = w_in.shape[0]
    assert rows % WPREP_ROWS == 0
    vmem = 2 * WPREP_ROWS * IN_WIDTH * (4 + 2)
    return pl.pallas_call(
        _w_in_prep_kernel,
        out_shape=jax.ShapeDtypeStruct((rows, IN_WIDTH), BF16),
        grid=(rows // WPREP_ROWS,),
        in_specs=[pl.BlockSpec((WPREP_ROWS, IN_WIDTH), lambda i: (i, 0))],
        out_specs=pl.BlockSpec((WPREP_ROWS, IN_WIDTH), lambda i: (i, 0)),
        compiler_params=pltpu.CompilerParams(
            dimension_semantics=("arbitrary",),
            vmem_limit_bytes=_vmem_limit(vmem)),
        name="w_in_prep",
    )(w_in)


def kernel(x, mem, norm_mix_w, w_in, ret_decay_fwd, ret_decay_bwd, ret_gn_w, ret_gn_b, attn_q_norm_w, attn_k_norm_w, w_out, norm_cross_w, norm_mem_w, w_cross_q, w_cross_k, w_cross_v, w_cross_o, norm_mlp_w, w_mlp_up, w_mlp_down, norm_final_w):
    batch, seq, _ = x.shape
    assert w_in.shape[0] == 1, "single-layer block: per-layer parameters have a leading axis of 1"
    cos_t, sin_t = _rope_tables(seq)
    xs = x.reshape(batch * seq, D_MODEL)
    mem2d = mem.reshape(batch * MEM_TOKENS, D_MODEL)
    dec = jnp.stack([ret_decay_fwd[0], ret_decay_bwd[0]], axis=1)
    dec = jnp.broadcast_to(dec[:, :, None], (RET_HEADS, 2, RET_C)).astype(F32)
    proj = _in_proj(xs, norm_mix_w, _in_proj_weights(w_in[0]), cos_t, sin_t,
                    _pair_heads(attn_q_norm_w), _pair_heads(attn_k_norm_w), seq)
    y_ret, (w_out_bf16, w_cq_bf16, w_co_bf16) = _retention(
        proj, dec, ret_gn_w, ret_gn_b, (w_out[0], w_cross_q[0], w_cross_o[0]), batch, seq)
    score_bound = (LOG2E * math.sqrt(HEAD_DIM) * ATT_BOUND_MARGIN
                   * jnp.max(jnp.abs(attn_q_norm_w)) * jnp.max(jnp.abs(attn_k_norm_w)))
    score_bound = jnp.full((1, HEAD_DIM), score_bound, F32)
    y_attn, w_up_bf16, w_down_bf16 = _attention(proj, score_bound, w_mlp_up[0], w_mlp_down[0],
                                                batch, seq)
    mem_kv = _mem_kv(mem2d, norm_mem_w, w_cross_k[0], w_cross_v[0])
    xs = _out_cross(xs, y_ret, y_attn, w_out_bf16, norm_cross_w, w_cq_bf16, mem_kv, w_co_bf16, seq)
    xs = _mlp(xs, norm_mlp_w, w_up_bf16, w_down_bf16, norm_final_w[None, :])
    return xs.reshape(batch, seq, D_MODEL)
```

```python
import functools
import math

import numpy as np
import jax
import jax.numpy as jnp
from jax import lax
from jax.experimental import pallas as pl
from jax.experimental.pallas import tpu as pltpu

D_MODEL = 2048
HEAD_DIM = 128
RET_WIDTH = 1024
ATTN_WIDTH = 1024
RET_HEADS = 8
ATTN_HEADS = 8
ATTN_KV_HEADS = 2
GQA_GROUP = ATTN_HEADS // ATTN_KV_HEADS
KV_WIDTH = ATTN_KV_HEADS * HEAD_DIM
IN_WIDTH = 4 * RET_WIDTH + ATTN_WIDTH + 2 * KV_WIDTH
GRID_W = 64
AXIS_DIM = HEAD_DIM // 2
ROPE_THETA = 10000.0
MEM_TOKENS = 256
CROSS_HEADS = 4
CROSS_HEAD_DIM = 128
CROSS_WIDTH = CROSS_HEADS * CROSS_HEAD_DIM
D_FF = 4 * D_MODEL
NORM_EPS = 1e-6

V7X_BF16_SUBLANES = 16
V7X_VMEM_BYTES = 64 * 1024 * 1024

F32 = jnp.float32
BF16 = jnp.bfloat16
LOG2E = math.log2(math.e)
NEG_BIG = -1e30

IN_TM = 512
IN_TN = 512
IN_NORM_RC = 256
WPREP_ROWS = 256
RET_C = 256
RET_TS = 2048
RET_HG = 4
ATT_TQ = 512
ATT_TK = 4096
ATT_DENOM_FLOOR = 2.0 ** -40
ATT_BOUND_MARGIN = 1.0 + 2.0 ** -6
OC_TM = 512
OC_RC = 512
MLP_TM = 1024
MLP_TF = 1024
MLP_EDGE_RC = 512


VMEM_TEMPORARIES = 16 * 1024 * 1024
VMEM_UNCLAIMED = 8 * 1024 * 1024


def _vmem_limit(nbytes):
    return int(min(nbytes + VMEM_TEMPORARIES, V7X_VMEM_BYTES - VMEM_UNCLAIMED))


def _rms_scale(y):
    return lax.rsqrt(jnp.mean(y * y, axis=-1, keepdims=True) + NORM_EPS)


def _slab_cast_specs(weights, steps, flat_step):
    specs, shapes, nbytes = [], [], 0
    for w in weights:
        assert w.shape[0] % steps == 0 and (w.shape[0] // steps) % V7X_BF16_SUBLANES == 0
        block = (w.shape[0] // steps, w.shape[1])
        specs.append(pl.BlockSpec(block, lambda *ids: (flat_step(*ids), 0)))
        shapes.append(jax.ShapeDtypeStruct(w.shape, BF16))
        nbytes += 2 * block[0] * block[1] * (4 + 2)
    return specs, shapes, nbytes


def _cast_slabs(src_refs, dst_refs):
    for src, dst in zip(src_refs, dst_refs):
        dst[...] = src[...].astype(BF16)


def _rope(y, cos, sin_signed):
    return y * cos + pltpu.roll(y, HEAD_DIM // 2, 1) * sin_signed


def _in_proj_kernel(x_ref, nw_ref, w_ref, cos_ref, sin_ref, qg_ref, kg_ref, o_ref, h_ref):
    for r in range(IN_TM // IN_NORM_RC):
        rows = slice(r * IN_NORM_RC, (r + 1) * IN_NORM_RC)
        xr = x_ref[rows, :]
        h_ref[rows, :] = (xr * _rms_scale(xr) * nw_ref[...]).astype(BF16)

    ret_k_scale = HEAD_DIM ** -0.5
    attn_q_scale = (HEAD_DIM ** -0.5) * LOG2E

    n_slabs = IN_TN // HEAD_DIM

    def run(j, epilogue):
        tile = slice(j * IN_TN, (j + 1) * IN_TN)
        acc = jnp.dot(h_ref[...], w_ref[:, tile], preferred_element_type=F32)
        slabs = epilogue([acc[:, s * HEAD_DIM:(s + 1) * HEAD_DIM] for s in range(n_slabs)],
                         cos_ref[...], sin_ref[...])
        for s, y in enumerate(slabs):
            c0 = j * IN_TN + s * HEAD_DIM
            o_ref[:, c0:c0 + HEAD_DIM] = y.astype(BF16)

    def ret_q(ys, cos, sin):
        return [_rope(y, cos, sin) for y in ys]

    def ret_k(ys, cos, sin):
        return [_rope(y, cos, sin) * ret_k_scale for y in ys]

    def plain(ys, cos, sin):
        return ys

    def head_inv_rms(y_a, y_b):
        width = 2 * HEAD_DIM
        r = lax.broadcasted_iota(jnp.int32, (width, width), 0) // HEAD_DIM
        c = lax.broadcasted_iota(jnp.int32, (width, width), 1) // HEAD_DIM
        ones_bd = jnp.where(r == c, 1.0, 0.0).astype(BF16)
        sq = jnp.concatenate([(y_a * y_a).astype(BF16), (y_b * y_b).astype(BF16)], axis=1)
        ms = jnp.dot(sq, ones_bd, preferred_element_type=F32) * (1.0 / HEAD_DIM)
        inv = lax.rsqrt(ms + NORM_EPS)
        return inv[:, :HEAD_DIM], inv[:, HEAD_DIM:]

    def normed_rope(y_a, y_b, gain, scale, cos, sin):
        inv_a, inv_b = head_inv_rms(y_a, y_b)
        return [_rope(y_a * gain, cos, sin) * (inv_a * scale),
                _rope(y_b * gain, cos, sin) * (inv_b * scale)]

    def attn_q(ys, cos, sin):
        gain = qg_ref[...]
        return (normed_rope(ys[0], ys[1], gain, attn_q_scale, cos, sin)
                + normed_rope(ys[2], ys[3], gain, attn_q_scale, cos, sin))

    def attn_kv(ys, cos, sin):
        return normed_rope(ys[0], ys[1], kg_ref[...], 1.0, cos, sin) + ys[ATTN_KV_HEADS:]

    nb = RET_WIDTH // IN_TN
    for j in range(IN_WIDTH // IN_TN):
        if j < nb:
            run(j, ret_q)
        elif j < 2 * nb:
            run(j, ret_k)
        elif j < 4 * nb:
            run(j, plain)
        elif j < 5 * nb:
            run(j, attn_q)
        else:
            run(j, attn_kv)


def _in_proj(x2d, norm_w, w_in, cos_t, sin_t, q_gain, k_gain, seq):
    n = x2d.shape[0]
    assert IN_WIDTH % IN_TN == 0 and 2 * KV_WIDTH == IN_TN and n % IN_TM == 0 and seq % IN_TM == 0
    tiles_per_seq = seq // IN_TM
    vmem = (2 * IN_TM * D_MODEL * 4 + IN_TM * D_MODEL * 2 + D_MODEL * IN_WIDTH * 2
            + 2 * IN_TM * IN_WIDTH * 2 + 4 * IN_TM * HEAD_DIM * 4)
    resident = pl.Buffered(1)
    return pl.pallas_call(
        _in_proj_kernel,
        out_shape=jax.ShapeDtypeStruct((n, IN_WIDTH), BF16),
        grid=(n // IN_TM,),
        in_specs=[
            pl.BlockSpec((IN_TM, D_MODEL), lambda i: (i, 0)),
            pl.BlockSpec((1, D_MODEL), lambda i: (0, 0)),
            pl.BlockSpec((D_MODEL, IN_WIDTH), lambda i: (0, 0), pipeline_mode=resident),
            pl.BlockSpec((IN_TM, HEAD_DIM), lambda i: (i % tiles_per_seq, 0)),
            pl.BlockSpec((IN_TM, HEAD_DIM), lambda i: (i % tiles_per_seq, 0)),
            pl.BlockSpec((1, HEAD_DIM), lambda i: (0, 0)),
            pl.BlockSpec((1, HEAD_DIM), lambda i: (0, 0)),
        ],
        out_specs=pl.BlockSpec((IN_TM, IN_WIDTH), lambda i: (i, 0)),
        scratch_shapes=[pltpu.VMEM((IN_TM, D_MODEL), BF16)],
        compiler_params=pltpu.CompilerParams(
            dimension_semantics=("arbitrary",),
            vmem_limit_bytes=_vmem_limit(vmem)),
        name="in_proj",
    )(x2d, norm_w, w_in, cos_t, sin_t, q_gain, k_gain)


def _retention_kernel(dec_ref, q_ref, k_ref, v_ref, g_ref, gw_ref, gb_ref,
                      wout_ref, wq_ref, wco_ref, o_ref, wout_o_ref, wq_o_ref, wco_o_ref,
                      d_ref, tab_ref, sb_ref, st_ref, *, n_seq):
    _cast_slabs((wout_ref, wq_ref, wco_ref), (wout_o_ref, wq_o_ref, wco_o_ref))
    s = pl.program_id(2)
    c_len = RET_C
    cps = RET_TS // RET_C
    contract_rows = (((0,), (0,)), ((), ()))
    contract_cols = (((1,), (1,)), ((), ()))
    head_cols = [slice(h * HEAD_DIM, (h + 1) * HEAD_DIM) for h in range(RET_HG)]
    chunk_rows = [slice(c * c_len, (c + 1) * c_len) for c in range(cps)]

    def log_gammas(h):
        return -jnp.exp(dec_ref[h, 0:1, :]), -jnp.exp(dec_ref[h, 1:2, :])

    @pl.when(s == 0)
    def _():
        row = lax.broadcasted_iota(jnp.int32, (c_len, c_len), 0)
        col = lax.broadcasted_iota(jnp.int32, (c_len, c_len), 1)
        rel = (row - col).astype(F32)
        idx = lax.broadcasted_iota(jnp.int32, (c_len, HEAD_DIM), 0).astype(F32)
        for h in range(RET_HG):
            lg_f, lg_b = log_gammas(h)
            d_ref[h] = jnp.exp(jnp.where(rel >= 0, lg_f * rel, -lg_b * rel))
            lf = lg_f[:, :HEAD_DIM]
            lb = lg_b[:, :HEAD_DIM]
            tab_ref[h, 0] = jnp.exp(lf * (idx + 1.0)).astype(BF16)
            tab_ref[h, 1] = jnp.exp(lb * (c_len - idx)).astype(BF16)
            tab_ref[h, 2] = jnp.exp(lf * (c_len - 1.0 - idx)).astype(BF16)
            tab_ref[h, 3] = jnp.exp(lb * idx).astype(BF16)
        st_ref[...] = jnp.zeros(st_ref.shape, F32)

    @pl.when(s < n_seq)
    def _():
        first_chunk = (n_seq - 1 - s) * cps
        for h in range(RET_HG):
            chunk_b = jnp.exp(log_gammas(h)[1][:, :HEAD_DIM] * c_len)
            state = st_ref[h]
            for c in reversed(range(cps)):
                sb_ref[first_chunk + c, h] = state.astype(BF16)
                k = k_ref[chunk_rows[c], head_cols[h]]
                kd = k * tab_ref[h, 3]
                upd = lax.dot_general(kd, v_ref[chunk_rows[c], head_cols[h]], contract_rows,
                                      preferred_element_type=F32)
                state = state * chunk_b + upd
            st_ref[h] = state

        @pl.when(s == n_seq - 1)
        def _():
            st_ref[...] = jnp.zeros(st_ref.shape, F32)

    @pl.when(s >= n_seq)
    def _():
        first_chunk = (s - n_seq) * cps
        heads = range(RET_HG)
        chunk_f = [jnp.exp(log_gammas(h)[0][:, :HEAD_DIM] * c_len) for h in heads]
        state = [st_ref[h] for h in heads]
        for c in range(cps):
            blk = [(chunk_rows[c], head_cols[h]) for h in heads]
            scores = [lax.dot_general(q_ref[blk[h]], k_ref[blk[h]], contract_cols,
                                      preferred_element_type=F32) for h in heads]
            out = []
            for h in heads:
                q = q_ref[blk[h]]
                masked = (scores[h] * d_ref[h]).astype(BF16)
                lhs = jnp.concatenate([masked, q * tab_ref[h, 0], q * tab_ref[h, 1]], axis=1)
                rhs = jnp.concatenate(
                    [v_ref[blk[h]], state[h].astype(BF16), sb_ref[first_chunk + c, h]], axis=0)
                out.append(jnp.dot(lhs, rhs, preferred_element_type=F32))
            upd = [lax.dot_general(k_ref[blk[h]] * tab_ref[h, 2], v_ref[blk[h]], contract_rows,
                                   preferred_element_type=F32) for h in heads]
            state = [state[h] * chunk_f[h] + upd[h] for h in heads]
            cen = [out[h] - jnp.mean(out[h], axis=-1, keepdims=True) for h in heads]
            var = [jnp.mean(cen[h] * cen[h], axis=-1, keepdims=True) for h in heads]
            for h in heads:
                y = (cen[h] * lax.rsqrt(var[h] + NORM_EPS) * gw_ref[:, head_cols[h]]
                     + gb_ref[:, head_cols[h]])
                g = g_ref[blk[h]].astype(F32)
                y = y * (g / (1.0 + jnp.exp(-g)))
                o_ref[blk[h]] = y.astype(BF16)
        for h in heads:
            st_ref[h] = state[h]


def _retention(proj, dec, gn_w, gn_b, cross_weights, batch, seq):
    n = proj.shape[0]
    assert seq % RET_TS == 0 and RET_TS % RET_C == 0 and RET_HEADS % RET_HG == 0
    n_seq = seq // RET_TS
    n_chunks = seq // RET_C
    group_w = RET_HG * HEAD_DIM
    groups = RET_HEADS // RET_HG

    def kv_block(section):
        def index(b, hg, s):
            blk = jnp.where(s < n_seq, n_seq - 1 - s, s - n_seq)
            return (b * n_seq + blk, section * groups + hg)
        return index

    def fwd_block(section):
        return lambda b, hg, s: (b * n_seq + jnp.maximum(s - n_seq, 0), section * groups + hg)

    cast_specs, cast_shapes, cast_bytes = _slab_cast_specs(
        cross_weights, batch * groups * 2 * n_seq,
        lambda b, hg, s: (b * groups + hg) * 2 * n_seq + s)
    vmem = (2 * 5 * RET_TS * group_w * 2 + RET_HG * RET_C * RET_C * 4
            + RET_HG * 4 * RET_C * HEAD_DIM * 4 + n_chunks * RET_HG * HEAD_DIM * HEAD_DIM * 2
            + RET_HG * HEAD_DIM * HEAD_DIM * 4 + cast_bytes)
    y_ret, *cast = pl.pallas_call(
        functools.partial(_retention_kernel, n_seq=n_seq),
        out_shape=(jax.ShapeDtypeStruct((n, RET_WIDTH), BF16), *cast_shapes),
        grid=(batch, groups, 2 * n_seq),
        in_specs=[
            pl.BlockSpec((RET_HG, 2, RET_C), lambda b, hg, s: (hg, 0, 0)),
            pl.BlockSpec((RET_TS, group_w), fwd_block(0)),
            pl.BlockSpec((RET_TS, group_w), kv_block(1)),
            pl.BlockSpec((RET_TS, group_w), kv_block(2)),
            pl.BlockSpec((RET_TS, group_w), fwd_block(3)),
            pl.BlockSpec((1, group_w), lambda b, hg, s: (0, hg)),
            pl.BlockSpec((1, group_w), lambda b, hg, s: (0, hg)),
            *cast_specs,
        ],
        out_specs=(pl.BlockSpec((RET_TS, group_w), fwd_block(0)), *cast_specs),
        scratch_shapes=[
            pltpu.VMEM((RET_HG, RET_C, RET_C), F32),
            pltpu.VMEM((RET_HG, 4, RET_C, HEAD_DIM), BF16),
            pltpu.VMEM((n_chunks, RET_HG, HEAD_DIM, HEAD_DIM), BF16),
            pltpu.VMEM((RET_HG, HEAD_DIM, HEAD_DIM), F32),
        ],
        compiler_params=pltpu.CompilerParams(
            dimension_semantics=("arbitrary", "arbitrary", "arbitrary"),
            vmem_limit_bytes=_vmem_limit(vmem)),
        name="retention",
    )(dec, proj, proj, proj, proj, gn_w, gn_b, *cross_weights)
    return y_ret, cast


def _attention_kernel(q_ref, k_ref, v_ref, bound_ref, wu_ref, wd_ref, o_ref, wu_o_ref, wd_o_ref,
                      vt_ref, qa_ref, acc_ref, den_ref, *, n_kv):
    _cast_slabs((wu_ref, wd_ref), (wu_o_ref, wd_o_ref))

    i = pl.program_id(2)
    contract_cols = (((1,), (1,)), ((), ()))

    def kv_rows(c):
        return pl.ds(pl.multiple_of(c * ATT_TK, ATT_TK), ATT_TK)

    @pl.when(i == 0)
    def _():
        def body(c, carry):
            vt_ref[c] = v_ref[kv_rows(c), :].astype(F32).T.astype(BF16)
            return carry
        lax.fori_loop(0, n_kv, body, 0)

    q_lane = lax.broadcasted_iota(jnp.int32, (ATT_TQ, HEAD_DIM), 1)
    k_lane = lax.broadcasted_iota(jnp.int32, (ATT_TK, HEAD_DIM), 1)
    k_one = jnp.where(k_lane == 0, 1.0, 0.0).astype(BF16)
    head_cols = [slice(h * HEAD_DIM, (h + 1) * HEAD_DIM) for h in range(GQA_GROUP)]

    def shift_column(shift):
        return jnp.where(q_lane == 0, -shift, 0.0).astype(BF16)

    def sweep(q_aug):
        acc_ref[...] = jnp.zeros(acc_ref.shape, F32)
        den_ref[...] = jnp.zeros(den_ref.shape, F32)

        def kv_step(c, carry):
            k_aug = jnp.concatenate([k_ref[kv_rows(c), :], k_one], axis=1)
            vt = vt_ref[c]
            def scores(h):
                return lax.dot_general(k_aug, q_aug(h), contract_cols,
                                       preferred_element_type=F32)

            s_next = scores(0)
            for h in range(GQA_GROUP):
                s_t = s_next
                if h + 1 < GQA_GROUP:
                    s_next = scores(h + 1)
                p_t = jnp.exp2(s_t)
                den_ref[h] += jnp.sum(p_t.reshape(ATT_TK // 8, 8, ATT_TQ), axis=0)
                acc_ref[h] += jnp.dot(vt, p_t.astype(BF16), preferred_element_type=F32)
            return carry
        lax.fori_loop(0, n_kv, kv_step, 0)

    bound_column = shift_column(bound_ref[...])
    sweep(lambda h: jnp.concatenate([q_ref[:, head_cols[h]], bound_column], axis=1))

    def denominator(h):
        return jnp.sum(den_ref[h], axis=0, keepdims=True)

    denom_min = functools.reduce(jnp.minimum,
                                 [jnp.min(denominator(h)) for h in range(GQA_GROUP)])

    @pl.when(denom_min < ATT_DENOM_FLOOR)
    def _():
        for h in range(GQA_GROUP):
            def body(c, m):
                s = lax.dot_general(q_ref[:, head_cols[h]], k_ref[kv_rows(c), :], contract_cols,
                                    preferred_element_type=F32)
                return jnp.maximum(m, jnp.max(s, axis=1, keepdims=True))
            row_max = lax.fori_loop(0, n_kv, body, jnp.full((ATT_TQ, 1), NEG_BIG, F32))
            qa_ref[h, :, 0:HEAD_DIM] = q_ref[:, head_cols[h]]
            qa_ref[h, :, HEAD_DIM:] = shift_column(row_max)
        sweep(lambda h: qa_ref[h])

    inv_den = [1.0 / denominator(h) for h in range(GQA_GROUP)]
    o_t = [acc_ref[h] * inv_den[h] for h in range(GQA_GROUP)]
    for h in range(GQA_GROUP):
        o_ref[:, head_cols[h]] = o_t[h].T.astype(BF16)


def _attention(proj, score_bound, w_up, w_down, batch, seq):
    n = proj.shape[0]
    assert seq % ATT_TQ == 0 and seq % ATT_TK == 0
    n_q = seq // ATT_TQ
    n_kv = seq // ATT_TK
    group_w = GQA_GROUP * HEAD_DIM
    q_col0 = 4 * RET_WIDTH // group_w
    k_col0 = (4 * RET_WIDTH + ATTN_WIDTH) // HEAD_DIM
    v_col0 = k_col0 + ATTN_KV_HEADS
    cast_specs, cast_shapes, cast_bytes = _slab_cast_specs(
        (w_up, w_down), batch * ATTN_KV_HEADS * n_q,
        lambda b, g, i: (b * ATTN_KV_HEADS + g) * n_q + i)
    vmem = (2 * ATT_TQ * group_w * 2 * 2 + 2 * 2 * seq * HEAD_DIM * 2
            + HEAD_DIM * seq * 2 + GQA_GROUP * (HEAD_DIM + 8) * ATT_TQ * 4
            + GQA_GROUP * ATT_TQ * 2 * HEAD_DIM * 2 + 2 * GQA_GROUP * ATT_TK * ATT_TQ * 4
            + cast_bytes)
    return pl.pallas_call(
        functools.partial(_attention_kernel, n_kv=n_kv),
        out_shape=(jax.ShapeDtypeStruct((n, ATTN_WIDTH), BF16), *cast_shapes),
        grid=(batch, ATTN_KV_HEADS, n_q),
        in_specs=[
            pl.BlockSpec((ATT_TQ, group_w), lambda b, g, i: (b * n_q + i, q_col0 + g)),
            pl.BlockSpec((seq, HEAD_DIM), lambda b, g, i: (b, k_col0 + g)),
            pl.BlockSpec((seq, HEAD_DIM), lambda b, g, i: (b, v_col0 + g)),
            pl.BlockSpec((1, HEAD_DIM), lambda b, g, i: (0, 0)),
            *cast_specs,
        ],
        out_specs=(pl.BlockSpec((ATT_TQ, group_w), lambda b, g, i: (b * n_q + i, g)),
                   *cast_specs),
        scratch_shapes=[
            pltpu.VMEM((n_kv, HEAD_DIM, ATT_TK), BF16),
            pltpu.VMEM((GQA_GROUP, ATT_TQ, 2 * HEAD_DIM), BF16),
            pltpu.VMEM((GQA_GROUP, HEAD_DIM, ATT_TQ), F32),
            pltpu.VMEM((GQA_GROUP, 8, ATT_TQ), F32),
        ],
        compiler_params=pltpu.CompilerParams(
            dimension_semantics=("arbitrary", "arbitrary", "arbitrary"),
            vmem_limit_bytes=_vmem_limit(vmem)),
        name="attention",
    )(proj, proj, proj, score_bound, w_up, w_down)


def _mem_kv_kernel(m_ref, nw_ref, wk_ref, wv_ref, o_ref):
    m = m_ref[...]
    h = (m * _rms_scale(m) * nw_ref[...]).astype(BF16)
    o_ref[:, :CROSS_WIDTH] = jnp.dot(h, wk_ref[...].astype(BF16),
                                     preferred_element_type=F32).astype(BF16)
    o_ref[:, CROSS_WIDTH:] = jnp.dot(h, wv_ref[...].astype(BF16),
                                     preferred_element_type=F32).astype(BF16)


def _mem_kv(mem2d, norm_w, w_k, w_v):
    rows = mem2d.shape[0]
    vmem = 2 * (rows * D_MODEL * 4 + 2 * D_MODEL * CROSS_WIDTH * 4 + rows * 2 * CROSS_WIDTH * 2)
    return pl.pallas_call(
        _mem_kv_kernel,
        out_shape=jax.ShapeDtypeStruct((rows, 2 * CROSS_WIDTH), BF16),
        grid=(1,),
        in_specs=[
            pl.BlockSpec((rows, D_MODEL), lambda i: (0, 0)),
            pl.BlockSpec((1, D_MODEL), lambda i: (0, 0)),
            pl.BlockSpec((D_MODEL, CROSS_WIDTH), lambda i: (0, 0)),
            pl.BlockSpec((D_MODEL, CROSS_WIDTH), lambda i: (0, 0)),
        ],
        out_specs=pl.BlockSpec((rows, 2 * CROSS_WIDTH), lambda i: (0, 0)),
        compiler_params=pltpu.CompilerParams(
            dimension_semantics=("arbitrary",),
            vmem_limit_bytes=_vmem_limit(vmem)),
        name="mem_kv",
    )(mem2d, norm_w, w_k, w_v)


def _out_cross_kernel(x_ref, yr_ref, ya_ref, wo_ref, nw_ref, wq_ref, kv_ref, wco_ref, o_ref):
    contract_cols = (((1,), (1,)), ((), ()))
    scale = CROSS_HEAD_DIM ** -0.5

    for r in range(OC_TM // OC_RC):
        rows = slice(r * OC_RC, (r + 1) * OC_RC)
        x1 = (x_ref[rows, :]
              + jnp.dot(yr_ref[rows, :], wo_ref[0:RET_WIDTH, :], preferred_element_type=F32)
              + jnp.dot(ya_ref[rows, :], wo_ref[RET_WIDTH:, :], preferred_element_type=F32))
        h = (x1 * _rms_scale(x1) * nw_ref[...]).astype(BF16)
        q = (jnp.dot(h, wq_ref[...], preferred_element_type=F32) * scale).astype(BF16)
        heads = range(CROSS_HEADS)
        cols = [slice(hd * CROSS_HEAD_DIM, (hd + 1) * CROSS_HEAD_DIM) for hd in heads]
        s = [lax.dot_general(q[:, cols[hd]], kv_ref[:, cols[hd]], contract_cols,
                             preferred_element_type=F32) for hd in heads]
        e = [jnp.exp(s[hd] - jnp.max(s[hd], axis=-1, keepdims=True)) for hd in heads]
        inv = [1.0 / jnp.sum(e[hd], axis=-1, keepdims=True) for hd in heads]
        pv = [jnp.dot(e[hd].astype(BF16),
                      kv_ref[:, CROSS_WIDTH + hd * CROSS_HEAD_DIM:CROSS_WIDTH + (hd + 1) * CROSS_HEAD_DIM],
                      preferred_element_type=F32) for hd in heads]
        o = jnp.concatenate([(pv[hd] * inv[hd]).astype(BF16) for hd in heads], axis=1)
        o_ref[rows, :] = x1 + jnp.dot(o, wco_ref[...], preferred_element_type=F32)


def _out_cross(x2d, y_ret, y_attn, w_out, norm_w, w_q, mem_kv, w_co, seq):
    n = x2d.shape[0]
    assert n % OC_TM == 0 and seq % OC_TM == 0
    tiles_per_seq = seq // OC_TM
    vmem = (2 * 2 * OC_TM * D_MODEL * 4 + 2 * 2 * OC_TM * RET_WIDTH * 2 + 2 * D_MODEL * D_MODEL * 2
            + 2 * 2 * D_MODEL * CROSS_WIDTH * 2 + 2 * MEM_TOKENS * 2 * CROSS_WIDTH * 2)
    return pl.pallas_call(
        _out_cross_kernel,
        out_shape=jax.ShapeDtypeStruct((n, D_MODEL), F32),
        grid=(n // OC_TM,),
        in_specs=[
            pl.BlockSpec((OC_TM, D_MODEL), lambda i: (i, 0)),
            pl.BlockSpec((OC_TM, RET_WIDTH), lambda i: (i, 0)),
            pl.BlockSpec((OC_TM, ATTN_WIDTH), lambda i: (i, 0)),
            pl.BlockSpec((D_MODEL, D_MODEL), lambda i: (0, 0)),
            pl.BlockSpec((1, D_MODEL), lambda i: (0, 0)),
            pl.BlockSpec((D_MODEL, CROSS_WIDTH), lambda i: (0, 0)),
            pl.BlockSpec((MEM_TOKENS, 2 * CROSS_WIDTH), lambda i: (i // tiles_per_seq, 0)),
            pl.BlockSpec((CROSS_WIDTH, D_MODEL), lambda i: (0, 0)),
        ],
        out_specs=pl.BlockSpec((OC_TM, D_MODEL), lambda i: (i, 0)),
        compiler_params=pltpu.CompilerParams(
            dimension_semantics=("arbitrary",),
            vmem_limit_bytes=_vmem_limit(vmem)),
        name="out_cross",
    )(x2d, y_ret, y_attn, w_out, norm_w, w_q, mem_kv, w_co)


def _mlp_kernel(x_hbm, nw_ref, wu_ref, wd_ref, fw_ref, o_ref, h_ref, x_ref, x_sem):
    i = pl.program_id(0)
    j = pl.program_id(1)
    last = pl.num_programs(1) - 1

    def x_copy(tile):
        rows = pl.ds(pl.multiple_of(tile * MLP_TM, MLP_TM), MLP_TM)
        return pltpu.make_async_copy(x_hbm.at[rows, :], x_ref, x_sem)

    @pl.when((i == 0) & (j == 0))
    def _():
        x_copy(0).start()

    @pl.when(j == 0)
    def _():
        x_copy(i).wait()

    @pl.when((j == 1) & (i + 1 < pl.num_programs(0)))
    def _():
        x_copy(i + 1).start()

    def step(rows, is_first, is_last):
        if is_first:
            xr = x_ref[rows, :]
            h = (xr * _rms_scale(xr) * nw_ref[...]).astype(BF16)
            h_ref[rows, :] = h
            base = xr
        else:
            h = h_ref[rows, :]
            base = o_ref[rows, :]
        u = jnp.maximum(jnp.dot(h, wu_ref[...], preferred_element_type=F32), 0.0)
        y = base + jnp.dot((u * u).astype(BF16), wd_ref[...], preferred_element_type=F32)
        if is_last:
            y = y * _rms_scale(y) * fw_ref[...]
        o_ref[rows, :] = y

    edge_chunks = [slice(r * MLP_EDGE_RC, (r + 1) * MLP_EDGE_RC)
                   for r in range(MLP_TM // MLP_EDGE_RC)]

    @pl.when(j == 0)
    def _():
        for rows in edge_chunks:
            step(rows, True, False)

    @pl.when((j > 0) & (j < last))
    def _():
        step(slice(None), False, False)

    @pl.when(j == last)
    def _():
        for rows in edge_chunks:
            step(rows, False, True)


def _mlp(x2d, norm_w, w_up, w_down, final_w):
    n = x2d.shape[0]
    assert n % MLP_TM == 0 and D_FF % MLP_TF == 0 and D_FF // MLP_TF >= 3
    vmem = (3 * MLP_TM * D_MODEL * 4 + MLP_TM * D_MODEL * 2 + 2 * 2 * D_MODEL * MLP_TF * 2)
    return pl.pallas_call(
        _mlp_kernel,
        out_shape=jax.ShapeDtypeStruct((n, D_MODEL), F32),
        grid=(n // MLP_TM, D_FF // MLP_TF),
        in_specs=[
            pl.BlockSpec(memory_space=pl.ANY),
            pl.BlockSpec((1, D_MODEL), lambda i, j: (0, 0)),
            pl.BlockSpec((D_MODEL, MLP_TF), lambda i, j: (0, j)),
            pl.BlockSpec((MLP_TF, D_MODEL), lambda i, j: (j, 0)),
            pl.BlockSpec((1, D_MODEL), lambda i, j: (0, 0)),
        ],
        out_specs=pl.BlockSpec((MLP_TM, D_MODEL), lambda i, j: (i, 0)),
        scratch_shapes=[
            pltpu.VMEM((MLP_TM, D_MODEL), BF16),
            pltpu.VMEM((MLP_TM, D_MODEL), F32),
            pltpu.SemaphoreType.DMA(()),
        ],
        compiler_params=pltpu.CompilerParams(
            dimension_semantics=("arbitrary", "arbitrary"),
            vmem_limit_bytes=_vmem_limit(vmem)),
        name="mlp",
    )(x2d, norm_w, w_up, w_down, final_w)


def _rope_tables(seq):
    t = np.arange(seq)
    row = (t // GRID_W).astype(np.float64)
    col = (t % GRID_W).astype(np.float64)
    inv_freq = 1.0 / (ROPE_THETA ** (np.arange(0, AXIS_DIM, 2, dtype=np.float64) / AXIS_DIM))
    ang_r = row[:, None] * inv_freq[None, :]
    ang_c = col[:, None] * inv_freq[None, :]
    cos_t = np.concatenate([np.cos(ang_r), np.cos(ang_c), np.cos(ang_r), np.cos(ang_c)], axis=-1)
    sin_t = np.concatenate([-np.sin(ang_r), -np.sin(ang_c), np.sin(ang_r), np.sin(ang_c)], axis=-1)
    return jnp.asarray(cos_t, F32), jnp.asarray(sin_t, F32)


def _pair_heads(w):
    lead = w.shape[:-1]
    heads = w.shape[-1] // HEAD_DIM
    quarter = HEAD_DIM // 4
    w = w.reshape(lead + (heads, 2, 2, quarter))
    return jnp.swapaxes(w, -2, -3).reshape(lead + (heads * HEAD_DIM,))


def _w_in_prep_kernel(w_ref, o_ref):
    lane = lax.broadcasted_iota(jnp.int32, (WPREP_ROWS, HEAD_DIM), 1)
    quarter = HEAD_DIM // 4
    takes_c1 = (lane >= quarter) & (lane < 2 * quarter)
    takes_r2 = (lane >= 2 * quarter) & (lane < 3 * quarter)
    rope_slabs = (set(range(0, 2 * RET_HEADS))
                  | set(range(4 * RET_HEADS, 4 * RET_HEADS + ATTN_HEADS + ATTN_KV_HEADS)))
    for s in range(IN_WIDTH // HEAD_DIM):
        cols = slice(s * HEAD_DIM, (s + 1) * HEAD_DIM)
        y = w_ref[:, cols]
        if s in rope_slabs:
            y = jnp.where(takes_c1, pltpu.roll(y, HEAD_DIM - quarter, 1),
                          jnp.where(takes_r2, pltpu.roll(y, quarter, 1), y))
        o_ref[:, cols] = y.astype(BF16)


def _in_proj_weights(w_in):
    rows = w_in.shape[0]
    assert rows % WPREP_ROWS == 0
    vmem = 2 * WPREP_ROWS * IN_WIDTH * (4 + 2)
    return pl.pallas_call(
        _w_in_prep_kernel,
        out_shape=jax.ShapeDtypeStruct((rows, IN_WIDTH), BF16),
        grid=(rows // WPREP_ROWS,),
        in_specs=[pl.BlockSpec((WPREP_ROWS, IN_WIDTH), lambda i: (i, 0))],
        out_specs=pl.BlockSpec((WPREP_ROWS, IN_WIDTH), lambda i: (i, 0)),
        compiler_params=pltpu.CompilerParams(
            dimension_semantics=("arbitrary",),
            vmem_limit_bytes=_vmem_limit(vmem)),
        name="w_in_prep",
    )(w_in)


def kernel(x, mem, norm_mix_w, w_in, ret_decay_fwd, ret_decay_bwd, ret_gn_w, ret_gn_b, attn_q_norm_w, attn_k_norm_w, w_out, norm_cross_w, norm_mem_w, w_cross_q, w_cross_k, w_cross_v, w_cross_o, norm_mlp_w, w_mlp_up, w_mlp_down, norm_final_w):
    batch, seq, _ = x.shape
    assert w_in.shape[0] == 1, "single-layer block: per-layer parameters have a leading axis of 1"
    cos_t, sin_t = _rope_tables(seq)
    xs = x.reshape(batch * seq, D_MODEL)
    mem2d = mem.reshape(batch * MEM_TOKENS, D_MODEL)
    dec = jnp.stack([ret_decay_fwd[0], ret_decay_bwd[0]], axis=1)
    dec = jnp.broadcast_to(dec[:, :, None], (RET_HEADS, 2, RET_C)).astype(F32)
    proj = _in_proj(xs, norm_mix_w, _in_proj_weights(w_in[0]), cos_t, sin_t,
                    _pair_heads(attn_q_norm_w), _pair_heads(attn_k_norm_w), seq)
    y_ret, (w_out_bf16, w_cq_bf16, w_co_bf16) = _retention(
        proj, dec, ret_gn_w, ret_gn_b, (w_out[0], w_cross_q[0], w_cross_o[0]), batch, seq)
    score_bound = (LOG2E * math.sqrt(HEAD_DIM) * ATT_BOUND_MARGIN
                   * jnp.max(jnp.abs(attn_q_norm_w)) * jnp.max(jnp.abs(attn_k_norm_w)))
    score_bound = jnp.full((1, HEAD_DIM), score_bound, F32)
    y_attn, w_up_bf16, w_down_bf16 = _attention(proj, score_bound, w_mlp_up[0], w_mlp_down[0],
                                                batch, seq)
    mem_kv = _mem_kv(mem2d, norm_mem_w, w_cross_k[0], w_cross_v[0])
    xs = _out_cross(xs, y_ret, y_attn, w_out_bf16, norm_cross_w, w_cq_bf16, mem_kv, w_co_bf16, seq)
    xs = _mlp(xs, norm_mlp_w, w_up_bf16, w_down_bf16, norm_final_w[None, :])
    return xs.reshape(batch, seq, D_MODEL)
```

```python
import functools
import math

import numpy as np
import jax
import jax.numpy as jnp
from jax import lax
from jax.experimental import pallas as pl
from jax.experimental.pallas import tpu as pltpu

D_MODEL = 2048
HEAD_DIM = 128
RET_WIDTH = 1024
ATTN_WIDTH = 1024
RET_HEADS = 8
ATTN_HEADS = 8
ATTN_KV_HEADS = 2
GQA_GROUP = ATTN_HEADS // ATTN_KV_HEADS
KV_WIDTH = ATTN_KV_HEADS * HEAD_DIM
IN_WIDTH = 4 * RET_WIDTH + ATTN_WIDTH + 2 * KV_WIDTH
GRID_W = 64
AXIS_DIM = HEAD_DIM // 2
ROPE_THETA = 10000.0
MEM_TOKENS = 256
CROSS_HEADS = 4
CROSS_HEAD_DIM = 128
CROSS_WIDTH = CROSS_HEADS * CROSS_HEAD_DIM
D_FF = 4 * D_MODEL
NORM_EPS = 1e-6

V7X_BF16_SUBLANES = 16
V7X_VMEM_BYTES = 64 * 1024 * 1024

F32 = jnp.float32
BF16 = jnp.bfloat16
LOG2E = math.log2(math.e)
NEG_BIG = -1e30

IN_TM = 512
IN_TN = 512
IN_NORM_RC = 256
WPREP_ROWS = 256
RET_C = 256
RET_TS = 2048
RET_HG = 4
ATT_TQ = 512
ATT_TK = 4096
ATT_DENOM_FLOOR = 2.0 ** -40
ATT_BOUND_MARGIN = 1.0 + 2.0 ** -6
OC_TM = 512
OC_RC = 512
MLP_TM = 1024
MLP_TF = 1024
MLP_EDGE_RC = 512


VMEM_TEMPORARIES = 16 * 1024 * 1024
VMEM_UNCLAIMED = 8 * 1024 * 1024


def _vmem_limit(nbytes):
    return int(min(nbytes + VMEM_TEMPORARIES, V7X_VMEM_BYTES - VMEM_UNCLAIMED))


def _rms_scale(y):
    return lax.rsqrt(jnp.mean(y * y, axis=-1, keepdims=True) + NORM_EPS)


def _slab_cast_specs(weights, steps, flat_step):
    specs, shapes, nbytes = [], [], 0
    for w in weights:
        assert w.shape[0] % steps == 0 and (w.shape[0] // steps) % V7X_BF16_SUBLANES == 0
        block = (w.shape[0] // steps, w.shape[1])
        specs.append(pl.BlockSpec(block, lambda *ids: (flat_step(*ids), 0)))
        shapes.append(jax.ShapeDtypeStruct(w.shape, BF16))
        nbytes += 2 * block[0] * block[1] * (4 + 2)
    return specs, shapes, nbytes


def _cast_slabs(src_refs, dst_refs):
    for src, dst in zip(src_refs, dst_refs):
        dst[...] = src[...].astype(BF16)


def _rope(y, cos, sin_signed):
    return y * cos + pltpu.roll(y, HEAD_DIM // 2, 1) * sin_signed


def _in_proj_kernel(x_ref, nw_ref, w_ref, cos_ref, sin_ref, qg_ref, kg_ref, o_ref, h_ref):
    for r in range(IN_TM // IN_NORM_RC):
        rows = slice(r * IN_NORM_RC, (r + 1) * IN_NORM_RC)
        xr = x_ref[rows, :]
        h_ref[rows, :] = (xr * _rms_scale(xr) * nw_ref[...]).astype(BF16)

    ret_k_scale = HEAD_DIM ** -0.5
    attn_q_scale = (HEAD_DIM ** -0.5) * LOG2E

    n_slabs = IN_TN // HEAD_DIM

    def run(j, epilogue):
        tile = slice(j * IN_TN, (j + 1) * IN_TN)
        acc = jnp.dot(h_ref[...], w_ref[:, tile], preferred_element_type=F32)
        slabs = epilogue([acc[:, s * HEAD_DIM:(s + 1) * HEAD_DIM] for s in range(n_slabs)],
                         cos_ref[...], sin_ref[...])
        for s, y in enumerate(slabs):
            c0 = j * IN_TN + s * HEAD_DIM
            o_ref[:, c0:c0 + HEAD_DIM] = y.astype(BF16)

    def ret_q(ys, cos, sin):
        return [_rope(y, cos, sin) for y in ys]

    def ret_k(ys, cos, sin):
        return [_rope(y, cos, sin) * ret_k_scale for y in ys]

    def plain(ys, cos, sin):
        return ys

    def head_inv_rms(y_a, y_b):
        width = 2 * HEAD_DIM
        r = lax.broadcasted_iota(jnp.int32, (width, width), 0) // HEAD_DIM
        c = lax.broadcasted_iota(jnp.int32, (width, width), 1) // HEAD_DIM
        ones_bd = jnp.where(r == c, 1.0, 0.0).astype(BF16)
        sq = jnp.concatenate([(y_a * y_a).astype(BF16), (y_b * y_b).astype(BF16)], axis=1)
        ms = jnp.dot(sq, ones_bd, preferred_element_type=F32) * (1.0 / HEAD_DIM)
        inv = lax.rsqrt(ms + NORM_EPS)
        return inv[:, :HEAD_DIM], inv[:, HEAD_DIM:]

    def normed_rope(y_a, y_b, gain, scale, cos, sin):
        inv_a, inv_b = head_inv_rms(y_a, y_b)
        return [_rope(y_a * gain, cos, sin) * (inv_a * scale),
                _rope(y_b * gain, cos, sin) * (inv_b * scale)]

    def attn_q(ys, cos, sin):
        gain = qg_ref[...]
        return (normed_rope(ys[0], ys[1], gain, attn_q_scale, cos, sin)
                + normed_rope(ys[2], ys[3], gain, attn_q_scale, cos, sin))

    def attn_kv(ys, cos, sin):
        return normed_rope(ys[0], ys[1], kg_ref[...], 1.0, cos, sin) + ys[ATTN_KV_HEADS:]

    nb = RET_WIDTH // IN_TN
    for j in range(IN_WIDTH // IN_TN):
        if j < nb:
            run(j, ret_q)
        elif j < 2 * nb:
            run(j, ret_k)
        elif j < 4 * nb:
            run(j, plain)
        elif j < 5 * nb:
            run(j, attn_q)
        else:
            run(j, attn_kv)


def _in_proj(x2d, norm_w, w_in, cos_t, sin_t, q_gain, k_gain, seq):
    n = x2d.shape[0]
    assert IN_WIDTH % IN_TN == 0 and 2 * KV_WIDTH == IN_TN and n % IN_TM == 0 and seq % IN_TM == 0
    tiles_per_seq = seq // IN_TM
    vmem = (2 * IN_TM * D_MODEL * 4 + IN_TM * D_MODEL * 2 + D_MODEL * IN_WIDTH * 2
            + 2 * IN_TM * IN_WIDTH * 2 + 4 * IN_TM * HEAD_DIM * 4)
    resident = pl.Buffered(1)
    return pl.pallas_call(
        _in_proj_kernel,
        out_shape=jax.ShapeDtypeStruct((n, IN_WIDTH), BF16),
        grid=(n // IN_TM,),
        in_specs=[
            pl.BlockSpec((IN_TM, D_MODEL), lambda i: (i, 0)),
            pl.BlockSpec((1, D_MODEL), lambda i: (0, 0)),
            pl.BlockSpec((D_MODEL, IN_WIDTH), lambda i: (0, 0), pipeline_mode=resident),
            pl.BlockSpec((IN_TM, HEAD_DIM), lambda i: (i % tiles_per_seq, 0)),
            pl.BlockSpec((IN_TM, HEAD_DIM), lambda i: (i % tiles_per_seq, 0)),
            pl.BlockSpec((1, HEAD_DIM), lambda i: (0, 0)),
            pl.BlockSpec((1, HEAD_DIM), lambda i: (0, 0)),
        ],
        out_specs=pl.BlockSpec((IN_TM, IN_WIDTH), lambda i: (i, 0)),
        scratch_shapes=[pltpu.VMEM((IN_TM, D_MODEL), BF16)],
        compiler_params=pltpu.CompilerParams(
            dimension_semantics=("arbitrary",),
            vmem_limit_bytes=_vmem_limit(vmem)),
        name="in_proj",
    )(x2d, norm_w, w_in, cos_t, sin_t, q_gain, k_gain)


def _retention_kernel(dec_ref, q_ref, k_ref, v_ref, g_ref, gw_ref, gb_ref,
                      wout_ref, wq_ref, wco_ref, o_ref, wout_o_ref, wq_o_ref, wco_o_ref,
                      d_ref, tab_ref, sb_ref, st_ref, *, n_seq):
    _cast_slabs((wout_ref, wq_ref, wco_ref), (wout_o_ref, wq_o_ref, wco_o_ref))
    s = pl.program_id(2)
    c_len = RET_C
    cps = RET_TS // RET_C
    contract_rows = (((0,), (0,)), ((), ()))
    contract_cols = (((1,), (1,)), ((), ()))
    head_cols = [slice(h * HEAD_DIM, (h + 1) * HEAD_DIM) for h in range(RET_HG)]
    chunk_rows = [slice(c * c_len, (c + 1) * c_len) for c in range(cps)]

    def log_gammas(h):
        return -jnp.exp(dec_ref[h, 0:1, :]), -jnp.exp(dec_ref[h, 1:2, :])

    @pl.when(s == 0)
    def _():
        row = lax.broadcasted_iota(jnp.int32, (c_len, c_len), 0)
        col = lax.broadcasted_iota(jnp.int32, (c_len, c_len), 1)
        rel = (row - col).astype(F32)
        idx = lax.broadcasted_iota(jnp.int32, (c_len, HEAD_DIM), 0).astype(F32)
        for h in range(RET_HG):
            lg_f, lg_b = log_gammas(h)
            d_ref[h] = jnp.exp(jnp.where(rel >= 0, lg_f * rel, -lg_b * rel))
            lf = lg_f[:, :HEAD_DIM]
            lb = lg_b[:, :HEAD_DIM]
            tab_ref[h, 0] = jnp.exp(lf * (idx + 1.0)).astype(BF16)
            tab_ref[h, 1] = jnp.exp(lb * (c_len - idx)).astype(BF16)
            tab_ref[h, 2] = jnp.exp(lf * (c_len - 1.0 - idx)).astype(BF16)
            tab_ref[h, 3] = jnp.exp(lb * idx).astype(BF16)
        st_ref[...] = jnp.zeros(st_ref.shape, F32)

    @pl.when(s < n_seq)
    def _():
        first_chunk = (n_seq - 1 - s) * cps
        for h in range(RET_HG):
            chunk_b = jnp.exp(log_gammas(h)[1][:, :HEAD_DIM] * c_len)
            state = st_ref[h]
            for c in reversed(range(cps)):
                sb_ref[first_chunk + c, h] = state.astype(BF16)
                k = k_ref[chunk_rows[c], head_cols[h]]
                kd = k * tab_ref[h, 3]
                upd = lax.dot_general(kd, v_ref[chunk_rows[c], head_cols[h]], contract_rows,
                                      preferred_element_type=F32)
                state = state * chunk_b + upd
            st_ref[h] = state

        @pl.when(s == n_seq - 1)
        def _():
            st_ref[...] = jnp.zeros(st_ref.shape, F32)

    @pl.when(s >= n_seq)
    def _():
        first_chunk = (s - n_seq) * cps
        heads = range(RET_HG)
        chunk_f = [jnp.exp(log_gammas(h)[0][:, :HEAD_DIM] * c_len) for h in heads]
        state = [st_ref[h] for h in heads]
        for c in range(cps):
            blk = [(chunk_rows[c], head_cols[h]) for h in heads]
            scores = [lax.dot_general(q_ref[blk[h]], k_ref[blk[h]], contract_cols,
                                      preferred_element_type=F32) for h in heads]
            out = []
            for h in heads:
                q = q_ref[blk[h]]
                masked = (scores[h] * d_ref[h]).astype(BF16)
                lhs = jnp.concatenate([masked, q * tab_ref[h, 0], q * tab_ref[h, 1]], axis=1)
                rhs = jnp.concatenate(
                    [v_ref[blk[h]], state[h].astype(BF16), sb_ref[first_chunk + c, h]], axis=0)
                out.append(jnp.dot(lhs, rhs, preferred_element_type=F32))
            upd = [lax.dot_general(k_ref[blk[h]] * tab_ref[h, 2], v_ref[blk[h]], contract_rows,
                                   preferred_element_type=F32) for h in heads]
            state = [state[h] * chunk_f[h] + upd[h] for h in heads]
            cen = [out[h] - jnp.mean(out[h], axis=-1, keepdims=True) for h in heads]
            var = [jnp.mean(cen[h] * cen[h], axis=-1, keepdims=True) for h in heads]
            for h in heads:
                y = (cen[h] * lax.rsqrt(var[h] + NORM_EPS) * gw_ref[:, head_cols[h]]
                     + gb_ref[:, head_cols[h]])
                g = g_ref[blk[h]].astype(F32)
                y = y * (g / (1.0 + jnp.exp(-g)))
                o_ref[blk[h]] = y.astype(BF16)
        for h in heads:
            st_ref[h] = state[h]


def _retention(proj, dec, gn_w, gn_b, cross_weights, batch, seq):
    n = proj.shape[0]
    assert seq % RET_TS == 0 and RET_TS % RET_C == 0 and RET_HEADS % RET_HG == 0
    n_seq = seq // RET_TS
    n_chunks = seq // RET_C
    group_w = RET_HG * HEAD_DIM
    groups = RET_HEADS // RET_HG

    def kv_block(section):
        def index(b, hg, s):
            blk = jnp.where(s < n_seq, n_seq - 1 - s, s - n_seq)
            return (b * n_seq + blk, section * groups + hg)
        return index

    def fwd_block(section):
        return lambda b, hg, s: (b * n_seq + jnp.maximum(s - n_seq, 0), section * groups + hg)

    cast_specs, cast_shapes, cast_bytes = _slab_cast_specs(
        cross_weights, batch * groups * 2 * n_seq,
        lambda b, hg, s: (b * groups + hg) * 2 * n_seq + s)
    vmem = (2 * 5 * RET_TS * group_w * 2 + RET_HG * RET_C * RET_C * 4
            + RET_HG * 4 * RET_C * HEAD_DIM * 4 + n_chunks * RET_HG * HEAD_DIM * HEAD_DIM * 2
            + RET_HG * HEAD_DIM * HEAD_DIM * 4 + cast_bytes)
    y_ret, *cast = pl.pallas_call(
        functools.partial(_retention_kernel, n_seq=n_seq),
        out_shape=(jax.ShapeDtypeStruct((n, RET_WIDTH), BF16), *cast_shapes),
        grid=(batch, groups, 2 * n_seq),
        in_specs=[
            pl.BlockSpec((RET_HG, 2, RET_C), lambda b, hg, s: (hg, 0, 0)),
            pl.BlockSpec((RET_TS, group_w), fwd_block(0)),
            pl.BlockSpec((RET_TS, group_w), kv_block(1)),
            pl.BlockSpec((RET_TS, group_w), kv_block(2)),
            pl.BlockSpec((RET_TS, group_w), fwd_block(3)),
            pl.BlockSpec((1, group_w), lambda b, hg, s: (0, hg)),
            pl.BlockSpec((1, group_w), lambda b, hg, s: (0, hg)),
            *cast_specs,
        ],
        out_specs=(pl.BlockSpec((RET_TS, group_w), fwd_block(0)), *cast_specs),
        scratch_shapes=[
            pltpu.VMEM((RET_HG, RET_C, RET_C), F32),
            pltpu.VMEM((RET_HG, 4, RET_C, HEAD_DIM), BF16),
            pltpu.VMEM((n_chunks, RET_HG, HEAD_DIM, HEAD_DIM), BF16),
            pltpu.VMEM((RET_HG, HEAD_DIM, HEAD_DIM), F32),
        ],
        compiler_params=pltpu.CompilerParams(
            dimension_semantics=("arbitrary", "arbitrary", "arbitrary"),
            vmem_limit_bytes=_vmem_limit(vmem)),
        name="retention",
    )(dec, proj, proj, proj, proj, gn_w, gn_b, *cross_weights)
    return y_ret, cast


def _attention_kernel(q_ref, k_ref, v_ref, bound_ref, wu_ref, wd_ref, o_ref, wu_o_ref, wd_o_ref,
                      vt_ref, qa_ref, acc_ref, den_ref, *, n_kv):
    _cast_slabs((wu_ref, wd_ref), (wu_o_ref, wd_o_ref))

    i = pl.program_id(2)
    contract_cols = (((1,), (1,)), ((), ()))

    def kv_rows(c):
        return pl.ds(pl.multiple_of(c * ATT_TK, ATT_TK), ATT_TK)

    @pl.when(i == 0)
    def _():
        def body(c, carry):
            vt_ref[c] = v_ref[kv_rows(c), :].astype(F32).T.astype(BF16)
            return carry
        lax.fori_loop(0, n_kv, body, 0)

    q_lane = lax.broadcasted_iota(jnp.int32, (ATT_TQ, HEAD_DIM), 1)
    k_lane = lax.broadcasted_iota(jnp.int32, (ATT_TK, HEAD_DIM), 1)
    k_one = jnp.where(k_lane == 0, 1.0, 0.0).astype(BF16)
    head_cols = [slice(h * HEAD_DIM, (h + 1) * HEAD_DIM) for h in range(GQA_GROUP)]

    def shift_column(shift):
        return jnp.where(q_lane == 0, -shift, 0.0).astype(BF16)

    def sweep(q_aug):
        acc_ref[...] = jnp.zeros(acc_ref.shape, F32)
        den_ref[...] = jnp.zeros(den_ref.shape, F32)

        def kv_step(c, carry):
            k_aug = jnp.concatenate([k_ref[kv_rows(c), :], k_one], axis=1)
            vt = vt_ref[c]
            def scores(h):
                return lax.dot_general(k_aug, q_aug(h), contract_cols,
                                       preferred_element_type=F32)

            s_next = scores(0)
            for h in range(GQA_GROUP):
                s_t = s_next
                if h + 1 < GQA_GROUP:
                    s_next = scores(h + 1)
                p_t = jnp.exp2(s_t)
                den_ref[h] += jnp.sum(p_t.reshape(ATT_TK // 8, 8, ATT_TQ), axis=0)
                acc_ref[h] += jnp.dot(vt, p_t.astype(BF16), preferred_element_type=F32)
            return carry
        lax.fori_loop(0, n_kv, kv_step, 0)

    bound_column = shift_column(bound_ref[...])
    sweep(lambda h: jnp.concatenate([q_ref[:, head_cols[h]], bound_column], axis=1))

    def denominator(h):
        return jnp.sum(den_ref[h], axis=0, keepdims=True)

    denom_min = functools.reduce(jnp.minimum,
                                 [jnp.min(denominator(h)) for h in range(GQA_GROUP)])

    def write_output():
        inv_den = [1.0 / denominator(h) for h in range(GQA_GROUP)]
        o_t = [acc_ref[h] * inv_den[h] for h in range(GQA_GROUP)]
        for h in range(GQA_GROUP):
            o_ref[:, head_cols[h]] = o_t[h].T.astype(BF16)

    write_output()

    @pl.when(denom_min < ATT_DENOM_FLOOR)
    def _():
        for h in range(GQA_GROUP):
            def body(c, m):
                s = lax.dot_general(q_ref[:, head_cols[h]], k_ref[kv_rows(c), :], contract_cols,
                                    preferred_element_type=F32)
                return jnp.maximum(m, jnp.max(s, axis=1, keepdims=True))
            row_max = lax.fori_loop(0, n_kv, body, jnp.full((ATT_TQ, 1), NEG_BIG, F32))
            qa_ref[h, :, 0:HEAD_DIM] = q_ref[:, head_cols[h]]
            qa_ref[h, :, HEAD_DIM:] = shift_column(row_max)
        sweep(lambda h: qa_ref[h])
        write_output()


def _attention(proj, score_bound, w_up, w_down, batch, seq):
    n = proj.shape[0]
    assert seq % ATT_TQ == 0 and seq % ATT_TK == 0
    n_q = seq // ATT_TQ
    n_kv = seq // ATT_TK
    group_w = GQA_GROUP * HEAD_DIM
    q_col0 = 4 * RET_WIDTH // group_w
    k_col0 = (4 * RET_WIDTH + ATTN_WIDTH) // HEAD_DIM
    v_col0 = k_col0 + ATTN_KV_HEADS
    cast_specs, cast_shapes, cast_bytes = _slab_cast_specs(
        (w_up, w_down), batch * ATTN_KV_HEADS * n_q,
        lambda b, g, i: (b * ATTN_KV_HEADS + g) * n_q + i)
    vmem = (2 * ATT_TQ * group_w * 2 * 2 + 2 * 2 * seq * HEAD_DIM * 2
            + HEAD_DIM * seq * 2 + GQA_GROUP * (HEAD_DIM + 8) * ATT_TQ * 4
            + GQA_GROUP * ATT_TQ * 2 * HEAD_DIM * 2 + 2 * GQA_GROUP * ATT_TK * ATT_TQ * 4
            + cast_bytes)
    return pl.pallas_call(
        functools.partial(_attention_kernel, n_kv=n_kv),
        out_shape=(jax.ShapeDtypeStruct((n, ATTN_WIDTH), BF16), *cast_shapes),
        grid=(batch, ATTN_KV_HEADS, n_q),
        in_specs=[
            pl.BlockSpec((ATT_TQ, group_w), lambda b, g, i: (b * n_q + i, q_col0 + g)),
            pl.BlockSpec((seq, HEAD_DIM), lambda b, g, i: (b, k_col0 + g)),
            pl.BlockSpec((seq, HEAD_DIM), lambda b, g, i: (b, v_col0 + g)),
            pl.BlockSpec((1, HEAD_DIM), lambda b, g, i: (0, 0)),
            *cast_specs,
        ],
        out_specs=(pl.BlockSpec((ATT_TQ, group_w), lambda b, g, i: (b * n_q + i, g)),
                   *cast_specs),
        scratch_shapes=[
            pltpu.VMEM((n_kv, HEAD_DIM, ATT_TK), BF16),
            pltpu.VMEM((GQA_GROUP, ATT_TQ, 2 * HEAD_DIM), BF16),
            pltpu.VMEM((GQA_GROUP, HEAD_DIM, ATT_TQ), F32),
            pltpu.VMEM((GQA_GROUP, 8, ATT_TQ), F32),
        ],
        compiler_params=pltpu.CompilerParams(
            dimension_semantics=("arbitrary", "arbitrary", "arbitrary"),
            vmem_limit_bytes=_vmem_limit(vmem)),
        name="attention",
    )(proj, proj, proj, score_bound, w_up, w_down)


def _mem_kv_kernel(m_ref, nw_ref, wk_ref, wv_ref, o_ref):
    m = m_ref[...]
    h = (m * _rms_scale(m) * nw_ref[...]).astype(BF16)
    o_ref[:, :CROSS_WIDTH] = jnp.dot(h, wk_ref[...].astype(BF16),
                                     preferred_element_type=F32).astype(BF16)
    o_ref[:, CROSS_WIDTH:] = jnp.dot(h, wv_ref[...].astype(BF16),
                                     preferred_element_type=F32).astype(BF16)


def _mem_kv(mem2d, norm_w, w_k, w_v):
    rows = mem2d.shape[0]
    vmem = 2 * (rows * D_MODEL * 4 + 2 * D_MODEL * CROSS_WIDTH * 4 + rows * 2 * CROSS_WIDTH * 2)
    return pl.pallas_call(
        _mem_kv_kernel,
        out_shape=jax.ShapeDtypeStruct((rows, 2 * CROSS_WIDTH), BF16),
        grid=(1,),
        in_specs=[
            pl.BlockSpec((rows, D_MODEL), lambda i: (0, 0)),
            pl.BlockSpec((1, D_MODEL), lambda i: (0, 0)),
            pl.BlockSpec((D_MODEL, CROSS_WIDTH), lambda i: (0, 0)),
            pl.BlockSpec((D_MODEL, CROSS_WIDTH), lambda i: (0, 0)),
        ],
        out_specs=pl.BlockSpec((rows, 2 * CROSS_WIDTH), lambda i: (0, 0)),
        compiler_params=pltpu.CompilerParams(
            dimension_semantics=("arbitrary",),
            vmem_limit_bytes=_vmem_limit(vmem)),
        name="mem_kv",
    )(mem2d, norm_w, w_k, w_v)


def _out_cross_kernel(x_ref, yr_ref, ya_ref, wo_ref, nw_ref, wq_ref, kv_ref, wco_ref, o_ref):
    contract_cols = (((1,), (1,)), ((), ()))
    scale = CROSS_HEAD_DIM ** -0.5

    for r in range(OC_TM // OC_RC):
        rows = slice(r * OC_RC, (r + 1) * OC_RC)
        x1 = (x_ref[rows, :]
              + jnp.dot(yr_ref[rows, :], wo_ref[0:RET_WIDTH, :], preferred_element_type=F32)
              + jnp.dot(ya_ref[rows, :], wo_ref[RET_WIDTH:, :], preferred_element_type=F32))
        h = (x1 * _rms_scale(x1) * nw_ref[...]).astype(BF16)
        q = (jnp.dot(h, wq_ref[...], preferred_element_type=F32) * scale).astype(BF16)
        heads = range(CROSS_HEADS)
        cols = [slice(hd * CROSS_HEAD_DIM, (hd + 1) * CROSS_HEAD_DIM) for hd in heads]
        s = [lax.dot_general(q[:, cols[hd]], kv_ref[:, cols[hd]], contract_cols,
                             preferred_element_type=F32) for hd in heads]
        e = [jnp.exp(s[hd] - jnp.max(s[hd], axis=-1, keepdims=True)) for hd in heads]
        inv = [1.0 / jnp.sum(e[hd], axis=-1, keepdims=True) for hd in heads]
        pv = [jnp.dot(e[hd].astype(BF16),
                      kv_ref[:, CROSS_WIDTH + hd * CROSS_HEAD_DIM:CROSS_WIDTH + (hd + 1) * CROSS_HEAD_DIM],
                      preferred_element_type=F32) for hd in heads]
        o = jnp.concatenate([(pv[hd] * inv[hd]).astype(BF16) for hd in heads], axis=1)
        o_ref[rows, :] = x1 + jnp.dot(o, wco_ref[...], preferred_element_type=F32)


def _out_cross(x2d, y_ret, y_attn, w_out, norm_w, w_q, mem_kv, w_co, seq):
    n = x2d.shape[0]
    assert n % OC_TM == 0 and seq % OC_TM == 0
    tiles_per_seq = seq // OC_TM
    vmem = (2 * 2 * OC_TM * D_MODEL * 4 + 2 * 2 * OC_TM * RET_WIDTH * 2 + 2 * D_MODEL * D_MODEL * 2
            + 2 * 2 * D_MODEL * CROSS_WIDTH * 2 + 2 * MEM_TOKENS * 2 * CROSS_WIDTH * 2)
    return pl.pallas_call(
        _out_cross_kernel,
        out_shape=jax.ShapeDtypeStruct((n, D_MODEL), F32),
        grid=(n // OC_TM,),
        in_specs=[
            pl.BlockSpec((OC_TM, D_MODEL), lambda i: (i, 0)),
            pl.BlockSpec((OC_TM, RET_WIDTH), lambda i: (i, 0)),
            pl.BlockSpec((OC_TM, ATTN_WIDTH), lambda i: (i, 0)),
            pl.BlockSpec((D_MODEL, D_MODEL), lambda i: (0, 0)),
            pl.BlockSpec((1, D_MODEL), lambda i: (0, 0)),
            pl.BlockSpec((D_MODEL, CROSS_WIDTH), lambda i: (0, 0)),
            pl.BlockSpec((MEM_TOKENS, 2 * CROSS_WIDTH), lambda i: (i // tiles_per_seq, 0)),
            pl.BlockSpec((CROSS_WIDTH, D_MODEL), lambda i: (0, 0)),
        ],
        out_specs=pl.BlockSpec((OC_TM, D_MODEL), lambda i: (i, 0)),
        compiler_params=pltpu.CompilerParams(
            dimension_semantics=("arbitrary",),
            vmem_limit_bytes=_vmem_limit(vmem)),
        name="out_cross",
    )(x2d, y_ret, y_attn, w_out, norm_w, w_q, mem_kv, w_co)


def _mlp_kernel(x_hbm, nw_ref, wu_ref, wd_ref, fw_ref, o_ref, h_ref, x_ref, x_sem):
    i = pl.program_id(0)
    j = pl.program_id(1)
    last = pl.num_programs(1) - 1

    def x_copy(tile):
        rows = pl.ds(pl.multiple_of(tile * MLP_TM, MLP_TM), MLP_TM)
        return pltpu.make_async_copy(x_hbm.at[rows, :], x_ref, x_sem)

    @pl.when((i == 0) & (j == 0))
    def _():
        x_copy(0).start()

    @pl.when(j == 0)
    def _():
        x_copy(i).wait()

    @pl.when((j == 1) & (i + 1 < pl.num_programs(0)))
    def _():
        x_copy(i + 1).start()

    def step(rows, is_first, is_last):
        if is_first:
            xr = x_ref[rows, :]
            h = (xr * _rms_scale(xr) * nw_ref[...]).astype(BF16)
            h_ref[rows, :] = h
            base = xr
        else:
            h = h_ref[rows, :]
            base = o_ref[rows, :]
        u = jnp.maximum(jnp.dot(h, wu_ref[...], preferred_element_type=F32), 0.0)
        y = base + jnp.dot((u * u).astype(BF16), wd_ref[...], preferred_element_type=F32)
        if is_last:
            y = y * _rms_scale(y) * fw_ref[...]
        o_ref[rows, :] = y

    edge_chunks = [slice(r * MLP_EDGE_RC, (r + 1) * MLP_EDGE_RC)
                   for r in range(MLP_TM // MLP_EDGE_RC)]

    @pl.when(j == 0)
    def _():
        for rows in edge_chunks:
            step(rows, True, False)

    @pl.when((j > 0) & (j < last))
    def _():
        step(slice(None), False, False)

    @pl.when(j == last)
    def _():
        for rows in edge_chunks:
            step(rows, False, True)


def _mlp(x2d, norm_w, w_up, w_down, final_w):
    n = x2d.shape[0]
    assert n % MLP_TM == 0 and D_FF % MLP_TF == 0 and D_FF // MLP_TF >= 3
    vmem = (3 * MLP_TM * D_MODEL * 4 + MLP_TM * D_MODEL * 2 + 2 * 2 * D_MODEL * MLP_TF * 2)
    return pl.pallas_call(
        _mlp_kernel,
        out_shape=jax.ShapeDtypeStruct((n, D_MODEL), F32),
        grid=(n // MLP_TM, D_FF // MLP_TF),
        in_specs=[
            pl.BlockSpec(memory_space=pl.ANY),
            pl.BlockSpec((1, D_MODEL), lambda i, j: (0, 0)),
            pl.BlockSpec((D_MODEL, MLP_TF), lambda i, j: (0, j)),
            pl.BlockSpec((MLP_TF, D_MODEL), lambda i, j: (j, 0)),
            pl.BlockSpec((1, D_MODEL), lambda i, j: (0, 0)),
        ],
        out_specs=pl.BlockSpec((MLP_TM, D_MODEL), lambda i, j: (i, 0)),
        scratch_shapes=[
            pltpu.VMEM((MLP_TM, D_MODEL), BF16),
            pltpu.VMEM((MLP_TM, D_MODEL), F32),
            pltpu.SemaphoreType.DMA(()),
        ],
        compiler_params=pltpu.CompilerParams(
            dimension_semantics=("arbitrary", "arbitrary"),
            vmem_limit_bytes=_vmem_limit(vmem)),
        name="mlp",
    )(x2d, norm_w, w_up, w_down, final_w)


def _rope_tables(seq):
    t = np.arange(seq)
    row = (t // GRID_W).astype(np.float64)
    col = (t % GRID_W).astype(np.float64)
    inv_freq = 1.0 / (ROPE_THETA ** (np.arange(0, AXIS_DIM, 2, dtype=np.float64) / AXIS_DIM))
    ang_r = row[:, None] * inv_freq[None, :]
    ang_c = col[:, None] * inv_freq[None, :]
    cos_t = np.concatenate([np.cos(ang_r), np.cos(ang_c), np.cos(ang_r), np.cos(ang_c)], axis=-1)
    sin_t = np.concatenate([-np.sin(ang_r), -np.sin(ang_c), np.sin(ang_r), np.sin(ang_c)], axis=-1)
    return jnp.asarray(cos_t, F32), jnp.asarray(sin_t, F32)


def _pair_heads(w):
    lead = w.shape[:-1]
    heads = w.shape[-1] // HEAD_DIM
    quarter = HEAD_DIM // 4
    w = w.reshape(lead + (heads, 2, 2, quarter))
    return jnp.swapaxes(w, -2, -3).reshape(lead + (heads * HEAD_DIM,))


def _w_in_prep_kernel(w_ref, o_ref):
    lane = lax.broadcasted_iota(jnp.int32, (WPREP_ROWS, HEAD_DIM), 1)
    quarter = HEAD_DIM // 4
    takes_c1 = (lane >= quarter) & (lane < 2 * quarter)
    takes_r2 = (lane >= 2 * quarter) & (lane < 3 * quarter)
    rope_slabs = (set(range(0, 2 * RET_HEADS))
                  | set(range(4 * RET_HEADS, 4 * RET_HEADS + ATTN_HEADS + ATTN_KV_HEADS)))
    for s in range(IN_WIDTH // HEAD_DIM):
        cols = slice(s * HEAD_DIM, (s + 1) * HEAD_DIM)
        y = w_ref[:, cols]
        if s in rope_slabs:
            y = jnp.where(takes_c1, pltpu.roll(y, HEAD_DIM - quarter, 1),
                          jnp.where(takes_r2, pltpu.roll(y, quarter, 1), y))
        o_ref[:, cols] = y.astype(BF16)


def _in_proj_weights(w_in):
    rows = w_in.shape[0]
    assert rows % WPREP_ROWS == 0
    vmem = 2 * WPREP_ROWS * IN_WIDTH * (4 + 2)
    return pl.pallas_call(
        _w_in_prep_kernel,
        out_shape=jax.ShapeDtypeStruct((rows, IN_WIDTH), BF16),
        grid=(rows // WPREP_ROWS,),
        in_specs=[pl.BlockSpec((WPREP_ROWS, IN_WIDTH), lambda i: (i, 0))],
        out_specs=pl.BlockSpec((WPREP_ROWS, IN_WIDTH), lambda i: (i, 0)),
        compiler_params=pltpu.CompilerParams(
            dimension_semantics=("arbitrary",),
            vmem_limit_bytes=_vmem_limit(vmem)),
        name="w_in_prep",
    )(w_in)


def kernel(x, mem, norm_mix_w, w_in, ret_decay_fwd, ret_decay_bwd, ret_gn_w, ret_gn_b, attn_q_norm_w, attn_k_norm_w, w_out, norm_cross_w, norm_mem_w, w_cross_q, w_cross_k, w_cross_v, w_cross_o, norm_mlp_w, w_mlp_up, w_mlp_down, norm_final_w):
    batch, seq, _ = x.shape
    assert w_in.shape[0] == 1, "single-layer block: per-layer parameters have a leading axis of 1"
    cos_t, sin_t = _rope_tables(seq)
    xs = x.reshape(batch * seq, D_MODEL)
    mem2d = mem.reshape(batch * MEM_TOKENS, D_MODEL)
    dec = jnp.stack([ret_decay_fwd[0], ret_decay_bwd[0]], axis=1)
    dec = jnp.broadcast_to(dec[:, :, None], (RET_HEADS, 2, RET_C)).astype(F32)
    proj = _in_proj(xs, norm_mix_w, _in_proj_weights(w_in[0]), cos_t, sin_t,
                    _pair_heads(attn_q_norm_w), _pair_heads(attn_k_norm_w), seq)
    y_ret, (w_out_bf16, w_cq_bf16, w_co_bf16) = _retention(
        proj, dec, ret_gn_w, ret_gn_b, (w_out[0], w_cross_q[0], w_cross_o[0]), batch, seq)
    score_bound = (LOG2E * math.sqrt(HEAD_DIM) * ATT_BOUND_MARGIN
                   * jnp.max(jnp.abs(attn_q_norm_w)) * jnp.max(jnp.abs(attn_k_norm_w)))
    score_bound = jnp.full((1, HEAD_DIM), score_bound, F32)
    y_attn, w_up_bf16, w_down_bf16 = _attention(proj, score_bound, w_mlp_up[0], w_mlp_down[0],
                                                batch, seq)
    mem_kv = _mem_kv(mem2d, norm_mem_w, w_cross_k[0], w_cross_v[0])
    xs = _out_cross(xs, y_ret, y_attn, w_out_bf16, norm_cross_w, w_cq_bf16, mem_kv, w_co_bf16, seq)
    xs = _mlp(xs, norm_mlp_w, w_up_bf16, w_down_bf16, norm_final_w[None, :])
    return xs.reshape(batch, seq, D_MODEL)
```

```python
import functools
import math

import numpy as np
import jax
import jax.numpy as jnp
from jax import lax
from jax.experimental import pallas as pl
from jax.experimental.pallas import tpu as pltpu

D_MODEL = 2048
HEAD_DIM = 128
RET_WIDTH = 1024
ATTN_WIDTH = 1024
RET_HEADS = 8
ATTN_HEADS = 8
ATTN_KV_HEADS = 2
GQA_GROUP = ATTN_HEADS // ATTN_KV_HEADS
KV_WIDTH = ATTN_KV_HEADS * HEAD_DIM
IN_WIDTH = 4 * RET_WIDTH + ATTN_WIDTH + 2 * KV_WIDTH
GRID_W = 64
AXIS_DIM = HEAD_DIM // 2
ROPE_THETA = 10000.0
MEM_TOKENS = 256
CROSS_HEADS = 4
CROSS_HEAD_DIM = 128
CROSS_WIDTH = CROSS_HEADS * CROSS_HEAD_DIM
D_FF = 4 * D_MODEL
NORM_EPS = 1e-6

V7X_BF16_SUBLANES = 16
V7X_VMEM_BYTES = 64 * 1024 * 1024

F32 = jnp.float32
BF16 = jnp.bfloat16
LOG2E = math.log2(math.e)
NEG_BIG = -1e30

IN_TM = 512
IN_TN = 512
IN_NORM_RC = 256
WPREP_ROWS = 256
RET_C = 256
RET_TS = 2048
RET_HG = 4
ATT_TQ = 512
ATT_TK = 4096
ATT_DENOM_FLOOR = 2.0 ** -40
ATT_BOUND_MARGIN = 1.0 + 2.0 ** -6
OC_TM = 512
OC_RC = 512
MLP_TM = 1024
MLP_TF = 1024
MLP_EDGE_RC = 512


VMEM_TEMPORARIES = 16 * 1024 * 1024
VMEM_UNCLAIMED = 8 * 1024 * 1024


def _vmem_limit(nbytes):
    return int(min(nbytes + VMEM_TEMPORARIES, V7X_VMEM_BYTES - VMEM_UNCLAIMED))


def _rms_scale(y):
    return lax.rsqrt(jnp.mean(y * y, axis=-1, keepdims=True) + NORM_EPS)


def _slab_cast_specs(weights, steps, flat_step):
    specs, shapes, nbytes = [], [], 0
    for w in weights:
        assert w.shape[0] % steps == 0 and (w.shape[0] // steps) % V7X_BF16_SUBLANES == 0
        block = (w.shape[0] // steps, w.shape[1])
        specs.append(pl.BlockSpec(block, lambda *ids: (flat_step(*ids), 0)))
        shapes.append(jax.ShapeDtypeStruct(w.shape, BF16))
        nbytes += 2 * block[0] * block[1] * (4 + 2)
    return specs, shapes, nbytes


def _cast_slabs(src_refs, dst_refs):
    for src, dst in zip(src_refs, dst_refs):
        dst[...] = src[...].astype(BF16)


def _rope(y, cos, sin_signed):
    return y * cos + pltpu.roll(y, HEAD_DIM // 2, 1) * sin_signed


def _in_proj_kernel(x_ref, nw_ref, w_ref, cos_ref, sin_ref, qg_ref, kg_ref, o_ref, h_ref):
    for r in range(IN_TM // IN_NORM_RC):
        rows = slice(r * IN_NORM_RC, (r + 1) * IN_NORM_RC)
        xr = x_ref[rows, :]
        h_ref[rows, :] = (xr * _rms_scale(xr) * nw_ref[...]).astype(BF16)

    ret_k_scale = HEAD_DIM ** -0.5
    attn_q_scale = (HEAD_DIM ** -0.5) * LOG2E

    n_slabs = IN_TN // HEAD_DIM

    def run(j, epilogue):
        tile = slice(j * IN_TN, (j + 1) * IN_TN)
        acc = jnp.dot(h_ref[...], w_ref[:, tile], preferred_element_type=F32)
        slabs = epilogue([acc[:, s * HEAD_DIM:(s + 1) * HEAD_DIM] for s in range(n_slabs)],
                         cos_ref[...], sin_ref[...])
        for s, y in enumerate(slabs):
            c0 = j * IN_TN + s * HEAD_DIM
            o_ref[:, c0:c0 + HEAD_DIM] = y.astype(BF16)

    def ret_q(ys, cos, sin):
        return [_rope(y, cos, sin) for y in ys]

    def ret_k(ys, cos, sin):
        return [_rope(y, cos, sin) * ret_k_scale for y in ys]

    def plain(ys, cos, sin):
        return ys

    def head_inv_rms(y_a, y_b):
        width = 2 * HEAD_DIM
        r = lax.broadcasted_iota(jnp.int32, (width, width), 0) // HEAD_DIM
        c = lax.broadcasted_iota(jnp.int32, (width, width), 1) // HEAD_DIM
        ones_bd = jnp.where(r == c, 1.0, 0.0).astype(BF16)
        sq = jnp.concatenate([(y_a * y_a).astype(BF16), (y_b * y_b).astype(BF16)], axis=1)
        ms = jnp.dot(sq, ones_bd, preferred_element_type=F32) * (1.0 / HEAD_DIM)
        inv = lax.rsqrt(ms + NORM_EPS)
        return inv[:, :HEAD_DIM], inv[:, HEAD_DIM:]

    def normed_rope(y_a, y_b, gain, scale, cos, sin):
        inv_a, inv_b = head_inv_rms(y_a, y_b)
        return [_rope(y_a * gain, cos, sin) * (inv_a * scale),
                _rope(y_b * gain, cos, sin) * (inv_b * scale)]

    def attn_q(ys, cos, sin):
        gain = qg_ref[...]
        return (normed_rope(ys[0], ys[1], gain, attn_q_scale, cos, sin)
                + normed_rope(ys[2], ys[3], gain, attn_q_scale, cos, sin))

    def attn_kv(ys, cos, sin):
        return normed_rope(ys[0], ys[1], kg_ref[...], 1.0, cos, sin) + ys[ATTN_KV_HEADS:]

    nb = RET_WIDTH // IN_TN
    for j in range(IN_WIDTH // IN_TN):
        if j < nb:
            run(j, ret_q)
        elif j < 2 * nb:
            run(j, ret_k)
        elif j < 4 * nb:
            run(j, plain)
        elif j < 5 * nb:
            run(j, attn_q)
        else:
            run(j, attn_kv)


def _in_proj(x2d, norm_w, w_in, cos_t, sin_t, q_gain, k_gain, seq):
    n = x2d.shape[0]
    assert IN_WIDTH % IN_TN == 0 and 2 * KV_WIDTH == IN_TN and n % IN_TM == 0 and seq % IN_TM == 0
    tiles_per_seq = seq // IN_TM
    vmem = (2 * IN_TM * D_MODEL * 4 + IN_TM * D_MODEL * 2 + D_MODEL * IN_WIDTH * 2
            + 2 * IN_TM * IN_WIDTH * 2 + 4 * IN_TM * HEAD_DIM * 4)
    resident = pl.Buffered(1)
    return pl.pallas_call(
        _in_proj_kernel,
        out_shape=jax.ShapeDtypeStruct((n, IN_WIDTH), BF16),
        grid=(n // IN_TM,),
        in_specs=[
            pl.BlockSpec((IN_TM, D_MODEL), lambda i: (i, 0)),
            pl.BlockSpec((1, D_MODEL), lambda i: (0, 0)),
            pl.BlockSpec((D_MODEL, IN_WIDTH), lambda i: (0, 0), pipeline_mode=resident),
            pl.BlockSpec((IN_TM, HEAD_DIM), lambda i: (i % tiles_per_seq, 0)),
            pl.BlockSpec((IN_TM, HEAD_DIM), lambda i: (i % tiles_per_seq, 0)),
            pl.BlockSpec((1, HEAD_DIM), lambda i: (0, 0)),
            pl.BlockSpec((1, HEAD_DIM), lambda i: (0, 0)),
        ],
        out_specs=pl.BlockSpec((IN_TM, IN_WIDTH), lambda i: (i, 0)),
        scratch_shapes=[pltpu.VMEM((IN_TM, D_MODEL), BF16)],
        compiler_params=pltpu.CompilerParams(
            dimension_semantics=("arbitrary",),
            vmem_limit_bytes=_vmem_limit(vmem)),
        name="in_proj",
    )(x2d, norm_w, w_in, cos_t, sin_t, q_gain, k_gain)


def _retention_kernel(dec_ref, q_ref, k_ref, v_ref, g_ref, gw_ref, gb_ref,
                      wout_ref, wq_ref, wco_ref, o_ref, wout_o_ref, wq_o_ref, wco_o_ref,
                      d_ref, tab_ref, sb_ref, st_ref, *, n_seq):
    _cast_slabs((wout_ref, wq_ref, wco_ref), (wout_o_ref, wq_o_ref, wco_o_ref))
    s = pl.program_id(2)
    c_len = RET_C
    cps = RET_TS // RET_C
    contract_rows = (((0,), (0,)), ((), ()))
    contract_cols = (((1,), (1,)), ((), ()))
    head_cols = [slice(h * HEAD_DIM, (h + 1) * HEAD_DIM) for h in range(RET_HG)]
    chunk_rows = [slice(c * c_len, (c + 1) * c_len) for c in range(cps)]

    def log_gammas(h):
        return -jnp.exp(dec_ref[h, 0:1, :]), -jnp.exp(dec_ref[h, 1:2, :])

    @pl.when(s == 0)
    def _():
        row = lax.broadcasted_iota(jnp.int32, (c_len, c_len), 0)
        col = lax.broadcasted_iota(jnp.int32, (c_len, c_len), 1)
        rel = (row - col).astype(F32)
        idx = lax.broadcasted_iota(jnp.int32, (c_len, HEAD_DIM), 0).astype(F32)
        for h in range(RET_HG):
            lg_f, lg_b = log_gammas(h)
            d_ref[h] = jnp.exp(jnp.where(rel >= 0, lg_f * rel, -lg_b * rel))
            lf = lg_f[:, :HEAD_DIM]
            lb = lg_b[:, :HEAD_DIM]
            tab_ref[h, 0] = jnp.exp(lf * (idx + 1.0)).astype(BF16)
            tab_ref[h, 1] = jnp.exp(lb * (c_len - idx)).astype(BF16)
            tab_ref[h, 2] = jnp.exp(lf * (c_len - 1.0 - idx)).astype(BF16)
            tab_ref[h, 3] = jnp.exp(lb * idx).astype(BF16)
        st_ref[...] = jnp.zeros(st_ref.shape, F32)

    @pl.when(s < n_seq)
    def _():
        first_chunk = (n_seq - 1 - s) * cps
        for h in range(RET_HG):
            chunk_b = jnp.exp(log_gammas(h)[1][:, :HEAD_DIM] * c_len)
            state = st_ref[h]
            for c in reversed(range(cps)):
                sb_ref[first_chunk + c, h] = state.astype(BF16)
                k = k_ref[chunk_rows[c], head_cols[h]]
                kd = k * tab_ref[h, 3]
                upd = lax.dot_general(kd, v_ref[chunk_rows[c], head_cols[h]], contract_rows,
                                      preferred_element_type=F32)
                state = state * chunk_b + upd
            st_ref[h] = state

        @pl.when(s == n_seq - 1)
        def _():
            st_ref[...] = jnp.zeros(st_ref.shape, F32)

    @pl.when(s >= n_seq)
    def _():
        first_chunk = (s - n_seq) * cps
        heads = range(RET_HG)
        chunk_f = [jnp.exp(log_gammas(h)[0][:, :HEAD_DIM] * c_len) for h in heads]
        state = [st_ref[h] for h in heads]
        for c in range(cps):
            blk = [(chunk_rows[c], head_cols[h]) for h in heads]
            scores = [lax.dot_general(q_ref[blk[h]], k_ref[blk[h]], contract_cols,
                                      preferred_element_type=F32) for h in heads]
            out = []
            for h in heads:
                q = q_ref[blk[h]]
                masked = (scores[h] * d_ref[h]).astype(BF16)
                lhs = jnp.concatenate([masked, q * tab_ref[h, 0], q * tab_ref[h, 1]], axis=1)
                rhs = jnp.concatenate(
                    [v_ref[blk[h]], state[h].astype(BF16), sb_ref[first_chunk + c, h]], axis=0)
                out.append(jnp.dot(lhs, rhs, preferred_element_type=F32))
            upd = [lax.dot_general(k_ref[blk[h]] * tab_ref[h, 2], v_ref[blk[h]], contract_rows,
                                   preferred_element_type=F32) for h in heads]
            state = [state[h] * chunk_f[h] + upd[h] for h in heads]
            cen = [out[h] - jnp.mean(out[h], axis=-1, keepdims=True) for h in heads]
            var = [jnp.mean(cen[h] * cen[h], axis=-1, keepdims=True) for h in heads]
            for h in heads:
                y = (cen[h] * lax.rsqrt(var[h] + NORM_EPS) * gw_ref[:, head_cols[h]]
                     + gb_ref[:, head_cols[h]])
                g = g_ref[blk[h]].astype(F32)
                y = y * (g / (1.0 + jnp.exp(-g)))
                o_ref[blk[h]] = y.astype(BF16)
        for h in heads:
            st_ref[h] = state[h]


def _retention(proj, dec, gn_w, gn_b, cross_weights, batch, seq):
    n = proj.shape[0]
    assert seq % RET_TS == 0 and RET_TS % RET_C == 0 and RET_HEADS % RET_HG == 0
    n_seq = seq // RET_TS
    n_chunks = seq // RET_C
    group_w = RET_HG * HEAD_DIM
    groups = RET_HEADS // RET_HG

    def kv_block(section):
        def index(b, hg, s):
            blk = jnp.where(s < n_seq, n_seq - 1 - s, s - n_seq)
            return (b * n_seq + blk, section * groups + hg)
        return index

    def fwd_block(section):
        return lambda b, hg, s: (b * n_seq + jnp.maximum(s - n_seq, 0), section * groups + hg)

    cast_specs, cast_shapes, cast_bytes = _slab_cast_specs(
        cross_weights, batch * groups * 2 * n_seq,
        lambda b, hg, s: (b * groups + hg) * 2 * n_seq + s)
    vmem = (2 * 5 * RET_TS * group_w * 2 + RET_HG * RET_C * RET_C * 4
            + RET_HG * 4 * RET_C * HEAD_DIM * 4 + n_chunks * RET_HG * HEAD_DIM * HEAD_DIM * 2
            + RET_HG * HEAD_DIM * HEAD_DIM * 4 + cast_bytes)
    y_ret, *cast = pl.pallas_call(
        functools.partial(_retention_kernel, n_seq=n_seq),
        out_shape=(jax.ShapeDtypeStruct((n, RET_WIDTH), BF16), *cast_shapes),
        grid=(batch, groups, 2 * n_seq),
        in_specs=[
            pl.BlockSpec((RET_HG, 2, RET_C), lambda b, hg, s: (hg, 0, 0)),
            pl.BlockSpec((RET_TS, group_w), fwd_block(0)),
            pl.BlockSpec((RET_TS, group_w), kv_block(1)),
            pl.BlockSpec((RET_TS, group_w), kv_block(2)),
            pl.BlockSpec((RET_TS, group_w), fwd_block(3)),
            pl.BlockSpec((1, group_w), lambda b, hg, s: (0, hg)),
            pl.BlockSpec((1, group_w), lambda b, hg, s: (0, hg)),
            *cast_specs,
        ],
        out_specs=(pl.BlockSpec((RET_TS, group_w), fwd_block(0)), *cast_specs),
        scratch_shapes=[
            pltpu.VMEM((RET_HG, RET_C, RET_C), F32),
            pltpu.VMEM((RET_HG, 4, RET_C, HEAD_DIM), BF16),
            pltpu.VMEM((n_chunks, RET_HG, HEAD_DIM, HEAD_DIM), BF16),
            pltpu.VMEM((RET_HG, HEAD_DIM, HEAD_DIM), F32),
        ],
        compiler_params=pltpu.CompilerParams(
            dimension_semantics=("arbitrary", "arbitrary", "arbitrary"),
            vmem_limit_bytes=_vmem_limit(vmem)),
        name="retention",
    )(dec, proj, proj, proj, proj, gn_w, gn_b, *cross_weights)
    return y_ret, cast


def _attention_kernel(q_ref, k_ref, v_ref, bound_ref, wu_ref, wd_ref, o_ref, wu_o_ref, wd_o_ref,
                      vt_ref, qa_ref, acc_ref, den_ref, *, n_kv):
    i = pl.program_id(2)
    contract_cols = (((1,), (1,)), ((), ()))

    def kv_rows(c):
        return pl.ds(pl.multiple_of(c * ATT_TK, ATT_TK), ATT_TK)

    @pl.when(i == 0)
    def _():
        def body(c, carry):
            vt_ref[c] = v_ref[kv_rows(c), :].astype(F32).T.astype(BF16)
            return carry
        lax.fori_loop(0, n_kv, body, 0)

    q_lane = lax.broadcasted_iota(jnp.int32, (ATT_TQ, HEAD_DIM), 1)
    k_lane = lax.broadcasted_iota(jnp.int32, (ATT_TK, HEAD_DIM), 1)
    k_one = jnp.where(k_lane == 0, 1.0, 0.0).astype(BF16)
    head_cols = [slice(h * HEAD_DIM, (h + 1) * HEAD_DIM) for h in range(GQA_GROUP)]

    def shift_column(shift):
        return jnp.where(q_lane == 0, -shift, 0.0).astype(BF16)

    def sweep(q_aug):
        acc_ref[...] = jnp.zeros(acc_ref.shape, F32)
        den_ref[...] = jnp.zeros(den_ref.shape, F32)

        def kv_step(c, carry):
            k_aug = jnp.concatenate([k_ref[kv_rows(c), :], k_one], axis=1)
            vt = vt_ref[c]
            def scores(h):
                return lax.dot_general(k_aug, q_aug(h), contract_cols,
                                       preferred_element_type=F32)

            s_next = scores(0)
            for h in range(GQA_GROUP):
                s_t = s_next
                if h + 1 < GQA_GROUP:
                    s_next = scores(h + 1)
                p_t = jnp.exp2(s_t)
                den_ref[h] += jnp.sum(p_t.reshape(ATT_TK // 8, 8, ATT_TQ), axis=0)
                acc_ref[h] += jnp.dot(vt, p_t.astype(BF16), preferred_element_type=F32)
            return carry
        lax.fori_loop(0, n_kv, kv_step, 0)

    bound_column = shift_column(bound_ref[...])
    sweep(lambda h: jnp.concatenate([q_ref[:, head_cols[h]], bound_column], axis=1))

    def denominator(h):
        return jnp.sum(den_ref[h], axis=0, keepdims=True)

    denom_min = functools.reduce(jnp.minimum,
                                 [jnp.min(denominator(h)) for h in range(GQA_GROUP)])

    def write_output():
        inv_den = [1.0 / denominator(h) for h in range(GQA_GROUP)]
        o_t = [acc_ref[h] * inv_den[h] for h in range(GQA_GROUP)]
        for h in range(GQA_GROUP):
            o_ref[:, head_cols[h]] = o_t[h].T.astype(BF16)

    write_output()
    _cast_slabs((wu_ref, wd_ref), (wu_o_ref, wd_o_ref))

    @pl.when(denom_min < ATT_DENOM_FLOOR)
    def _():
        for h in range(GQA_GROUP):
            def body(c, m):
                s = lax.dot_general(q_ref[:, head_cols[h]], k_ref[kv_rows(c), :], contract_cols,
                                    preferred_element_type=F32)
                return jnp.maximum(m, jnp.max(s, axis=1, keepdims=True))
            row_max = lax.fori_loop(0, n_kv, body, jnp.full((ATT_TQ, 1), NEG_BIG, F32))
            qa_ref[h, :, 0:HEAD_DIM] = q_ref[:, head_cols[h]]
            qa_ref[h, :, HEAD_DIM:] = shift_column(row_max)
        sweep(lambda h: qa_ref[h])
        write_output()


def _attention(proj, score_bound, w_up, w_down, batch, seq):
    n = proj.shape[0]
    assert seq % ATT_TQ == 0 and seq % ATT_TK == 0
    n_q = seq // ATT_TQ
    n_kv = seq // ATT_TK
    group_w = GQA_GROUP * HEAD_DIM
    q_col0 = 4 * RET_WIDTH // group_w
    k_col0 = (4 * RET_WIDTH + ATTN_WIDTH) // HEAD_DIM
    v_col0 = k_col0 + ATTN_KV_HEADS
    cast_specs, cast_shapes, cast_bytes = _slab_cast_specs(
        (w_up, w_down), batch * ATTN_KV_HEADS * n_q,
        lambda b, g, i: (b * ATTN_KV_HEADS + g) * n_q + i)
    vmem = (2 * ATT_TQ * group_w * 2 * 2 + 2 * 2 * seq * HEAD_DIM * 2
            + HEAD_DIM * seq * 2 + GQA_GROUP * (HEAD_DIM + 8) * ATT_TQ * 4
            + GQA_GROUP * ATT_TQ * 2 * HEAD_DIM * 2 + 2 * GQA_GROUP * ATT_TK * ATT_TQ * 4
            + cast_bytes)
    return pl.pallas_call(
        functools.partial(_attention_kernel, n_kv=n_kv),
        out_shape=(jax.ShapeDtypeStruct((n, ATTN_WIDTH), BF16), *cast_shapes),
        grid=(batch, ATTN_KV_HEADS, n_q),
        in_specs=[
            pl.BlockSpec((ATT_TQ, group_w), lambda b, g, i: (b * n_q + i, q_col0 + g)),
            pl.BlockSpec((seq, HEAD_DIM), lambda b, g, i: (b, k_col0 + g)),
            pl.BlockSpec((seq, HEAD_DIM), lambda b, g, i: (b, v_col0 + g)),
            pl.BlockSpec((1, HEAD_DIM), lambda b, g, i: (0, 0)),
            *cast_specs,
        ],
        out_specs=(pl.BlockSpec((ATT_TQ, group_w), lambda b, g, i: (b * n_q + i, g)),
                   *cast_specs),
        scratch_shapes=[
            pltpu.VMEM((n_kv, HEAD_DIM, ATT_TK), BF16),
            pltpu.VMEM((GQA_GROUP, ATT_TQ, 2 * HEAD_DIM), BF16),
            pltpu.VMEM((GQA_GROUP, HEAD_DIM, ATT_TQ), F32),
            pltpu.VMEM((GQA_GROUP, 8, ATT_TQ), F32),
        ],
        compiler_params=pltpu.CompilerParams(
            dimension_semantics=("arbitrary", "arbitrary", "arbitrary"),
            vmem_limit_bytes=_vmem_limit(vmem)),
        name="attention",
    )(proj, proj, proj, score_bound, w_up, w_down)


def _mem_kv_kernel(m_ref, nw_ref, wk_ref, wv_ref, o_ref):
    m = m_ref[...]
    h = (m * _rms_scale(m) * nw_ref[...]).astype(BF16)
    o_ref[:, :CROSS_WIDTH] = jnp.dot(h, wk_ref[...].astype(BF16),
                                     preferred_element_type=F32).astype(BF16)
    o_ref[:, CROSS_WIDTH:] = jnp.dot(h, wv_ref[...].astype(BF16),
                                     preferred_element_type=F32).astype(BF16)


def _mem_kv(mem2d, norm_w, w_k, w_v):
    rows = mem2d.shape[0]
    vmem = 2 * (rows * D_MODEL * 4 + 2 * D_MODEL * CROSS_WIDTH * 4 + rows * 2 * CROSS_WIDTH * 2)
    return pl.pallas_call(
        _mem_kv_kernel,
        out_shape=jax.ShapeDtypeStruct((rows, 2 * CROSS_WIDTH), BF16),
        grid=(1,),
        in_specs=[
            pl.BlockSpec((rows, D_MODEL), lambda i: (0, 0)),
            pl.BlockSpec((1, D_MODEL), lambda i: (0, 0)),
            pl.BlockSpec((D_MODEL, CROSS_WIDTH), lambda i: (0, 0)),
            pl.BlockSpec((D_MODEL, CROSS_WIDTH), lambda i: (0, 0)),
        ],
        out_specs=pl.BlockSpec((rows, 2 * CROSS_WIDTH), lambda i: (0, 0)),
        compiler_params=pltpu.CompilerParams(
            dimension_semantics=("arbitrary",),
            vmem_limit_bytes=_vmem_limit(vmem)),
        name="mem_kv",
    )(mem2d, norm_w, w_k, w_v)


def _out_cross_kernel(x_ref, yr_ref, ya_ref, wo_ref, nw_ref, wq_ref, kv_ref, wco_ref, o_ref):
    contract_cols = (((1,), (1,)), ((), ()))
    scale = CROSS_HEAD_DIM ** -0.5

    for r in range(OC_TM // OC_RC):
        rows = slice(r * OC_RC, (r + 1) * OC_RC)
        x1 = (x_ref[rows, :]
              + jnp.dot(yr_ref[rows, :], wo_ref[0:RET_WIDTH, :], preferred_element_type=F32)
              + jnp.dot(ya_ref[rows, :], wo_ref[RET_WIDTH:, :], preferred_element_type=F32))
        h = (x1 * _rms_scale(x1) * nw_ref[...]).astype(BF16)
        q = (jnp.dot(h, wq_ref[...], preferred_element_type=F32) * scale).astype(BF16)
        heads = range(CROSS_HEADS)
        cols = [slice(hd * CROSS_HEAD_DIM, (hd + 1) * CROSS_HEAD_DIM) for hd in heads]
        s = [lax.dot_general(q[:, cols[hd]], kv_ref[:, cols[hd]], contract_cols,
                             preferred_element_type=F32) for hd in heads]
        e = [jnp.exp(s[hd] - jnp.max(s[hd], axis=-1, keepdims=True)) for hd in heads]
        inv = [1.0 / jnp.sum(e[hd], axis=-1, keepdims=True) for hd in heads]
        pv = [jnp.dot(e[hd].astype(BF16),
                      kv_ref[:, CROSS_WIDTH + hd * CROSS_HEAD_DIM:CROSS_WIDTH + (hd + 1) * CROSS_HEAD_DIM],
                      preferred_element_type=F32) for hd in heads]
        o = jnp.concatenate([(pv[hd] * inv[hd]).astype(BF16) for hd in heads], axis=1)
        o_ref[rows, :] = x1 + jnp.dot(o, wco_ref[...], preferred_element_type=F32)


def _out_cross(x2d, y_ret, y_attn, w_out, norm_w, w_q, mem_kv, w_co, seq):
    n = x2d.shape[0]
    assert n % OC_TM == 0 and seq % OC_TM == 0
    tiles_per_seq = seq // OC_TM
    vmem = (2 * 2 * OC_TM * D_MODEL * 4 + 2 * 2 * OC_TM * RET_WIDTH * 2 + 2 * D_MODEL * D_MODEL * 2
            + 2 * 2 * D_MODEL * CROSS_WIDTH * 2 + 2 * MEM_TOKENS * 2 * CROSS_WIDTH * 2)
    return pl.pallas_call(
        _out_cross_kernel,
        out_shape=jax.ShapeDtypeStruct((n, D_MODEL), F32),
        grid=(n // OC_TM,),
        in_specs=[
            pl.BlockSpec((OC_TM, D_MODEL), lambda i: (i, 0)),
            pl.BlockSpec((OC_TM, RET_WIDTH), lambda i: (i, 0)),
            pl.BlockSpec((OC_TM, ATTN_WIDTH), lambda i: (i, 0)),
            pl.BlockSpec((D_MODEL, D_MODEL), lambda i: (0, 0)),
            pl.BlockSpec((1, D_MODEL), lambda i: (0, 0)),
            pl.BlockSpec((D_MODEL, CROSS_WIDTH), lambda i: (0, 0)),
            pl.BlockSpec((MEM_TOKENS, 2 * CROSS_WIDTH), lambda i: (i // tiles_per_seq, 0)),
            pl.BlockSpec((CROSS_WIDTH, D_MODEL), lambda i: (0, 0)),
        ],
        out_specs=pl.BlockSpec((OC_TM, D_MODEL), lambda i: (i, 0)),
        compiler_params=pltpu.CompilerParams(
            dimension_semantics=("arbitrary",),
            vmem_limit_bytes=_vmem_limit(vmem)),
        name="out_cross",
    )(x2d, y_ret, y_attn, w_out, norm_w, w_q, mem_kv, w_co)


def _mlp_kernel(x_hbm, nw_ref, wu_ref, wd_ref, fw_ref, o_ref, h_ref, x_ref, x_sem):
    i = pl.program_id(0)
    j = pl.program_id(1)
    last = pl.num_programs(1) - 1

    def x_copy(tile):
        rows = pl.ds(pl.multiple_of(tile * MLP_TM, MLP_TM), MLP_TM)
        return pltpu.make_async_copy(x_hbm.at[rows, :], x_ref, x_sem)

    @pl.when((i == 0) & (j == 0))
    def _():
        x_copy(0).start()

    @pl.when(j == 0)
    def _():
        x_copy(i).wait()

    @pl.when((j == 1) & (i + 1 < pl.num_programs(0)))
    def _():
        x_copy(i + 1).start()

    def step(rows, is_first, is_last):
        if is_first:
            xr = x_ref[rows, :]
            h = (xr * _rms_scale(xr) * nw_ref[...]).astype(BF16)
            h_ref[rows, :] = h
            base = xr
        else:
            h = h_ref[rows, :]
            base = o_ref[rows, :]
        u = jnp.maximum(jnp.dot(h, wu_ref[...], preferred_element_type=F32), 0.0)
        y = base + jnp.dot((u * u).astype(BF16), wd_ref[...], preferred_element_type=F32)
        if is_last:
            y = y * _rms_scale(y) * fw_ref[...]
        o_ref[rows, :] = y

    edge_chunks = [slice(r * MLP_EDGE_RC, (r + 1) * MLP_EDGE_RC)
                   for r in range(MLP_TM // MLP_EDGE_RC)]

    @pl.when(j == 0)
    def _():
        for rows in edge_chunks:
            step(rows, True, False)

    @pl.when((j > 0) & (j < last))
    def _():
        step(slice(None), False, False)

    @pl.when(j == last)
    def _():
        for rows in edge_chunks:
            step(rows, False, True)


def _mlp(x2d, norm_w, w_up, w_down, final_w):
    n = x2d.shape[0]
    assert n % MLP_TM == 0 and D_FF % MLP_TF == 0 and D_FF // MLP_TF >= 3
    vmem = (3 * MLP_TM * D_MODEL * 4 + MLP_TM * D_MODEL * 2 + 2 * 2 * D_MODEL * MLP_TF * 2)
    return pl.pallas_call(
        _mlp_kernel,
        out_shape=jax.ShapeDtypeStruct((n, D_MODEL), F32),
        grid=(n // MLP_TM, D_FF // MLP_TF),
        in_specs=[
            pl.BlockSpec(memory_space=pl.ANY),
            pl.BlockSpec((1, D_MODEL), lambda i, j: (0, 0)),
            pl.BlockSpec((D_MODEL, MLP_TF), lambda i, j: (0, j)),
            pl.BlockSpec((MLP_TF, D_MODEL), lambda i, j: (j, 0)),
            pl.BlockSpec((1, D_MODEL), lambda i, j: (0, 0)),
        ],
        out_specs=pl.BlockSpec((MLP_TM, D_MODEL), lambda i, j: (i, 0)),
        scratch_shapes=[
            pltpu.VMEM((MLP_TM, D_MODEL), BF16),
            pltpu.VMEM((MLP_TM, D_MODEL), F32),
            pltpu.SemaphoreType.DMA(()),
        ],
        compiler_params=pltpu.CompilerParams(
            dimension_semantics=("arbitrary", "arbitrary"),
            vmem_limit_bytes=_vmem_limit(vmem)),
        name="mlp",
    )(x2d, norm_w, w_up, w_down, final_w)


def _rope_tables(seq):
    t = np.arange(seq)
    row = (t // GRID_W).astype(np.float64)
    col = (t % GRID_W).astype(np.float64)
    inv_freq = 1.0 / (ROPE_THETA ** (np.arange(0, AXIS_DIM, 2, dtype=np.float64) / AXIS_DIM))
    ang_r = row[:, None] * inv_freq[None, :]
    ang_c = col[:, None] * inv_freq[None, :]
    cos_t = np.concatenate([np.cos(ang_r), np.cos(ang_c), np.cos(ang_r), np.cos(ang_c)], axis=-1)
    sin_t = np.concatenate([-np.sin(ang_r), -np.sin(ang_c), np.sin(ang_r), np.sin(ang_c)], axis=-1)
    return jnp.asarray(cos_t, F32), jnp.asarray(sin_t, F32)


def _pair_heads(w):
    lead = w.shape[:-1]
    heads = w.shape[-1] // HEAD_DIM
    quarter = HEAD_DIM // 4
    w = w.reshape(lead + (heads, 2, 2, quarter))
    return jnp.swapaxes(w, -2, -3).reshape(lead + (heads * HEAD_DIM,))


def _w_in_prep_kernel(w_ref, o_ref):
    lane = lax.broadcasted_iota(jnp.int32, (WPREP_ROWS, HEAD_DIM), 1)
    quarter = HEAD_DIM // 4
    takes_c1 = (lane >= quarter) & (lane < 2 * quarter)
    takes_r2 = (lane >= 2 * quarter) & (lane < 3 * quarter)
    rope_slabs = (set(range(0, 2 * RET_HEADS))
                  | set(range(4 * RET_HEADS, 4 * RET_HEADS + ATTN_HEADS + ATTN_KV_HEADS)))
    for s in range(IN_WIDTH // HEAD_DIM):
        cols = slice(s * HEAD_DIM, (s + 1) * HEAD_DIM)
        y = w_ref[:, cols]
        if s in rope_slabs:
            y = jnp.where(takes_c1, pltpu.roll(y, HEAD_DIM - quarter, 1),
                          jnp.where(takes_r2, pltpu.roll(y, quarter, 1), y))
        o_ref[:, cols] = y.astype(BF16)


def _in_proj_weights(w_in):
    rows = w_in.shape[0]
    assert rows % WPREP_ROWS == 0
    vmem = 2 * WPREP_ROWS * IN_WIDTH * (4 + 2)
    return pl.pallas_call(
        _w_in_prep_kernel,
        out_shape=jax.ShapeDtypeStruct((rows, IN_WIDTH), BF16),
        grid=(rows // WPREP_ROWS,),
        in_specs=[pl.BlockSpec((WPREP_ROWS, IN_WIDTH), lambda i: (i, 0))],
        out_specs=pl.BlockSpec((WPREP_ROWS, IN_WIDTH), lambda i: (i, 0)),
        compiler_params=pltpu.CompilerParams(
            dimension_semantics=("arbitrary",),
            vmem_limit_bytes=_vmem_limit(vmem)),
        name="w_in_prep",
    )(w_in)


def kernel(x, mem, norm_mix_w, w_in, ret_decay_fwd, ret_decay_bwd, ret_gn_w, ret_gn_b, attn_q_norm_w, attn_k_norm_w, w_out, norm_cross_w, norm_mem_w, w_cross_q, w_cross_k, w_cross_v, w_cross_o, norm_mlp_w, w_mlp_up, w_mlp_down, norm_final_w):
    batch, seq, _ = x.shape
    assert w_in.shape[0] == 1, "single-layer block: per-layer parameters have a leading axis of 1"
    cos_t, sin_t = _rope_tables(seq)
    xs = x.reshape(batch * seq, D_MODEL)
    mem2d = mem.reshape(batch * MEM_TOKENS, D_MODEL)
    dec = jnp.stack([ret_decay_fwd[0], ret_decay_bwd[0]], axis=1)
    dec = jnp.broadcast_to(dec[:, :, None], (RET_HEADS, 2, RET_C)).astype(F32)
    proj = _in_proj(xs, norm_mix_w, _in_proj_weights(w_in[0]), cos_t, sin_t,
                    _pair_heads(attn_q_norm_w), _pair_heads(attn_k_norm_w), seq)
    y_ret, (w_out_bf16, w_cq_bf16, w_co_bf16) = _retention(
        proj, dec, ret_gn_w, ret_gn_b, (w_out[0], w_cross_q[0], w_cross_o[0]), batch, seq)
    score_bound = (LOG2E * math.sqrt(HEAD_DIM) * ATT_BOUND_MARGIN
                   * jnp.max(jnp.abs(attn_q_norm_w)) * jnp.max(jnp.abs(attn_k_norm_w)))
    score_bound = jnp.full((1, HEAD_DIM), score_bound, F32)
    y_attn, w_up_bf16, w_down_bf16 = _attention(proj, score_bound, w_mlp_up[0], w_mlp_down[0],
                                                batch, seq)
    mem_kv = _mem_kv(mem2d, norm_mem_w, w_cross_k[0], w_cross_v[0])
    xs = _out_cross(xs, y_ret, y_attn, w_out_bf16, norm_cross_w, w_cq_bf16, mem_kv, w_co_bf16, seq)
    xs = _mlp(xs, norm_mlp_w, w_up_bf16, w_down_bf16, norm_final_w[None, :])
    return xs.reshape(batch, seq, D_MODEL)
```
